```python
import jax, jax.numpy as jnp
from jax import lax
import numpy as np

D_MODEL = 1024
BATCH = 2
SEQ = 8192
DEPTH = 4
DEC_BATCH = 128
DEC_SEQ = 1
PAST_LEN = 8192
PAGE_SIZE = 128

N_MIXERS = 3
N_LAYERS_A = (DEPTH + 2) // 3
N_LAYERS_B = (DEPTH + 1) // 3
N_LAYERS_C = DEPTH // 3
NORM_EPS = 1e-6
LN_EPS = 1e-5

A_EXPAND = 2
D_A = A_EXPAND * D_MODEL
CHUNK = 128
A_GROUPS = 4

HEAD_DIM = 64
N_HEADS = D_MODEL // HEAD_DIM
N_KV_HEADS = N_HEADS // 8
Q_PER_KV = N_HEADS // N_KV_HEADS
WINDOW = 128
ROPE_THETA = 10000.0

D_RNN = D_MODEL
C_BLOCKS = 4
C_BLOCK_W = D_RNN // C_BLOCKS
CONV_W = 4
LRU_C = 8.0

kernel_name = 'hybrid_gmlp_swa_rglru_decode_step'


def rms_norm(x, g):
    xf = x.astype(jnp.float32)
    y = xf * lax.rsqrt(jnp.mean(xf * xf, axis=-1, keepdims=True) + NORM_EPS)
    return (y * g.astype(jnp.float32)).astype(x.dtype)


def layer_norm(x, g, b):
    xf = x.astype(jnp.float32)
    mu = jnp.mean(xf, axis=-1, keepdims=True)
    var = jnp.mean(jnp.square(xf - mu), axis=-1, keepdims=True)
    y = (xf - mu) * lax.rsqrt(var + LN_EPS)
    return (y * g.astype(jnp.float32) + b.astype(jnp.float32)).astype(x.dtype)


def rope(x, pos):
    half = HEAD_DIM // 2
    inv_freq = ROPE_THETA ** (-jnp.arange(half, dtype=jnp.float32) / half)
    ang = pos.astype(jnp.float32)[:, None] * inv_freq[None, :]
    cos = jnp.cos(ang)[:, None, :]
    sin = jnp.sin(ang)[:, None, :]
    xf = x.astype(jnp.float32)
    x1, x2 = xf[..., :half], xf[..., half:]
    return jnp.concatenate([x1 * cos - x2 * sin, x2 * cos + x1 * sin], axis=-1).astype(x.dtype)


def sink_softmax(scores, mask, sinks):
    s = jnp.where(mask, scores.astype(jnp.float32), -jnp.inf)
    sink = sinks.astype(jnp.float32)[..., None, None]
    m = jnp.maximum(jnp.max(s, axis=-1, keepdims=True), sink)
    e = jnp.exp(s - m)
    return e / (jnp.sum(e, axis=-1, keepdims=True) + jnp.exp(sink - m))


def mixer_a(h, w_in, ln_g, ln_b, w_s, b_s, w_out):
    bsz, t_len, _ = h.shape
    u, v, g = jnp.split(h @ w_in, 3, axis=-1)
    u = jax.nn.gelu(u)
    v = layer_norm(jax.nn.gelu(v), ln_g, ln_b)
    n_chunks = -(-t_len // CHUNK)
    pad = n_chunks * CHUNK - t_len
    vc = jnp.pad(v, ((0, 0), (0, pad), (0, 0))).reshape(bsz, n_chunks, CHUNK, A_GROUPS, D_A // A_GROUPS)
    causal = jnp.tril(jnp.ones((CHUNK, CHUNK), dtype=bool))
    ws = jnp.where(causal[None], w_s, 0.0)
    mixed = jnp.einsum('gts,bnsgc->bntgc', ws, vc) + b_s.T[None, None, :, :, None]
    mixed = mixed.reshape(bsz, n_chunks * CHUNK, D_A)[:, :t_len]
    out = (u * mixed * jax.nn.silu(g)) @ w_out
    return out, v


def project_b(h, w_in):
    bsz, t_len, _ = h.shape
    qd, kd = N_HEADS * HEAD_DIM, N_KV_HEADS * HEAD_DIM
    z = h @ w_in
    q = z[..., :qd].reshape(bsz, t_len, N_HEADS, HEAD_DIM)
    k = z[..., qd:qd + kd].reshape(bsz, t_len, N_KV_HEADS, HEAD_DIM)
    v = z[..., qd + kd:qd + 2 * kd].reshape(bsz, t_len, N_KV_HEADS, HEAD_DIM)
    g = z[..., qd + 2 * kd:]
    return q, k, v, g


def mixer_b_prompt(h, w_in, sinks, w_out):
    bsz, t_len, _ = h.shape
    q, k, v, g = project_b(h, w_in)
    pos = jnp.arange(t_len, dtype=jnp.int32)
    q, k = rope(q, pos), rope(k, pos)
    nb = t_len // WINDOW
    qb = q.reshape(bsz, nb, WINDOW, N_KV_HEADS, Q_PER_KV, HEAD_DIM)
    kb = k.reshape(bsz, nb, WINDOW, N_KV_HEADS, HEAD_DIM)
    vb = v.reshape(bsz, nb, WINDOW, N_KV_HEADS, HEAD_DIM)
    prev = lambda a: jnp.pad(a, ((0, 0), (1, 0), (0, 0), (0, 0), (0, 0)))[:, :-1]
    kk = jnp.concatenate([prev(kb), kb], axis=2)
    vv = jnp.concatenate([prev(vb), vb], axis=2)
    scores = jnp.einsum('bnqkgd,bnskd->bnkgqs', qb, kk) * (HEAD_DIM ** -0.5)
    blk = jnp.arange(nb)[:, None, None]
    qpos = blk * WINDOW + jnp.arange(WINDOW)[None, :, None]
    kpos = (blk - 1) * WINDOW + jnp.arange(2 * WINDOW)[None, None, :]
    mask = (kpos <= qpos) & (qpos - kpos <= WINDOW) & (kpos >= 0)
    p = sink_softmax(scores, mask[None, :, None, None], sinks.reshape(N_KV_HEADS, Q_PER_KV))
    o = jnp.einsum('bnkgqs,bnskd->bnqkgd', p.astype(vv.dtype), vv).reshape(bsz, t_len, N_HEADS * HEAD_DIM)
    out = (o * jax.nn.silu(g)) @ w_out
    keep = min(WINDOW, t_len)
    return out, k[:, t_len - keep:], v[:, t_len - keep:]


def mixer_b_sample(h, buf_k, buf_v, w_in, sinks, w_out):
    bsz, t_len, _ = h.shape
    wb = buf_k.shape[1]
    q, k, v, g = project_b(h, w_in)
    qpos = PAST_LEN + jnp.arange(t_len, dtype=jnp.int32)
    q, k = rope(q, qpos), rope(k, qpos)
    kk = jnp.concatenate([buf_k.astype(k.dtype), k], axis=1)
    vv = jnp.concatenate([buf_v.astype(v.dtype), v], axis=1)
    kpos = jnp.concatenate([PAST_LEN - wb + jnp.arange(wb, dtype=jnp.int32), qpos])
    qg = q.reshape(bsz, t_len, N_KV_HEADS, Q_PER_KV, HEAD_DIM)
    scores = jnp.einsum('btkgd,bskd->bkgts', qg, kk) * (HEAD_DIM ** -0.5)
    mask = (kpos[None, :] <= qpos[:, None]) & (qpos[:, None] - kpos[None, :] <= WINDOW)
    p = sink_softmax(scores, mask, sinks.reshape(N_KV_HEADS, Q_PER_KV))
    o = jnp.einsum('bkgts,bskd->btkgd', p.astype(vv.dtype), vv).reshape(bsz, t_len, N_HEADS * HEAD_DIM)
    out = (o * jax.nn.silu(g)) @ w_out
    keep = min(WINDOW, wb + t_len)
    return out, kk[:, wb + t_len - keep:], vv[:, wb + t_len - keep:]


def mixer_c(h, conv_buf, h0, pos, w_in, conv_w, conv_b, w_a, b_a, w_x, b_x, lam, w_out):
    bsz, t_len, _ = h.shape
    xr, g = jnp.split(h @ w_in, 2, axis=-1)
    xpad = jnp.concatenate([conv_buf.astype(xr.dtype), xr], axis=1)
    xc = conv_b + xpad[:, 0:t_len] * conv_w[0]
    for tap in range(1, CONV_W):
        xc = xc + xpad[:, tap:tap + t_len] * conv_w[tap]
    new_conv = xpad[:, t_len:]
    xblk = xc.reshape(bsz, t_len, C_BLOCKS, C_BLOCK_W)
    r = jax.nn.sigmoid(jnp.einsum('btnc,ncd->btnd', xblk, w_a).reshape(bsz, t_len, D_RNN) + b_a)
    i_gate = jax.nn.sigmoid(jnp.einsum('btnc,ncd->btnd', xblk, w_x).reshape(bsz, t_len, D_RNN) + b_x)
    log_a = -LRU_C * r.astype(jnp.float32) * jax.nn.softplus(-lam.astype(jnp.float32))
    a = jnp.exp(log_a)
    mult = jnp.sqrt(-jnp.expm1(2.0 * log_a))
    mult = jnp.where((pos == 0)[None, :, None], 1.0, mult)
    bterm = mult * (i_gate * xc).astype(jnp.float32)
    bterm = bterm.at[:, 0].add(a[:, 0] * h0.astype(jnp.float32))

    def combine(left, right):
        a1, b1 = left
        a2, b2 = right
        return a1 * a2, a2 * b1 + b2

    _, hs = lax.associative_scan(combine, (a, bterm), axis=1)
    out = (hs.astype(h.dtype) * jax.nn.silu(g)) @ w_out
    return out, new_conv, hs[:, -1]


def setup_inputs(seed: int = 0) -> dict:
    key = jax.random.key(seed)
    ks = jax.random.split(key, 32)
    f32 = jnp.float32

    def nrm(k, shape, scale):
        return jax.random.normal(k, shape, f32) * scale

    w_buf = min(WINDOW, PAST_LEN)
    qkvg = 2 * N_HEADS * HEAD_DIM + 2 * N_KV_HEADS * HEAD_DIM
    u = jax.random.uniform(ks[29], (N_LAYERS_C, D_RNN), f32, 0.9, 0.999)
    base = u ** (1.0 / LRU_C)
    c_lam = jnp.log(base) - jnp.log1p(-base)
    return {
        'x_prompt': nrm(ks[0], (BATCH, SEQ, D_MODEL), 1.0),
        'x_sample': nrm(ks[1], (DEC_BATCH, DEC_SEQ, D_MODEL), 1.0),
        'cache_b_k': nrm(ks[2], (N_LAYERS_B, DEC_BATCH, w_buf, N_KV_HEADS, HEAD_DIM), 1.0),
        'cache_b_v': nrm(ks[3], (N_LAYERS_B, DEC_BATCH, w_buf, N_KV_HEADS, HEAD_DIM), 1.0),
        'state_c_conv': nrm(ks[4], (N_LAYERS_C, DEC_BATCH, CONV_W - 1, D_RNN), 1.0),
        'state_c_h': nrm(ks[5], (N_LAYERS_C, DEC_BATCH, D_RNN), 0.5),
        'norm_pre': 1.0 + nrm(ks[6], (DEPTH, D_MODEL), 0.02),
        'norm_post': 1.0 + nrm(ks[7], (DEPTH, D_MODEL), 0.02),
        'a_w_in': nrm(ks[8], (N_LAYERS_A, D_MODEL, 3 * D_A), D_MODEL ** -0.5),
        'a_ln_g': 1.0 + nrm(ks[9], (N_LAYERS_A, D_A), 0.02),
        'a_ln_b': nrm(ks[10], (N_LAYERS_A, D_A), 0.02),
        'a_w_s': nrm(ks[11], (N_LAYERS_A, A_GROUPS, CHUNK, CHUNK), CHUNK ** -0.5),
        'a_b_s': 1.0 + nrm(ks[12], (N_LAYERS_A, A_GROUPS, CHUNK), 0.02),
        'a_w_out': nrm(ks[13], (N_LAYERS_A, D_A, D_MODEL), D_A ** -0.5),
        'b_w_in': nrm(ks[14], (N_LAYERS_B, D_MODEL, qkvg), D_MODEL ** -0.5),
        'b_sinks': nrm(ks[15], (N_LAYERS_B, N_HEADS), 1.0),
        'b_w_out': nrm(ks[16], (N_LAYERS_B, N_HEADS * HEAD_DIM, D_MODEL), (N_HEADS * HEAD_DIM) ** -0.5),
        'c_w_in': nrm(ks[17], (N_LAYERS_C, D_MODEL, 2 * D_RNN), D_MODEL ** -0.5),
        'c_conv_w': nrm(ks[18], (N_LAYERS_C, CONV_W, D_RNN), CONV_W ** -0.5),
        'c_conv_b': nrm(ks[19], (N_LAYERS_C, D_RNN), 0.02),
        'c_w_a': nrm(ks[20], (N_LAYERS_C, C_BLOCKS, C_BLOCK_W, C_BLOCK_W), C_BLOCK_W ** -0.5),
        'c_b_a': nrm(ks[21], (N_LAYERS_C, D_RNN), 0.02),
        'c_w_x': nrm(ks[22], (N_LAYERS_C, C_BLOCKS, C_BLOCK_W, C_BLOCK_W), C_BLOCK_W ** -0.5),
        'c_b_x': nrm(ks[23], (N_LAYERS_C, D_RNN), 0.02),
        'c_lam': c_lam,
        'c_w_out': nrm(ks[24], (N_LAYERS_C, D_RNN, D_MODEL), D_RNN ** -0.5),
    }


def reference(x_prompt, x_sample, cache_b_k, cache_b_v, state_c_conv, state_c_h,
              norm_pre, norm_post,
              a_w_in, a_ln_g, a_ln_b, a_w_s, a_b_s, a_w_out,
              b_w_in, b_sinks, b_w_out,
              c_w_in, c_conv_w, c_conv_b, c_w_a, c_b_a, c_w_x, c_b_x, c_lam, c_w_out):
    xp, xs = x_prompt, x_sample
    pos_p = jnp.arange(xp.shape[1], dtype=jnp.int32)
    pos_s = PAST_LEN + jnp.arange(xs.shape[1], dtype=jnp.int32)
    a_v_s = []
    b_kp, b_vp, b_ks, b_vs = [], [], [], []
    c_cp, c_hp, c_cs, c_hs = [], [], [], []
    for i in range(DEPTH):
        kind, j = i % N_MIXERS, i // N_MIXERS
        hp = rms_norm(xp, norm_pre[i])
        hs = rms_norm(xs, norm_pre[i])
        if kind == 0:
            yp, _ = mixer_a(hp, a_w_in[j], a_ln_g[j], a_ln_b[j], a_w_s[j], a_b_s[j], a_w_out[j])
            ys, vs = mixer_a(hs, a_w_in[j], a_ln_g[j], a_ln_b[j], a_w_s[j], a_b_s[j], a_w_out[j])
            a_v_s.append(vs)
        elif kind == 1:
            yp, kp, vp = mixer_b_prompt(hp, b_w_in[j], b_sinks[j], b_w_out[j])
            ys, ks_, vs_ = mixer_b_sample(hs, cache_b_k[j], cache_b_v[j], b_w_in[j], b_sinks[j], b_w_out[j])
            b_kp.append(kp); b_vp.append(vp); b_ks.append(ks_); b_vs.append(vs_)
        else:
            yp, cp, hlp = mixer_c(hp, jnp.zeros((hp.shape[0], CONV_W - 1, D_RNN), hp.dtype),
                                  jnp.zeros((hp.shape[0], D_RNN), jnp.float32), pos_p,
                                  c_w_in[j], c_conv_w[j], c_conv_b[j], c_w_a[j], c_b_a[j],
                                  c_w_x[j], c_b_x[j], c_lam[j], c_w_out[j])
            ys, cs, hls = mixer_c(hs, state_c_conv[j], state_c_h[j], pos_s,
                                  c_w_in[j], c_conv_w[j], c_conv_b[j], c_w_a[j], c_b_a[j],
                                  c_w_x[j], c_b_x[j], c_lam[j], c_w_out[j])
            c_cp.append(cp); c_hp.append(hlp); c_cs.append(cs); c_hs.append(hls)
        xp = xp + rms_norm(yp, norm_post[i])
        xs = xs + rms_norm(ys, norm_post[i])
    new_a_v_sample = jnp.stack(a_v_s)
    new_b_k_prompt = jnp.stack(b_kp)
    new_b_v_prompt = jnp.stack(b_vp)
    new_b_k_sample = jnp.stack(b_ks)
    new_b_v_sample = jnp.stack(b_vs)
    new_c_conv_prompt = jnp.stack(c_cp)
    new_c_h_prompt = jnp.stack(c_hp)
    new_c_conv_sample = jnp.stack(c_cs)
    new_c_h_sample = jnp.stack(c_hs)
    return (xp, xs, new_a_v_sample, new_b_k_prompt, new_b_v_prompt, new_b_k_sample, new_b_v_sample,
            new_c_conv_prompt, new_c_h_prompt, new_c_conv_sample, new_c_h_sample)
```

```python
import functools

import jax
import jax.numpy as jnp
import numpy as np
from jax import lax
from jax.experimental import pallas as pl
from jax.experimental.pallas import tpu as pltpu

D_MODEL = 1024
DEPTH = 4
N_MIXERS = 3
NORM_EPS = 1e-6
LN_EPS = 1e-5

D_A = 2 * D_MODEL
CHUNK = 128
A_GROUPS = 4
A_GROUP_W = D_A // A_GROUPS

HEAD_DIM = 64
N_HEADS = D_MODEL // HEAD_DIM
N_KV_HEADS = N_HEADS // 8
Q_PER_KV = N_HEADS // N_KV_HEADS
WINDOW = 128
ROPE_THETA = 10000.0
QD = N_HEADS * HEAD_DIM
KD = N_KV_HEADS * HEAD_DIM

D_RNN = D_MODEL
C_BLOCKS = 4
C_BLOCK_W = D_RNN // C_BLOCKS
CONV_W = 4
LRU_C = 8.0

PAST_LEN = 8192

V7X_LANES = 128
V7X_SUBLANES = 8
V7X_VMEM_BYTES = 64 * 1024 * 1024
VMEM_LIMIT_BYTES = V7X_VMEM_BYTES - 8 * 1024 * 1024

ROW_TILE = 256
SAMPLE_BATCH_TILE = 16

BF16 = jnp.bfloat16
F32 = jnp.float32
SQRT_2_OVER_PI = np.float32(np.sqrt(2.0 / np.pi))


def _dot(a, b):
    return jnp.dot(a, b, preferred_element_type=F32)


def _rms(x, g):
    return x * lax.rsqrt(jnp.mean(x * x, axis=-1, keepdims=True) + NORM_EPS) * g


def _gelu(x):
    return x * (0.5 * (1.0 + jnp.tanh(SQRT_2_OVER_PI * (x + 0.044715 * (x * x * x)))))


def _sigmoid(x):
    return 1.0 / (1.0 + jnp.exp(-x))


def _silu(x):
    return x * _sigmoid(x)


def _softplus(x):
    return jnp.maximum(x, 0.0) + jnp.log1p(jnp.exp(-jnp.abs(x)))


def _const_spec(shape):
    zeros = (0,) * len(shape)
    return pl.BlockSpec(shape, lambda i: zeros, pipeline_mode=pl.Buffered(1))


def _params():
    return pltpu.CompilerParams(dimension_semantics=("arbitrary",),
                                vmem_limit_bytes=VMEM_LIMIT_BYTES)


def _mixer_a_kernel(x_ref, npre_ref, npost_ref, win_ref, lng_ref, lnb_ref, ws_ref, bst_ref,
                    wout_ref, o_ref, *rest, tm, sample):
    x = x_ref[...]
    h = _rms(x, npre_ref[...]).astype(BF16)
    v = _gelu(_dot(h, win_ref[:, D_A:2 * D_A]))
    d = v - jnp.mean(v, axis=-1, keepdims=True)
    var = jnp.mean(d * d, axis=-1, keepdims=True)
    vn = d * lax.rsqrt(var + LN_EPS) * lng_ref[...] + lnb_ref[...]
    if sample:
        rest[0][...] = vn
    else:
        vnb = vn.astype(BF16)
        row = lax.broadcasted_iota(jnp.int32, (CHUNK, CHUNK), 0)
        col = lax.broadcasted_iota(jnp.int32, (CHUNK, CHUNK), 1)
        causal = row >= col
    acc = jnp.zeros((tm, D_MODEL), F32)
    for g in range(A_GROUPS):
        c0 = g * A_GROUP_W
        u = _gelu(_dot(h, win_ref[:, c0:c0 + A_GROUP_W]))
        gate = _silu(_dot(h, win_ref[:, 2 * D_A + c0:2 * D_A + c0 + A_GROUP_W]))
        if sample:
            mixed = ws_ref[g][0:1, 0:1] * vn[:, c0:c0 + A_GROUP_W] + bst_ref[0:1, g:g + 1]
        else:
            wsg = jnp.where(causal, ws_ref[g], 0.0).astype(BF16)
            bias = bst_ref[:, g:g + 1]
            mixed = jnp.concatenate(
                [_dot(wsg, vnb[c * CHUNK:(c + 1) * CHUNK, c0:c0 + A_GROUP_W]) + bias
                 for c in range(tm // CHUNK)], axis=0)
        z = (u * mixed * gate).astype(BF16)
        acc = acc + _dot(z, wout_ref[c0:c0 + A_GROUP_W, :])
    o_ref[...] = x + _rms(acc, npost_ref[...])


def _mixer_a(x, npre, npost, win, lng, lnb, ws, bst, wout, *, tm, sample):
    n = x.shape[0]
    row_spec = pl.BlockSpec((tm, D_MODEL), lambda i: (i, 0))
    out_shape = [jax.ShapeDtypeStruct((n, D_MODEL), F32)]
    out_specs = [row_spec]
    if sample:
        out_shape.append(jax.ShapeDtypeStruct((n, D_A), F32))
        out_specs.append(pl.BlockSpec((tm, D_A), lambda i: (i, 0)))
    return pl.pallas_call(
        functools.partial(_mixer_a_kernel, tm=tm, sample=sample),
        grid=(n // tm,),
        in_specs=[row_spec, _const_spec((1, D_MODEL)), _const_spec((1, D_MODEL)),
                  _const_spec((D_MODEL, 3 * D_A)), _const_spec((1, D_A)), _const_spec((1, D_A)),
                  _const_spec((A_GROUPS, CHUNK, CHUNK)), _const_spec((CHUNK, A_GROUPS)),
                  _const_spec((D_A, D_MODEL))],
        out_specs=out_specs, out_shape=out_shape,
        compiler_params=_params(),
        name="mixer_a_sample" if sample else "mixer_a_prompt",
    )(x, npre, npost, win, lng, lnb, ws, bst, wout)


def _rope_tables(positions):
    half = HEAD_DIM // 2
    inv_freq = ROPE_THETA ** (-jnp.arange(half, dtype=F32) / half)
    ang = positions.astype(F32)[:, None] * inv_freq[None, :]
    cos, sin = jnp.cos(ang), jnp.sin(ang)
    reps = V7X_LANES // HEAD_DIM
    cos_t = jnp.tile(jnp.concatenate([cos, cos], axis=-1), (1, reps))
    sin_t = jnp.tile(jnp.concatenate([-sin, sin], axis=-1), (1, reps))
    return cos_t, sin_t


def _rope(x, cos, sin_signed):
    lane = lax.broadcasted_iota(jnp.int32, x.shape, 1)
    first_half = (lane % HEAD_DIM) < (HEAD_DIM // 2)
    rot = jnp.where(first_half, pltpu.roll(x, V7X_LANES - HEAD_DIM // 2, 1),
                    pltpu.roll(x, HEAD_DIM // 2, 1))
    return x * cos + rot * sin_signed


def _dup_kv_halves(x, lane):
    swapped = pltpu.roll(x, HEAD_DIM, 1)
    low = lane < HEAD_DIM
    return jnp.where(low, x, swapped), jnp.where(low, swapped, x)


def _mixer_b_prompt_kernel(x_ref, npre_ref, npost_ref, win_ref, sinks_ref, cos_ref, sin_ref,
                           wout_ref, o_ref, kout_ref, vout_ref,
                           k0_ref, k1_ref, v0_ref, v1_ref, att_ref, *, tm, tiles_per_seq):
    first = (pl.program_id(0) % tiles_per_seq) == 0

    @pl.when(first)
    def _():
        zeros = jnp.zeros((WINDOW, V7X_LANES), BF16)
        for ref in (k0_ref, k1_ref, v0_ref, v1_ref):
            ref[0:WINDOW, :] = zeros

    x = x_ref[...]
    h = _rms(x, npre_ref[...]).astype(BF16)
    lane = lax.broadcasted_iota(jnp.int32, (tm, V7X_LANES), 1)
    cos, sin = cos_ref[...], sin_ref[...]

    k = _rope(_dot(h, win_ref[:, QD:QD + KD]), cos, sin)
    v = _dot(h, win_ref[:, QD + KD:QD + 2 * KD])
    kout_ref[...] = k[tm - WINDOW:, :]
    vout_ref[...] = v[tm - WINDOW:, :]
    kd0, kd1 = _dup_kv_halves(k, lane)
    vd0, vd1 = _dup_kv_halves(v, lane)
    k0_ref[WINDOW:, :] = kd0.astype(BF16)
    k1_ref[WINDOW:, :] = kd1.astype(BF16)
    v0_ref[WINDOW:, :] = vd0.astype(BF16)
    v1_ref[WINDOW:, :] = vd1.astype(BF16)

    qi = lax.broadcasted_iota(jnp.int32, (2 * WINDOW, 2 * WINDOW), 0) % WINDOW
    ci = lax.broadcasted_iota(jnp.int32, (2 * WINDOW, 2 * WINDOW), 1)
    band = (ci >= qi) & (ci <= qi + WINDOW)
    upper_rows = lax.broadcasted_iota(jnp.int32, (2 * WINDOW, 1), 0) < WINDOW
    low_lanes = lax.broadcasted_iota(jnp.int32, (WINDOW, V7X_LANES), 1) < HEAD_DIM
    scale = HEAD_DIM ** -0.5

    q = _dot(h, win_ref[:, :QD])
    for j in range(tm // WINDOW):
        rows = slice(j * WINDOW, (j + 1) * WINDOW)
        if j == 0:
            mask = band & (ci >= jnp.where(first, WINDOW, 0))
        else:
            mask = band
        for p in range(N_HEADS // 2):
            kvh = (2 * p) // Q_PER_KV
            kref, vref = (k0_ref, v0_ref) if kvh == 0 else (k1_ref, v1_ref)
            qp = q[rows, p * V7X_LANES:(p + 1) * V7X_LANES]
            qp = _rope(qp, cos[rows], sin[rows]) * scale
            qs = jnp.concatenate([jnp.where(low_lanes, qp, 0.0), jnp.where(low_lanes, 0.0, qp)],
                                 axis=0).astype(BF16)
            kk = kref[j * WINDOW:(j + 2) * WINDOW, :]
            s = lax.dot_general(qs, kk, (((1,), (1,)), ((), ())), preferred_element_type=F32)
            s = jnp.where(mask, s, -jnp.inf)
            sink = jnp.where(upper_rows, sinks_ref[0, 2 * p], sinks_ref[0, 2 * p + 1])
            m = jnp.maximum(jnp.max(s, axis=-1, keepdims=True), sink)
            e = jnp.exp(s - m)
            inv = 1.0 / (jnp.sum(e, axis=-1, keepdims=True) + jnp.exp(sink - m))
            r = _dot((e * inv).astype(BF16), vref[j * WINDOW:(j + 2) * WINDOW, :])
            att_ref[rows, p * V7X_LANES:(p + 1) * V7X_LANES] = jnp.where(
                low_lanes, r[:WINDOW], r[WINDOW:])

    for ref in (k0_ref, k1_ref, v0_ref, v1_ref):
        ref[0:WINDOW, :] = ref[tm:tm + WINDOW, :]

    gate = _silu(_dot(h, win_ref[:, QD + 2 * KD:]))
    y = _dot((att_ref[...] * gate).astype(BF16), wout_ref[...])
    o_ref[...] = x + _rms(y, npost_ref[...])


def _mixer_b_prompt(x, npre, npost, win, sinks, cos, sin, wout, *, tm, seq):
    n = x.shape[0]
    tiles_per_seq = seq // tm
    n_seq = n // seq
    row_spec = pl.BlockSpec((tm, D_MODEL), lambda i: (i, 0))
    rope_spec = pl.BlockSpec((tm, V7X_LANES), lambda i: (i % tiles_per_seq, 0))
    kv_spec = pl.BlockSpec((WINDOW, KD), lambda i: (i // tiles_per_seq, 0))
    return pl.pallas_call(
        functools.partial(_mixer_b_prompt_kernel, tm=tm, tiles_per_seq=tiles_per_seq),
        grid=(n // tm,),
        in_specs=[row_spec, _const_spec((1, D_MODEL)), _const_spec((1, D_MODEL)),
                  _const_spec((D_MODEL, 2 * QD + 2 * KD)),
                  pl.BlockSpec(memory_space=pltpu.SMEM),
                  rope_spec, rope_spec, _const_spec((QD, D_MODEL))],
        out_specs=[row_spec, kv_spec, kv_spec],
        out_shape=[jax.ShapeDtypeStruct((n, D_MODEL), F32),
                   jax.ShapeDtypeStruct((n_seq * WINDOW, KD), F32),
                   jax.ShapeDtypeStruct((n_seq * WINDOW, KD), F32)],
        scratch_shapes=[pltpu.VMEM((tm + WINDOW, V7X_LANES), BF16)] * 4
        + [pltpu.VMEM((tm, QD), F32)],
        compiler_params=_params(),
        name="mixer_b_prompt",
    )(x, npre, npost, win, sinks, cos, sin, wout)


def _mixer_b_sample_proj_kernel(x_ref, npre_ref, win_ref, cos_ref, sin_ref,
                                q_ref, k_ref, v_ref, gate_ref):
    n = x_ref.shape[0]
    h = _rms(x_ref[...], npre_ref[...]).astype(BF16)
    cos, sin = cos_ref[...], sin_ref[...]
    scale = HEAD_DIM ** -0.5
    low_lanes = lax.broadcasted_iota(jnp.int32, (n, V7X_LANES), 1) < HEAD_DIM
    for p in range(N_HEADS // 2):
        qp = _rope(_dot(h, win_ref[:, p * V7X_LANES:(p + 1) * V7X_LANES]), cos, sin) * scale
        swapped = pltpu.roll(qp, HEAD_DIM, 1)
        if (2 * p) // Q_PER_KV == 0:
            even, odd = jnp.where(low_lanes, qp, 0.0), jnp.where(low_lanes, swapped, 0.0)
        else:
            even, odd = jnp.where(low_lanes, 0.0, swapped), jnp.where(low_lanes, 0.0, qp)
        q_ref[pl.ds(2 * p, n, stride=N_HEADS), :] = even
        q_ref[pl.ds(2 * p + 1, n, stride=N_HEADS), :] = odd
    k_ref[...] = _rope(_dot(h, win_ref[:, QD:QD + KD]), cos, sin)
    v_ref[...] = _dot(h, win_ref[:, QD + KD:QD + 2 * KD])
    gate_ref[...] = _silu(_dot(h, win_ref[:, QD + 2 * KD:]))


def _mixer_b_sample_proj(x, npre, win, cos_row, sin_row):
    n = x.shape[0]
    return pl.pallas_call(
        _mixer_b_sample_proj_kernel,
        out_shape=[jax.ShapeDtypeStruct((n * N_HEADS, V7X_LANES), F32),
                   jax.ShapeDtypeStruct((n, KD), F32),
                   jax.ShapeDtypeStruct((n, KD), F32), jax.ShapeDtypeStruct((n, QD), F32)],
        compiler_params=pltpu.CompilerParams(vmem_limit_bytes=VMEM_LIMIT_BYTES),
        name="mixer_b_sample_proj",
    )(x, npre, win, cos_row, sin_row)


def _mixer_b_sample_attn_kernel(q_ref, kc_ref, vc_ref, kn_ref, vn_ref, sinks_ref,
                                o_ref, ko_ref, vo_ref):
    q = q_ref[...]
    kc, vc = kc_ref[...], vc_ref[...]
    kn, vn = kn_ref[...], vn_ref[...]
    qb = q.astype(BF16)
    s_c = jnp.einsum("bhl,bsl->bhs", qb, kc.astype(BF16), preferred_element_type=F32)
    s_n = jnp.sum(qb.astype(F32) * kn.astype(BF16).astype(F32), axis=-1, keepdims=True)
    sink = sinks_ref[...][None]
    m = jnp.maximum(jnp.maximum(jnp.max(s_c, axis=-1, keepdims=True), s_n), sink)
    e_c = jnp.exp(s_c - m)
    e_n = jnp.exp(s_n - m)
    inv = 1.0 / (jnp.sum(e_c, axis=-1, keepdims=True) + e_n + jnp.exp(sink - m))
    o = jnp.einsum("bhs,bsl->bhl", (e_c * inv).astype(BF16), vc.astype(BF16),
                   preferred_element_type=F32)
    p_n = (e_n * inv).astype(BF16).astype(F32)
    o_ref[...] = o + p_n * vn.astype(BF16).astype(F32)
    w = kc.shape[1]
    ko_ref[:, 0:w - 1, :] = kc[:, 1:w, :]
    ko_ref[:, w - 1:w, :] = kn
    vo_ref[:, 0:w - 1, :] = vc[:, 1:w, :]
    vo_ref[:, w - 1:w, :] = vn


def _mixer_b_sample_attn(qm, kc, vc, kn, vn, sinks_col, *, bt):
    b, w, _ = kc.shape
    q_spec = pl.BlockSpec((bt, N_HEADS, V7X_LANES), lambda i: (i, 0, 0))
    c_spec = pl.BlockSpec((bt, w, KD), lambda i: (i, 0, 0))
    n_spec = pl.BlockSpec((bt, 1, KD), lambda i: (i, 0, 0))
    return pl.pallas_call(
        _mixer_b_sample_attn_kernel,
        grid=(b // bt,),
        in_specs=[q_spec, c_spec, c_spec, n_spec, n_spec,
                  pl.BlockSpec((N_HEADS, 1), lambda i: (0, 0))],
        out_specs=[q_spec, c_spec, c_spec],
        out_shape=[jax.ShapeDtypeStruct((b, N_HEADS, V7X_LANES), F32),
                   jax.ShapeDtypeStruct((b, w, KD), F32), jax.ShapeDtypeStruct((b, w, KD), F32)],
        compiler_params=_params(),
        name="mixer_b_sample_attn",
    )(qm, kc, vc, kn, vn, sinks_col)


def _mixer_b_sample_out_kernel(x_ref, att_ref, gate_ref, wout_ref, npost_ref, o_ref):
    n = x_ref.shape[0]
    low_lanes = lax.broadcasted_iota(jnp.int32, (n, V7X_LANES), 1) < HEAD_DIM
    pairs = []
    for p in range(N_HEADS // 2):
        even = att_ref[pl.ds(2 * p, n, stride=N_HEADS), :]
        odd = att_ref[pl.ds(2 * p + 1, n, stride=N_HEADS), :]
        if (2 * p) // Q_PER_KV == 0:
            odd = pltpu.roll(odd, HEAD_DIM, 1)
        else:
            even = pltpu.roll(even, HEAD_DIM, 1)
        pairs.append(jnp.where(low_lanes, even, odd))
    att = jnp.concatenate(pairs, axis=1)
    y = _dot((att * gate_ref[...]).astype(BF16), wout_ref[...])
    o_ref[...] = x_ref[...] + _rms(y, npost_ref[...])


def _mixer_b_sample_out(x, att, gate, wout, npost):
    return pl.pallas_call(
        _mixer_b_sample_out_kernel,
        out_shape=jax.ShapeDtypeStruct(x.shape, F32),
        compiler_params=pltpu.CompilerParams(vmem_limit_bytes=VMEM_LIMIT_BYTES),
        name="mixer_b_sample_out",
    )(x, att, gate, wout, npost)


def _rglru_gates(xc, wa_ref, ba_ref, wx_ref, bx_ref, lam_ref):
    xcb = xc.astype(BF16)
    ra, rx = [], []
    for blk in range(C_BLOCKS):
        cols = slice(blk * C_BLOCK_W, (blk + 1) * C_BLOCK_W)
        ra.append(_dot(xcb[:, cols], wa_ref[blk]))
        rx.append(_dot(xcb[:, cols], wx_ref[blk]))
    r = _sigmoid(jnp.concatenate(ra, axis=-1) + ba_ref[...])
    i_gate = _sigmoid(jnp.concatenate(rx, axis=-1) + bx_ref[...])
    log_a = (-LRU_C * r) * _softplus(-lam_ref[...])
    a = jnp.exp(log_a)
    mult = jnp.sqrt(-jnp.tanh(log_a) * (a * a + 1.0))
    return a, mult, i_gate * xc


def _mixer_c_prompt_kernel(x_ref, npre_ref, npost_ref, win_ref, cw_ref, cb_ref, wa_ref, ba_ref,
                           wx_ref, bx_ref, lam_ref, wout_ref, o_ref, conv_ref, hlast_ref,
                           xpad_ref, h_ref, *, tm, tiles_per_seq):
    first = (pl.program_id(0) % tiles_per_seq) == 0

    @pl.when(first)
    def _():
        xpad_ref[0:V7X_SUBLANES, :] = jnp.zeros((V7X_SUBLANES, D_RNN), F32)
        h_ref[...] = jnp.zeros((V7X_SUBLANES, D_RNN), F32)

    x = x_ref[...]
    h = _rms(x, npre_ref[...]).astype(BF16)
    xr = _dot(h, win_ref[:, :D_RNN])
    xpad_ref[V7X_SUBLANES:, :] = xr
    conv_ref[...] = xr[tm - V7X_SUBLANES:, :]
    cw = cw_ref[...]
    xc = cb_ref[...] + xr * cw[CONV_W - 1:CONV_W, :]
    for tap in range(CONV_W - 1):
        back = CONV_W - 1 - tap
        xc = xc + xpad_ref[V7X_SUBLANES - back:V7X_SUBLANES - back + tm, :] * cw[tap:tap + 1, :]
    xpad_ref[0:V7X_SUBLANES, :] = xr[tm - V7X_SUBLANES:, :]

    a, mult, gx = _rglru_gates(xc, wa_ref, ba_ref, wx_ref, bx_ref, lam_ref)
    row = lax.broadcasted_iota(jnp.int32, (tm, 1), 0)
    mult = jnp.where(first & (row == 0), 1.0, mult)
    b = mult * gx

    sub = row % V7X_SUBLANES
    for dist in (1, 2, 4):
        keep = sub >= dist
        a_prev = pltpu.roll(a, dist, 0)
        b_prev = pltpu.roll(b, dist, 0)
        b = jnp.where(keep, a * b_prev + b, b)
        a = jnp.where(keep, a * a_prev, a)
    carry = h_ref[V7X_SUBLANES - 1:V7X_SUBLANES, :]
    groups = []
    for gi in range(tm // V7X_SUBLANES):
        rows = slice(gi * V7X_SUBLANES, (gi + 1) * V7X_SUBLANES)
        hs_g = a[rows] * carry + b[rows]
        carry = hs_g[V7X_SUBLANES - 1:V7X_SUBLANES, :]
        groups.append(hs_g)
    h_ref[...] = groups[-1]
    hlast_ref[...] = groups[-1]
    hs = jnp.concatenate(groups, axis=0)

    gate = _silu(_dot(h, win_ref[:, D_RNN:]))
    y = _dot((hs * gate).astype(BF16), wout_ref[...])
    o_ref[...] = x + _rms(y, npost_ref[...])


def _mixer_c_prompt(x, npre, npost, win, cw, cb, wa, ba, wx, bx, lam, wout, *, tm, seq):
    n = x.shape[0]
    tiles_per_seq = seq // tm
    n_seq = n // seq
    row_spec = pl.BlockSpec((tm, D_MODEL), lambda i: (i, 0))
    tail_spec = pl.BlockSpec((V7X_SUBLANES, D_RNN), lambda i: (i // tiles_per_seq, 0))
    vec = _const_spec((1, D_RNN))
    blk = _const_spec((C_BLOCKS, C_BLOCK_W, C_BLOCK_W))
    return pl.pallas_call(
        functools.partial(_mixer_c_prompt_kernel, tm=tm, tiles_per_seq=tiles_per_seq),
        grid=(n // tm,),
        in_specs=[row_spec, vec, vec, _const_spec((D_MODEL, 2 * D_RNN)),
                  _const_spec((CONV_W, D_RNN)), vec, blk, vec, blk, vec, vec,
                  _const_spec((D_RNN, D_MODEL))],
        out_specs=[row_spec, tail_spec, tail_spec],
        out_shape=[jax.ShapeDtypeStruct((n, D_MODEL), F32),
                   jax.ShapeDtypeStruct((n_seq * V7X_SUBLANES, D_RNN), F32),
                   jax.ShapeDtypeStruct((n_seq * V7X_SUBLANES, D_RNN), F32)],
        scratch_shapes=[pltpu.VMEM((tm + V7X_SUBLANES, D_RNN), F32),
                        pltpu.VMEM((V7X_SUBLANES, D_RNN), F32)],
        compiler_params=_params(),
        name="mixer_c_prompt",
    )(x, npre, npost, win, cw, cb, wa, ba, wx, bx, lam, wout)


def _mixer_c_sample_kernel(x_ref, npre_ref, npost_ref, win_ref, cw_ref, cb_ref, wa_ref, ba_ref,
                           wx_ref, bx_ref, lam_ref, wout_ref, conv_ref, h0_ref,
                           o_ref, conv_out_ref, h_out_ref):
    x = x_ref[...]
    h = _rms(x, npre_ref[...]).astype(BF16)
    xr = _dot(h, win_ref[:, :D_RNN])
    cw = cw_ref[...]
    xc = cb_ref[...] + xr * cw[CONV_W - 1:CONV_W, :]
    for tap in range(CONV_W - 1):
        xc = xc + conv_ref[tap] * cw[tap:tap + 1, :]
        if tap > 0:
            conv_out_ref[tap - 1] = conv_ref[tap]
    conv_out_ref[CONV_W - 2] = xr
    a, mult, gx = _rglru_gates(xc, wa_ref, ba_ref, wx_ref, bx_ref, lam_ref)
    hs = a * h0_ref[...] + mult * gx
    h_out_ref[...] = hs
    gate = _silu(_dot(h, win_ref[:, D_RNN:]))
    y = _dot((hs * gate).astype(BF16), wout_ref[...])
    o_ref[...] = x + _rms(y, npost_ref[...])


def _mixer_c_sample(x, npre, npost, win, cw, cb, wa, ba, wx, bx, lam, wout, conv_t, h0):
    n = x.shape[0]
    return pl.pallas_call(
        _mixer_c_sample_kernel,
        out_shape=[jax.ShapeDtypeStruct((n, D_MODEL), F32),
                   jax.ShapeDtypeStruct((CONV_W - 1, n, D_RNN), F32),
                   jax.ShapeDtypeStruct((n, D_RNN), F32)],
        compiler_params=pltpu.CompilerParams(vmem_limit_bytes=VMEM_LIMIT_BYTES),
        name="mixer_c_sample",
    )(x, npre, npost, win, cw, cb, wa, ba, wx, bx, lam, wout, conv_t, h0)


def kernel(x_prompt, x_sample, cache_b_k, cache_b_v, state_c_conv, state_c_h, norm_pre, norm_post,
           a_w_in, a_ln_g, a_ln_b, a_w_s, a_b_s, a_w_out, b_w_in, b_sinks, b_w_out, c_w_in,
           c_conv_w, c_conv_b, c_w_a, c_b_a, c_w_x, c_b_x, c_lam, c_w_out):
    batch, seq, _ = x_prompt.shape
    dec_batch, dec_seq, _ = x_sample.shape
    past_len = PAST_LEN
    w_buf = cache_b_k.shape[2]
    assert dec_seq == 1 and w_buf == WINDOW and seq % ROW_TILE == 0
    assert ROW_TILE % CHUNK == 0 and ROW_TILE % WINDOW == 0

    xp = x_prompt.reshape(batch * seq, D_MODEL)
    xs = x_sample.reshape(dec_batch, D_MODEL)
    row = lambda a: a.reshape(1, -1)

    pos = jnp.arange(max(seq, past_len + 1), dtype=jnp.int32)
    cos_t, sin_t = _rope_tables(pos)
    cos_p, sin_p = cos_t[:seq], sin_t[:seq]
    cos_s, sin_s = cos_t[past_len:past_len + 1], sin_t[past_len:past_len + 1]

    a_v_s = []
    b_kp, b_vp, b_ks, b_vs = [], [], [], []
    c_cp, c_hp, c_cs, c_hs = [], [], [], []
    for i in range(DEPTH):
        kind, j = i % N_MIXERS, i // N_MIXERS
        npre, npost = row(norm_pre[i]), row(norm_post[i])
        if kind == 0:
            args = (npre, npost, a_w_in[j].astype(BF16), row(a_ln_g[j]), row(a_ln_b[j]),
                    a_w_s[j], a_b_s[j].T, a_w_out[j].astype(BF16))
            (xp,) = _mixer_a(xp, *args, tm=ROW_TILE, sample=False)
            xs, vs = _mixer_a(xs, *args, tm=dec_batch, sample=True)
            a_v_s.append(vs.reshape(dec_batch, dec_seq, D_A))
        elif kind == 1:
            win, wout = b_w_in[j].astype(BF16), b_w_out[j].astype(BF16)
            xp, kp, vp = _mixer_b_prompt(xp, npre, npost, win, row(b_sinks[j]), cos_p, sin_p, wout,
                                         tm=ROW_TILE, seq=seq)
            b_kp.append(kp.reshape(batch, WINDOW, N_KV_HEADS, HEAD_DIM))
            b_vp.append(vp.reshape(batch, WINDOW, N_KV_HEADS, HEAD_DIM))

            qm, kn, vn, gate = _mixer_b_sample_proj(xs, npre, win, cos_s, sin_s)
            qm = qm.reshape(dec_batch, N_HEADS, V7X_LANES)
            om, ks_new, vs_new = _mixer_b_sample_attn(
                qm, cache_b_k[j].reshape(dec_batch, w_buf, KD),
                cache_b_v[j].reshape(dec_batch, w_buf, KD),
                kn.reshape(dec_batch, 1, KD), vn.reshape(dec_batch, 1, KD),
                b_sinks[j].reshape(N_HEADS, 1), bt=SAMPLE_BATCH_TILE)
            xs = _mixer_b_sample_out(xs, om.reshape(dec_batch * N_HEADS, V7X_LANES), gate, wout,
                                     npost)
            b_ks.append(ks_new.reshape(dec_batch, WINDOW, N_KV_HEADS, HEAD_DIM))
            b_vs.append(vs_new.reshape(dec_batch, WINDOW, N_KV_HEADS, HEAD_DIM))
        else:
            args = (npre, npost, c_w_in[j].astype(BF16), c_conv_w[j], row(c_conv_b[j]),
                    c_w_a[j].astype(BF16), row(c_b_a[j]), c_w_x[j].astype(BF16), row(c_b_x[j]),
                    row(c_lam[j]), c_w_out[j].astype(BF16))
            xp, conv_tail, h_tail = _mixer_c_prompt(xp, *args, tm=ROW_TILE, seq=seq)
            c_cp.append(conv_tail.reshape(batch, V7X_SUBLANES, D_RNN)[:, V7X_SUBLANES - (CONV_W - 1):])
            c_hp.append(h_tail.reshape(batch, V7X_SUBLANES, D_RNN)[:, V7X_SUBLANES - 1])
            xs, conv_new, h_new = _mixer_c_sample(
                xs, *args, jnp.transpose(state_c_conv[j], (1, 0, 2)), state_c_h[j])
            c_cs.append(jnp.transpose(conv_new, (1, 0, 2)))
            c_hs.append(h_new)

    return (xp.reshape(batch, seq, D_MODEL), xs.reshape(dec_batch, dec_seq, D_MODEL),
            jnp.stack(a_v_s), jnp.stack(b_kp), jnp.stack(b_vp), jnp.stack(b_ks), jnp.stack(b_vs),
            jnp.stack(c_cp), jnp.stack(c_hp), jnp.stack(c_cs), jnp.stack(c_hs))
```

```python
import functools

import jax
import jax.numpy as jnp
import numpy as np
from jax import lax
from jax.experimental import pallas as pl
from jax.experimental.pallas import tpu as pltpu

D_MODEL = 1024
DEPTH = 4
N_MIXERS = 3
NORM_EPS = 1e-6
LN_EPS = 1e-5

D_A = 2 * D_MODEL
CHUNK = 128
A_GROUPS = 4
A_GROUP_W = D_A // A_GROUPS

HEAD_DIM = 64
N_HEADS = D_MODEL // HEAD_DIM
N_KV_HEADS = N_HEADS // 8
Q_PER_KV = N_HEADS // N_KV_HEADS
WINDOW = 128
ROPE_THETA = 10000.0
QD = N_HEADS * HEAD_DIM
KD = N_KV_HEADS * HEAD_DIM

D_RNN = D_MODEL
C_BLOCKS = 4
C_BLOCK_W = D_RNN // C_BLOCKS
CONV_W = 4
LRU_C = 8.0

PAST_LEN = 8192

V7X_LANES = 128
V7X_SUBLANES = 8
V7X_VMEM_BYTES = 64 * 1024 * 1024
VMEM_LIMIT_BYTES = V7X_VMEM_BYTES - 8 * 1024 * 1024

ROW_TILE = 512
SAMPLE_BATCH_TILE = 16

BF16 = jnp.bfloat16
F32 = jnp.float32
SQRT_2_OVER_PI = np.float32(np.sqrt(2.0 / np.pi))


def _dot(a, b):
    return jnp.dot(a, b, preferred_element_type=F32)


def _rms(x, g):
    return x * lax.rsqrt(jnp.mean(x * x, axis=-1, keepdims=True) + NORM_EPS) * g


def _gelu(x):
    return x * (0.5 * (1.0 + jnp.tanh(SQRT_2_OVER_PI * (x + 0.044715 * (x * x * x)))))


def _sigmoid(x):
    return 1.0 / (1.0 + jnp.exp(-x))


def _silu(x):
    return x * _sigmoid(x)


def _softplus(x):
    return jnp.maximum(x, 0.0) + jnp.log1p(jnp.exp(-jnp.abs(x)))


def _const_spec(shape):
    zeros = (0,) * len(shape)
    return pl.BlockSpec(shape, lambda i: zeros, pipeline_mode=pl.Buffered(1))


def _params():
    return pltpu.CompilerParams(dimension_semantics=("arbitrary",),
                                vmem_limit_bytes=VMEM_LIMIT_BYTES)


def _mixer_a_kernel(x_ref, npre_ref, npost_ref, win_ref, lng_ref, lnb_ref, ws_ref, bst_ref,
                    wout_ref, o_ref, *rest, tm, sample):
    x = x_ref[...]
    h = _rms(x, npre_ref[...]).astype(BF16)
    v = _gelu(_dot(h, win_ref[:, D_A:2 * D_A]))
    d = v - jnp.mean(v, axis=-1, keepdims=True)
    var = jnp.mean(d * d, axis=-1, keepdims=True)
    vn = d * lax.rsqrt(var + LN_EPS) * lng_ref[...] + lnb_ref[...]
    if sample:
        rest[0][...] = vn
    else:
        vnb = vn.astype(BF16)
        row = lax.broadcasted_iota(jnp.int32, (CHUNK, CHUNK), 0)
        col = lax.broadcasted_iota(jnp.int32, (CHUNK, CHUNK), 1)
        causal = row >= col
    acc = jnp.zeros((tm, D_MODEL), F32)
    for g in range(A_GROUPS):
        c0 = g * A_GROUP_W
        u = _gelu(_dot(h, win_ref[:, c0:c0 + A_GROUP_W]))
        gate = _silu(_dot(h, win_ref[:, 2 * D_A + c0:2 * D_A + c0 + A_GROUP_W]))
        if sample:
            mixed = ws_ref[g][0:1, 0:1] * vn[:, c0:c0 + A_GROUP_W] + bst_ref[0:1, g:g + 1]
        else:
            wsg = jnp.where(causal, ws_ref[g], 0.0).astype(BF16)
            bias = bst_ref[:, g:g + 1]
            mixed = jnp.concatenate(
                [_dot(wsg, vnb[c * CHUNK:(c + 1) * CHUNK, c0:c0 + A_GROUP_W]) + bias
                 for c in range(tm // CHUNK)], axis=0)
        z = (u * mixed * gate).astype(BF16)
        acc = acc + _dot(z, wout_ref[c0:c0 + A_GROUP_W, :])
    o_ref[...] = x + _rms(acc, npost_ref[...])


def _mixer_a(x, npre, npost, win, lng, lnb, ws, bst, wout, *, tm, sample):
    n = x.shape[0]
    row_spec = pl.BlockSpec((tm, D_MODEL), lambda i: (i, 0))
    out_shape = [jax.ShapeDtypeStruct((n, D_MODEL), F32)]
    out_specs = [row_spec]
    if sample:
        out_shape.append(jax.ShapeDtypeStruct((n, D_A), F32))
        out_specs.append(pl.BlockSpec((tm, D_A), lambda i: (i, 0)))
    return pl.pallas_call(
        functools.partial(_mixer_a_kernel, tm=tm, sample=sample),
        grid=(n // tm,),
        in_specs=[row_spec, _const_spec((1, D_MODEL)), _const_spec((1, D_MODEL)),
                  _const_spec((D_MODEL, 3 * D_A)), _const_spec((1, D_A)), _const_spec((1, D_A)),
                  _const_spec((A_GROUPS, CHUNK, CHUNK)), _const_spec((CHUNK, A_GROUPS)),
                  _const_spec((D_A, D_MODEL))],
        out_specs=out_specs, out_shape=out_shape,
        compiler_params=_params(),
        name="mixer_a_sample" if sample else "mixer_a_prompt",
    )(x, npre, npost, win, lng, lnb, ws, bst, wout)


def _rope_tables(positions):
    half = HEAD_DIM // 2
    inv_freq = ROPE_THETA ** (-jnp.arange(half, dtype=F32) / half)
    ang = positions.astype(F32)[:, None] * inv_freq[None, :]
    cos, sin = jnp.cos(ang), jnp.sin(ang)
    reps = V7X_LANES // HEAD_DIM
    cos_t = jnp.tile(jnp.concatenate([cos, cos], axis=-1), (1, reps))
    sin_t = jnp.tile(jnp.concatenate([-sin, sin], axis=-1), (1, reps))
    return cos_t, sin_t


def _rope(x, cos, sin_signed):
    lane = lax.broadcasted_iota(jnp.int32, x.shape, 1)
    first_half = (lane % HEAD_DIM) < (HEAD_DIM // 2)
    rot = jnp.where(first_half, pltpu.roll(x, V7X_LANES - HEAD_DIM // 2, 1),
                    pltpu.roll(x, HEAD_DIM // 2, 1))
    return x * cos + rot * sin_signed


def _dup_kv_halves(x, lane):
    swapped = pltpu.roll(x, HEAD_DIM, 1)
    low = lane < HEAD_DIM
    return jnp.where(low, x, swapped), jnp.where(low, swapped, x)


def _mixer_b_prompt_kernel(x_ref, npre_ref, npost_ref, win_ref, sinks_ref, cos_ref, sin_ref,
                           wout_ref, o_ref, kout_ref, vout_ref,
                           k0_ref, k1_ref, v0_ref, v1_ref, att_ref, *, tm, tiles_per_seq):
    first = (pl.program_id(0) % tiles_per_seq) == 0

    @pl.when(first)
    def _():
        zeros = jnp.zeros((WINDOW, V7X_LANES), BF16)
        for ref in (k0_ref, k1_ref, v0_ref, v1_ref):
            ref[0:WINDOW, :] = zeros

    x = x_ref[...]
    h = _rms(x, npre_ref[...]).astype(BF16)
    lane = lax.broadcasted_iota(jnp.int32, (tm, V7X_LANES), 1)
    cos, sin = cos_ref[...], sin_ref[...]

    k = _rope(_dot(h, win_ref[:, QD:QD + KD]), cos, sin)
    v = _dot(h, win_ref[:, QD + KD:QD + 2 * KD])
    kout_ref[...] = k[tm - WINDOW:, :]
    vout_ref[...] = v[tm - WINDOW:, :]
    kd0, kd1 = _dup_kv_halves(k, lane)
    vd0, vd1 = _dup_kv_halves(v, lane)
    k0_ref[WINDOW:, :] = kd0.astype(BF16)
    k1_ref[WINDOW:, :] = kd1.astype(BF16)
    v0_ref[WINDOW:, :] = vd0.astype(BF16)
    v1_ref[WINDOW:, :] = vd1.astype(BF16)

    qi = lax.broadcasted_iota(jnp.int32, (2 * WINDOW, 2 * WINDOW), 0) % WINDOW
    ci = lax.broadcasted_iota(jnp.int32, (2 * WINDOW, 2 * WINDOW), 1)
    band = (ci >= qi) & (ci <= qi + WINDOW)
    upper_rows = lax.broadcasted_iota(jnp.int32, (2 * WINDOW, 1), 0) < WINDOW
    low_lanes = lax.broadcasted_iota(jnp.int32, (WINDOW, V7X_LANES), 1) < HEAD_DIM
    scale = HEAD_DIM ** -0.5

    q = _dot(h, win_ref[:, :QD])
    for j in range(tm // WINDOW):
        rows = slice(j * WINDOW, (j + 1) * WINDOW)
        if j == 0:
            mask = band & (ci >= jnp.where(first, WINDOW, 0))
        else:
            mask = band
        for p in range(N_HEADS // 2):
            kvh = (2 * p) // Q_PER_KV
            kref, vref = (k0_ref, v0_ref) if kvh == 0 else (k1_ref, v1_ref)
            qp = q[rows, p * V7X_LANES:(p + 1) * V7X_LANES]
            qp = _rope(qp, cos[rows], sin[rows]) * scale
            qs = jnp.concatenate([jnp.where(low_lanes, qp, 0.0), jnp.where(low_lanes, 0.0, qp)],
                                 axis=0).astype(BF16)
            kk = kref[j * WINDOW:(j + 2) * WINDOW, :]
            s = lax.dot_general(qs, kk, (((1,), (1,)), ((), ())), preferred_element_type=F32)
            s = jnp.where(mask, s, -jnp.inf)
            sink = jnp.where(upper_rows, sinks_ref[0, 2 * p], sinks_ref[0, 2 * p + 1])
            m = jnp.maximum(jnp.max(s, axis=-1, keepdims=True), sink)
            e = jnp.exp(s - m)
            inv = 1.0 / (jnp.sum(e, axis=-1, keepdims=True) + jnp.exp(sink - m))
            r = _dot((e * inv).astype(BF16), vref[j * WINDOW:(j + 2) * WINDOW, :])
            att_ref[rows, p * V7X_LANES:(p + 1) * V7X_LANES] = jnp.where(
                low_lanes, r[:WINDOW], r[WINDOW:])

    for ref in (k0_ref, k1_ref, v0_ref, v1_ref):
        ref[0:WINDOW, :] = ref[tm:tm + WINDOW, :]

    gate = _silu(_dot(h, win_ref[:, QD + 2 * KD:]))
    y = _dot((att_ref[...] * gate).astype(BF16), wout_ref[...])
    o_ref[...] = x + _rms(y, npost_ref[...])


def _mixer_b_prompt(x, npre, npost, win, sinks, cos, sin, wout, *, tm, seq):
    n = x.shape[0]
    tiles_per_seq = seq // tm
    n_seq = n // seq
    row_spec = pl.BlockSpec((tm, D_MODEL), lambda i: (i, 0))
    rope_spec = pl.BlockSpec((tm, V7X_LANES), lambda i: (i % tiles_per_seq, 0))
    kv_spec = pl.BlockSpec((WINDOW, KD), lambda i: (i // tiles_per_seq, 0))
    return pl.pallas_call(
        functools.partial(_mixer_b_prompt_kernel, tm=tm, tiles_per_seq=tiles_per_seq),
        grid=(n // tm,),
        in_specs=[row_spec, _const_spec((1, D_MODEL)), _const_spec((1, D_MODEL)),
                  _const_spec((D_MODEL, 2 * QD + 2 * KD)),
                  pl.BlockSpec(memory_space=pltpu.SMEM),
                  rope_spec, rope_spec, _const_spec((QD, D_MODEL))],
        out_specs=[row_spec, kv_spec, kv_spec],
        out_shape=[jax.ShapeDtypeStruct((n, D_MODEL), F32),
                   jax.ShapeDtypeStruct((n_seq * WINDOW, KD), F32),
                   jax.ShapeDtypeStruct((n_seq * WINDOW, KD), F32)],
        scratch_shapes=[pltpu.VMEM((tm + WINDOW, V7X_LANES), BF16)] * 4
        + [pltpu.VMEM((tm, QD), F32)],
        compiler_params=_params(),
        name="mixer_b_prompt",
    )(x, npre, npost, win, sinks, cos, sin, wout)


def _mixer_b_sample_proj_kernel(x_ref, npre_ref, win_ref, cos_ref, sin_ref,
                                q_ref, k_ref, v_ref, gate_ref):
    n = x_ref.shape[0]
    h = _rms(x_ref[...], npre_ref[...]).astype(BF16)
    cos, sin = cos_ref[...], sin_ref[...]
    scale = HEAD_DIM ** -0.5
    low_lanes = lax.broadcasted_iota(jnp.int32, (n, V7X_LANES), 1) < HEAD_DIM
    for p in range(N_HEADS // 2):
        qp = _rope(_dot(h, win_ref[:, p * V7X_LANES:(p + 1) * V7X_LANES]), cos, sin) * scale
        swapped = pltpu.roll(qp, HEAD_DIM, 1)
        if (2 * p) // Q_PER_KV == 0:
            even, odd = jnp.where(low_lanes, qp, 0.0), jnp.where(low_lanes, swapped, 0.0)
        else:
            even, odd = jnp.where(low_lanes, 0.0, swapped), jnp.where(low_lanes, 0.0, qp)
        q_ref[pl.ds(2 * p, n, stride=N_HEADS), :] = even
        q_ref[pl.ds(2 * p + 1, n, stride=N_HEADS), :] = odd
    k_ref[...] = _rope(_dot(h, win_ref[:, QD:QD + KD]), cos, sin)
    v_ref[...] = _dot(h, win_ref[:, QD + KD:QD + 2 * KD])
    gate_ref[...] = _silu(_dot(h, win_ref[:, QD + 2 * KD:]))


def _mixer_b_sample_proj(x, npre, win, cos_row, sin_row):
    n = x.shape[0]
    return pl.pallas_call(
        _mixer_b_sample_proj_kernel,
        out_shape=[jax.ShapeDtypeStruct((n * N_HEADS, V7X_LANES), F32),
                   jax.ShapeDtypeStruct((n, KD), F32),
                   jax.ShapeDtypeStruct((n, KD), F32), jax.ShapeDtypeStruct((n, QD), F32)],
        compiler_params=pltpu.CompilerParams(vmem_limit_bytes=VMEM_LIMIT_BYTES),
        name="mixer_b_sample_proj",
    )(x, npre, win, cos_row, sin_row)


def _mixer_b_sample_attn_kernel(q_ref, kc_ref, vc_ref, kn_ref, vn_ref, sinks_ref,
                                o_ref, ko_ref, vo_ref):
    q = q_ref[...]
    kc, vc = kc_ref[...], vc_ref[...]
    kn, vn = kn_ref[...], vn_ref[...]
    qb = q.astype(BF16)
    s_c = jnp.einsum("bhl,bsl->bhs", qb, kc.astype(BF16), preferred_element_type=F32)
    s_n = jnp.sum(qb.astype(F32) * kn.astype(BF16).astype(F32), axis=-1, keepdims=True)
    sink = sinks_ref[...][None]
    m = jnp.maximum(jnp.maximum(jnp.max(s_c, axis=-1, keepdims=True), s_n), sink)
    e_c = jnp.exp(s_c - m)
    e_n = jnp.exp(s_n - m)
    inv = 1.0 / (jnp.sum(e_c, axis=-1, keepdims=True) + e_n + jnp.exp(sink - m))
    o = jnp.einsum("bhs,bsl->bhl", (e_c * inv).astype(BF16), vc.astype(BF16),
                   preferred_element_type=F32)
    p_n = (e_n * inv).astype(BF16).astype(F32)
    o_ref[...] = o + p_n * vn.astype(BF16).astype(F32)
    w = kc.shape[1]
    ko_ref[:, 0:w - 1, :] = kc[:, 1:w, :]
    ko_ref[:, w - 1:w, :] = kn
    vo_ref[:, 0:w - 1, :] = vc[:, 1:w, :]
    vo_ref[:, w - 1:w, :] = vn


def _mixer_b_sample_attn(qm, kc, vc, kn, vn, sinks_col, *, bt):
    b, w, _ = kc.shape
    q_spec = pl.BlockSpec((bt, N_HEADS, V7X_LANES), lambda i: (i, 0, 0))
    c_spec = pl.BlockSpec((bt, w, KD), lambda i: (i, 0, 0))
    n_spec = pl.BlockSpec((bt, 1, KD), lambda i: (i, 0, 0))
    return pl.pallas_call(
        _mixer_b_sample_attn_kernel,
        grid=(b // bt,),
        in_specs=[q_spec, c_spec, c_spec, n_spec, n_spec,
                  pl.BlockSpec((N_HEADS, 1), lambda i: (0, 0))],
        out_specs=[q_spec, c_spec, c_spec],
        out_shape=[jax.ShapeDtypeStruct((b, N_HEADS, V7X_LANES), F32),
                   jax.ShapeDtypeStruct((b, w, KD), F32), jax.ShapeDtypeStruct((b, w, KD), F32)],
        compiler_params=_params(),
        name="mixer_b_sample_attn",
    )(qm, kc, vc, kn, vn, sinks_col)


def _mixer_b_sample_out_kernel(x_ref, att_ref, gate_ref, wout_ref, npost_ref, o_ref):
    n = x_ref.shape[0]
    low_lanes = lax.broadcasted_iota(jnp.int32, (n, V7X_LANES), 1) < HEAD_DIM
    pairs = []
    for p in range(N_HEADS // 2):
        even = att_ref[pl.ds(2 * p, n, stride=N_HEADS), :]
        odd = att_ref[pl.ds(2 * p + 1, n, stride=N_HEADS), :]
        if (2 * p) // Q_PER_KV == 0:
            odd = pltpu.roll(odd, HEAD_DIM, 1)
        else:
            even = pltpu.roll(even, HEAD_DIM, 1)
        pairs.append(jnp.where(low_lanes, even, odd))
    att = jnp.concatenate(pairs, axis=1)
    y = _dot((att * gate_ref[...]).astype(BF16), wout_ref[...])
    o_ref[...] = x_ref[...] + _rms(y, npost_ref[...])


def _mixer_b_sample_out(x, att, gate, wout, npost):
    return pl.pallas_call(
        _mixer_b_sample_out_kernel,
        out_shape=jax.ShapeDtypeStruct(x.shape, F32),
        compiler_params=pltpu.CompilerParams(vmem_limit_bytes=VMEM_LIMIT_BYTES),
        name="mixer_b_sample_out",
    )(x, att, gate, wout, npost)


def _rglru_gates(xc, wa_ref, ba_ref, wx_ref, bx_ref, lam_ref):
    xcb = xc.astype(BF16)
    ra, rx = [], []
    for blk in range(C_BLOCKS):
        cols = slice(blk * C_BLOCK_W, (blk + 1) * C_BLOCK_W)
        ra.append(_dot(xcb[:, cols], wa_ref[blk]))
        rx.append(_dot(xcb[:, cols], wx_ref[blk]))
    r = _sigmoid(jnp.concatenate(ra, axis=-1) + ba_ref[...])
    i_gate = _sigmoid(jnp.concatenate(rx, axis=-1) + bx_ref[...])
    log_a = (-LRU_C * r) * _softplus(-lam_ref[...])
    a = jnp.exp(log_a)
    mult = jnp.sqrt(-jnp.tanh(log_a) * (a * a + 1.0))
    return a, mult, i_gate * xc


def _mixer_c_prompt_kernel(x_ref, npre_ref, npost_ref, win_ref, cw_ref, cb_ref, wa_ref, ba_ref,
                           wx_ref, bx_ref, lam_ref, wout_ref, o_ref, conv_ref, hlast_ref,
                           xpad_ref, h_ref, *, tm, tiles_per_seq):
    first = (pl.program_id(0) % tiles_per_seq) == 0

    @pl.when(first)
    def _():
        xpad_ref[0:V7X_SUBLANES, :] = jnp.zeros((V7X_SUBLANES, D_RNN), F32)
        h_ref[...] = jnp.zeros((V7X_SUBLANES, D_RNN), F32)

    x = x_ref[...]
    h = _rms(x, npre_ref[...]).astype(BF16)
    xr = _dot(h, win_ref[:, :D_RNN])
    xpad_ref[V7X_SUBLANES:, :] = xr
    conv_ref[...] = xr[tm - V7X_SUBLANES:, :]
    cw = cw_ref[...]
    xc = cb_ref[...] + xr * cw[CONV_W - 1:CONV_W, :]
    for tap in range(CONV_W - 1):
        back = CONV_W - 1 - tap
        xc = xc + xpad_ref[V7X_SUBLANES - back:V7X_SUBLANES - back + tm, :] * cw[tap:tap + 1, :]
    xpad_ref[0:V7X_SUBLANES, :] = xr[tm - V7X_SUBLANES:, :]

    a, mult, gx = _rglru_gates(xc, wa_ref, ba_ref, wx_ref, bx_ref, lam_ref)
    row = lax.broadcasted_iota(jnp.int32, (tm, 1), 0)
    mult = jnp.where(first & (row == 0), 1.0, mult)
    b = mult * gx

    n_groups = tm // V7X_SUBLANES
    a = a.reshape(n_groups, V7X_SUBLANES, D_RNN)
    b = b.reshape(n_groups, V7X_SUBLANES, D_RNN)
    sub = lax.broadcasted_iota(jnp.int32, (n_groups, V7X_SUBLANES, D_RNN), 1)
    for dist in (1, 2, 4):
        keep = sub >= dist
        a_prev = pltpu.roll(a, dist, 1)
        b_prev = pltpu.roll(b, dist, 1)
        b = jnp.where(keep, a * b_prev + b, b)
        a = jnp.where(keep, a * a_prev, a)
    carry = h_ref[V7X_SUBLANES - 1:V7X_SUBLANES, :]
    groups = []
    for gi in range(n_groups):
        hs_g = a[gi] * carry + b[gi]
        carry = hs_g[V7X_SUBLANES - 1:V7X_SUBLANES, :]
        groups.append(hs_g)
    h_ref[...] = groups[-1]
    hlast_ref[...] = groups[-1]
    hs = jnp.concatenate(groups, axis=0)

    gate = _silu(_dot(h, win_ref[:, D_RNN:]))
    y = _dot((hs * gate).astype(BF16), wout_ref[...])
    o_ref[...] = x + _rms(y, npost_ref[...])


def _mixer_c_prompt(x, npre, npost, win, cw, cb, wa, ba, wx, bx, lam, wout, *, tm, seq):
    n = x.shape[0]
    tiles_per_seq = seq // tm
    n_seq = n // seq
    row_spec = pl.BlockSpec((tm, D_MODEL), lambda i: (i, 0))
    tail_spec = pl.BlockSpec((V7X_SUBLANES, D_RNN), lambda i: (i // tiles_per_seq, 0))
    vec = _const_spec((1, D_RNN))
    blk = _const_spec((C_BLOCKS, C_BLOCK_W, C_BLOCK_W))
    return pl.pallas_call(
        functools.partial(_mixer_c_prompt_kernel, tm=tm, tiles_per_seq=tiles_per_seq),
        grid=(n // tm,),
        in_specs=[row_spec, vec, vec, _const_spec((D_MODEL, 2 * D_RNN)),
                  _const_spec((CONV_W, D_RNN)), vec, blk, vec, blk, vec, vec,
                  _const_spec((D_RNN, D_MODEL))],
        out_specs=[row_spec, tail_spec, tail_spec],
        out_shape=[jax.ShapeDtypeStruct((n, D_MODEL), F32),
                   jax.ShapeDtypeStruct((n_seq * V7X_SUBLANES, D_RNN), F32),
                   jax.ShapeDtypeStruct((n_seq * V7X_SUBLANES, D_RNN), F32)],
        scratch_shapes=[pltpu.VMEM((tm + V7X_SUBLANES, D_RNN), F32),
                        pltpu.VMEM((V7X_SUBLANES, D_RNN), F32)],
        compiler_params=_params(),
        name="mixer_c_prompt",
    )(x, npre, npost, win, cw, cb, wa, ba, wx, bx, lam, wout)


def _mixer_c_sample_kernel(x_ref, npre_ref, npost_ref, win_ref, cw_ref, cb_ref, wa_ref, ba_ref,
                           wx_ref, bx_ref, lam_ref, wout_ref, conv_ref, h0_ref,
                           o_ref, conv_out_ref, h_out_ref):
    x = x_ref[...]
    h = _rms(x, npre_ref[...]).astype(BF16)
    xr = _dot(h, win_ref[:, :D_RNN])
    cw = cw_ref[...]
    xc = cb_ref[...] + xr * cw[CONV_W - 1:CONV_W, :]
    for tap in range(CONV_W - 1):
        xc = xc + conv_ref[tap] * cw[tap:tap + 1, :]
        if tap > 0:
            conv_out_ref[tap - 1] = conv_ref[tap]
    conv_out_ref[CONV_W - 2] = xr
    a, mult, gx = _rglru_gates(xc, wa_ref, ba_ref, wx_ref, bx_ref, lam_ref)
    hs = a * h0_ref[...] + mult * gx
    h_out_ref[...] = hs
    gate = _silu(_dot(h, win_ref[:, D_RNN:]))
    y = _dot((hs * gate).astype(BF16), wout_ref[...])
    o_ref[...] = x + _rms(y, npost_ref[...])


def _mixer_c_sample(x, npre, npost, win, cw, cb, wa, ba, wx, bx, lam, wout, conv_t, h0):
    n = x.shape[0]
    return pl.pallas_call(
        _mixer_c_sample_kernel,
        out_shape=[jax.ShapeDtypeStruct((n, D_MODEL), F32),
                   jax.ShapeDtypeStruct((CONV_W - 1, n, D_RNN), F32),
                   jax.ShapeDtypeStruct((n, D_RNN), F32)],
        compiler_params=pltpu.CompilerParams(vmem_limit_bytes=VMEM_LIMIT_BYTES),
        name="mixer_c_sample",
    )(x, npre, npost, win, cw, cb, wa, ba, wx, bx, lam, wout, conv_t, h0)


def kernel(x_prompt, x_sample, cache_b_k, cache_b_v, state_c_conv, state_c_h, norm_pre, norm_post,
           a_w_in, a_ln_g, a_ln_b, a_w_s, a_b_s, a_w_out, b_w_in, b_sinks, b_w_out, c_w_in,
           c_conv_w, c_conv_b, c_w_a, c_b_a, c_w_x, c_b_x, c_lam, c_w_out):
    batch, seq, _ = x_prompt.shape
    dec_batch, dec_seq, _ = x_sample.shape
    past_len = PAST_LEN
    w_buf = cache_b_k.shape[2]
    assert dec_seq == 1 and w_buf == WINDOW and seq % ROW_TILE == 0
    assert ROW_TILE % CHUNK == 0 and ROW_TILE % WINDOW == 0

    xp = x_prompt.reshape(batch * seq, D_MODEL)
    xs = x_sample.reshape(dec_batch, D_MODEL)
    row = lambda a: a.reshape(1, -1)

    cos_p, sin_p = _rope_tables(jnp.arange(seq, dtype=jnp.int32))
    cos_s, sin_s = _rope_tables(past_len + jnp.arange(dec_seq, dtype=jnp.int32))

    a_v_s = []
    b_kp, b_vp, b_ks, b_vs = [], [], [], []
    c_cp, c_hp, c_cs, c_hs = [], [], [], []
    for i in range(DEPTH):
        kind, j = i % N_MIXERS, i // N_MIXERS
        npre, npost = row(norm_pre[i]), row(norm_post[i])
        if kind == 0:
            args = (npre, npost, a_w_in[j].astype(BF16), row(a_ln_g[j]), row(a_ln_b[j]),
                    a_w_s[j], a_b_s[j].T, a_w_out[j].astype(BF16))
            (xp,) = _mixer_a(xp, *args, tm=ROW_TILE, sample=False)
            xs, vs = _mixer_a(xs, *args, tm=dec_batch, sample=True)
            a_v_s.append(vs.reshape(dec_batch, dec_seq, D_A))
        elif kind == 1:
            win, wout = b_w_in[j].astype(BF16), b_w_out[j].astype(BF16)
            xp, kp, vp = _mixer_b_prompt(xp, npre, npost, win, row(b_sinks[j]), cos_p, sin_p, wout,
                                         tm=ROW_TILE, seq=seq)
            b_kp.append(kp.reshape(batch, WINDOW, N_KV_HEADS, HEAD_DIM))
            b_vp.append(vp.reshape(batch, WINDOW, N_KV_HEADS, HEAD_DIM))

            qm, kn, vn, gate = _mixer_b_sample_proj(xs, npre, win, cos_s, sin_s)
            qm = qm.reshape(dec_batch, N_HEADS, V7X_LANES)
            om, ks_new, vs_new = _mixer_b_sample_attn(
                qm, cache_b_k[j].reshape(dec_batch, w_buf, KD),
                cache_b_v[j].reshape(dec_batch, w_buf, KD),
                kn.reshape(dec_batch, 1, KD), vn.reshape(dec_batch, 1, KD),
                b_sinks[j].reshape(N_HEADS, 1), bt=SAMPLE_BATCH_TILE)
            xs = _mixer_b_sample_out(xs, om.reshape(dec_batch * N_HEADS, V7X_LANES), gate, wout,
                                     npost)
            b_ks.append(ks_new.reshape(dec_batch, WINDOW, N_KV_HEADS, HEAD_DIM))
            b_vs.append(vs_new.reshape(dec_batch, WINDOW, N_KV_HEADS, HEAD_DIM))
        else:
            args = (npre, npost, c_w_in[j].astype(BF16), c_conv_w[j], row(c_conv_b[j]),
                    c_w_a[j].astype(BF16), row(c_b_a[j]), c_w_x[j].astype(BF16), row(c_b_x[j]),
                    row(c_lam[j]), c_w_out[j].astype(BF16))
            xp, conv_tail, h_tail = _mixer_c_prompt(xp, *args, tm=ROW_TILE, seq=seq)
            c_cp.append(conv_tail.reshape(batch, V7X_SUBLANES, D_RNN)[:, V7X_SUBLANES - (CONV_W - 1):])
            c_hp.append(h_tail.reshape(batch, V7X_SUBLANES, D_RNN)[:, V7X_SUBLANES - 1])
            xs, conv_new, h_new = _mixer_c_sample(
                xs, *args, jnp.transpose(state_c_conv[j], (1, 0, 2)), state_c_h[j])
            c_cs.append(jnp.transpose(conv_new, (1, 0, 2)))
            c_hs.append(h_new)

    return (xp.reshape(batch, seq, D_MODEL), xs.reshape(dec_batch, dec_seq, D_MODEL),
            jnp.stack(a_v_s), jnp.stack(b_kp), jnp.stack(b_vp), jnp.stack(b_ks), jnp.stack(b_vs),
            jnp.stack(c_cp), jnp.stack(c_hp), jnp.stack(c_cs), jnp.stack(c_hs))
```

```python
import functools

import jax
import jax.numpy as jnp
import numpy as np
from jax import lax
from jax.experimental import pallas as pl
from jax.experimental.pallas import tpu as pltpu

D_MODEL = 1024
DEPTH = 4
N_MIXERS = 3
NORM_EPS = 1e-6
LN_EPS = 1e-5

D_A = 2 * D_MODEL
CHUNK = 128
A_GROUPS = 4
A_GROUP_W = D_A // A_GROUPS

HEAD_DIM = 64
N_HEADS = D_MODEL // HEAD_DIM
N_KV_HEADS = N_HEADS // 8
Q_PER_KV = N_HEADS // N_KV_HEADS
WINDOW = 128
ROPE_THETA = 10000.0
QD = N_HEADS * HEAD_DIM
KD = N_KV_HEADS * HEAD_DIM

D_RNN = D_MODEL
C_BLOCKS = 4
C_BLOCK_W = D_RNN // C_BLOCKS
CONV_W = 4
LRU_C = 8.0

PAST_LEN = 8192

V7X_LANES = 128
V7X_SUBLANES = 8
V7X_VMEM_BYTES = 64 * 1024 * 1024
VMEM_LIMIT_BYTES = V7X_VMEM_BYTES - 8 * 1024 * 1024

ROW_TILE = 512
A_ROW_TILE = 1024
A_SUB_ROWS = 256
B_PROJ_ROWS = 256
SAMPLE_BATCH_TILE = 16

BF16 = jnp.bfloat16
F32 = jnp.float32
SQRT_2_OVER_PI = np.float32(np.sqrt(2.0 / np.pi))


def _dot(a, b):
    return jnp.dot(a, b, preferred_element_type=F32)


def _rms(x, g):
    return x * lax.rsqrt(jnp.mean(x * x, axis=-1, keepdims=True) + NORM_EPS) * g


def _gelu(x):
    return x * (0.5 * (1.0 + jnp.tanh(SQRT_2_OVER_PI * (x + 0.044715 * (x * x * x)))))


def _sigmoid(x):
    return 1.0 / (1.0 + jnp.exp(-x))


def _silu(x):
    return x * _sigmoid(x)


def _softplus(x):
    return jnp.maximum(x, 0.0) + jnp.log1p(jnp.exp(-jnp.abs(x)))


def _const_spec(shape):
    zeros = (0,) * len(shape)
    return pl.BlockSpec(shape, lambda i: zeros, pipeline_mode=pl.Buffered(1))


def _params():
    return pltpu.CompilerParams(dimension_semantics=("arbitrary",),
                                vmem_limit_bytes=VMEM_LIMIT_BYTES)


def _layer_norm_a(v, lng_ref, lnb_ref):
    d = v - jnp.mean(v, axis=-1, keepdims=True)
    var = jnp.mean(d * d, axis=-1, keepdims=True)
    return d * lax.rsqrt(var + LN_EPS) * lng_ref[...] + lnb_ref[...]


def _mixer_a_sample_kernel(x_ref, npre_ref, npost_ref, win_ref, lng_ref, lnb_ref, ws_ref, bst_ref,
                           wout_ref, o_ref, v_ref):
    x = x_ref[...]
    h = _rms(x, npre_ref[...]).astype(BF16)
    vn = _layer_norm_a(_gelu(_dot(h, win_ref[:, D_A:2 * D_A])), lng_ref, lnb_ref)
    v_ref[...] = vn
    acc = jnp.zeros(x.shape, F32)
    for g in range(A_GROUPS):
        c0 = g * A_GROUP_W
        u = _gelu(_dot(h, win_ref[:, c0:c0 + A_GROUP_W]))
        gate = _silu(_dot(h, win_ref[:, 2 * D_A + c0:2 * D_A + c0 + A_GROUP_W]))
        mixed = ws_ref[g][0:1, 0:1] * vn[:, c0:c0 + A_GROUP_W] + bst_ref[0:1, g:g + 1]
        acc = acc + _dot((u * mixed * gate).astype(BF16), wout_ref[c0:c0 + A_GROUP_W, :])
    o_ref[...] = x + _rms(acc, npost_ref[...])


def _mixer_a_prompt_kernel(x_ref, npre_ref, npost_ref, win_ref, lng_ref, lnb_ref, ws_ref, bst_ref,
                           wout_ref, o_ref, *, tm, sub):
    x = x_ref[...]
    h = _rms(x, npre_ref[...]).astype(BF16)
    row = lax.broadcasted_iota(jnp.int32, (CHUNK, CHUNK), 0)
    col = lax.broadcasted_iota(jnp.int32, (CHUNK, CHUNK), 1)
    causal = row >= col
    n_sub = tm // sub
    items = [(r, g) for r in range(n_sub) for g in range(A_GROUPS)]
    v_parts = {r: [] for r in range(n_sub)}
    vnb, acc, ws_masked = {}, {}, {}

    def sub_rows(r):
        return slice(r * sub, (r + 1) * sub)

    def v_part(r, c):
        c0 = D_A + c * A_GROUP_W
        v_parts[r].append(_gelu(_dot(h[sub_rows(r)], win_ref[:, c0:c0 + A_GROUP_W])))

    def v_finish(r):
        v = jnp.concatenate(v_parts.pop(r), axis=1)
        vnb[r] = _layer_norm_a(v, lng_ref, lnb_ref).astype(BF16)

    def front(r, g):
        c0 = g * A_GROUP_W
        hr = h[sub_rows(r)]
        u = _dot(hr, win_ref[:, c0:c0 + A_GROUP_W])
        gate = _dot(hr, win_ref[:, 2 * D_A + c0:2 * D_A + c0 + A_GROUP_W])
        if g not in ws_masked:
            ws_masked[g] = jnp.where(causal, ws_ref[g], 0.0).astype(BF16)
        mixed = jnp.concatenate(
            [_dot(ws_masked[g], vnb[r][c * CHUNK:(c + 1) * CHUNK, c0:c0 + A_GROUP_W])
             for c in range(sub // CHUNK)], axis=0)
        return u, gate, mixed

    def back(r, g, u, gate, mixed):
        c0 = g * A_GROUP_W
        bias = jnp.concatenate([bst_ref[:, g:g + 1]] * (sub // CHUNK), axis=0)
        z = (_gelu(u) * (mixed + bias) * _silu(gate)).astype(BF16)
        y = _dot(z, wout_ref[c0:c0 + A_GROUP_W, :])
        acc[r] = y if g == 0 else acc[r] + y
        if g == A_GROUPS - 1:
            rows = sub_rows(r)
            o_ref[rows, :] = x[rows] + _rms(acc.pop(r), npost_ref[...])

    for c in range(A_GROUPS):
        v_part(0, c)
    v_finish(0)
    nxt = front(*items[0])
    for k, (r, g) in enumerate(items):
        cur = nxt
        if r + 1 < n_sub:
            v_part(r + 1, g)
            if g == A_GROUPS - 1:
                v_finish(r + 1)
        if k + 1 < len(items):
            nxt = front(*items[k + 1])
        back(r, g, *cur)


def _mixer_a(x, npre, npost, win, lng, lnb, ws, bst, wout, *, tm, sample):
    n = x.shape[0]
    row_spec = pl.BlockSpec((tm, D_MODEL), lambda i: (i, 0))
    out_shape = [jax.ShapeDtypeStruct((n, D_MODEL), F32)]
    out_specs = [row_spec]
    if sample:
        out_shape.append(jax.ShapeDtypeStruct((n, D_A), F32))
        out_specs.append(pl.BlockSpec((tm, D_A), lambda i: (i, 0)))
    return pl.pallas_call(
        _mixer_a_sample_kernel if sample
        else functools.partial(_mixer_a_prompt_kernel, tm=tm, sub=A_SUB_ROWS),
        grid=(n // tm,),
        in_specs=[row_spec, _const_spec((1, D_MODEL)), _const_spec((1, D_MODEL)),
                  _const_spec((D_MODEL, 3 * D_A)), _const_spec((1, D_A)), _const_spec((1, D_A)),
                  _const_spec((A_GROUPS, CHUNK, CHUNK)), _const_spec((CHUNK, A_GROUPS)),
                  _const_spec((D_A, D_MODEL))],
        out_specs=out_specs, out_shape=out_shape,
        compiler_params=_params(),
        name="mixer_a_sample" if sample else "mixer_a_prompt",
    )(x, npre, npost, win, lng, lnb, ws, bst, wout)


def _rope_tables(positions):
    half = HEAD_DIM // 2
    inv_freq = ROPE_THETA ** (-jnp.arange(half, dtype=F32) / half)
    ang = positions.astype(F32)[:, None] * inv_freq[None, :]
    cos, sin = jnp.cos(ang), jnp.sin(ang)
    reps = V7X_LANES // HEAD_DIM
    cos_t = jnp.tile(jnp.concatenate([cos, cos], axis=-1), (1, reps))
    sin_t = jnp.tile(jnp.concatenate([-sin, sin], axis=-1), (1, reps))
    return cos_t, sin_t


def _rope(x, cos, sin_signed):
    lane = lax.broadcasted_iota(jnp.int32, x.shape, 1)
    first_half = (lane % HEAD_DIM) < (HEAD_DIM // 2)
    rot = jnp.where(first_half, pltpu.roll(x, V7X_LANES - HEAD_DIM // 2, 1),
                    pltpu.roll(x, HEAD_DIM // 2, 1))
    return x * cos + rot * sin_signed


def _dup_kv_halves(x, lane):
    swapped = pltpu.roll(x, HEAD_DIM, 1)
    low = lane < HEAD_DIM
    return jnp.where(low, x, swapped), jnp.where(low, swapped, x)


def _mixer_b_prompt_kernel(x_ref, npre_ref, npost_ref, win_ref, sinks_ref, cos_ref, sin_ref,
                           wout_ref, o_ref, kout_ref, vout_ref,
                           k0_ref, k1_ref, vt0_ref, vt1_ref, att_ref, *, tm, tiles_per_seq):
    first = (pl.program_id(0) % tiles_per_seq) == 0

    @pl.when(first)
    def _():
        for ref in (k0_ref, k1_ref):
            ref[0:WINDOW, :] = jnp.zeros((WINDOW, V7X_LANES), BF16)
        for ref in (vt0_ref, vt1_ref):
            ref[:, 0:WINDOW] = jnp.zeros((V7X_LANES, WINDOW), BF16)

    x = x_ref[...]
    h = _rms(x, npre_ref[...]).astype(BF16)
    lane = lax.broadcasted_iota(jnp.int32, (tm, V7X_LANES), 1)
    cos, sin = cos_ref[...], sin_ref[...]

    k = _rope(_dot(h, win_ref[:, QD:QD + KD]), cos, sin)
    v = _dot(h, win_ref[:, QD + KD:QD + 2 * KD])
    kout_ref[...] = k[tm - WINDOW:, :]
    vout_ref[...] = v[tm - WINDOW:, :]
    kd0, kd1 = _dup_kv_halves(k, lane)
    k0_ref[WINDOW:, :] = kd0.astype(BF16)
    k1_ref[WINDOW:, :] = kd1.astype(BF16)
    vt = v.T
    vt0_ref[:, WINDOW:] = jnp.concatenate([vt[:HEAD_DIM], vt[:HEAD_DIM]], axis=0).astype(BF16)
    vt1_ref[:, WINDOW:] = jnp.concatenate([vt[HEAD_DIM:], vt[HEAD_DIM:]], axis=0).astype(BF16)

    ci = lax.broadcasted_iota(jnp.int32, (2 * WINDOW, 2 * WINDOW), 0)
    qi = lax.broadcasted_iota(jnp.int32, (2 * WINDOW, 2 * WINDOW), 1) % WINDOW
    band = (ci >= qi) & (ci <= qi + WINDOW)
    bias = jnp.where(band, 0.0, -jnp.inf)
    bias_first = jnp.where(band & (ci >= jnp.where(first, WINDOW, 0)), 0.0, -jnp.inf)
    head_lane = lax.broadcasted_iota(jnp.int32, (1, 2 * WINDOW), 1) < WINDOW
    low_lanes = lax.broadcasted_iota(jnp.int32, (WINDOW, V7X_LANES), 1) < HEAD_DIM
    top_rows = lax.broadcasted_iota(jnp.int32, (V7X_LANES, WINDOW), 0) < HEAD_DIM
    scale = HEAD_DIM ** -0.5

    n_blocks = tm // WINDOW
    n_pairs = N_HEADS // 2
    chunk = 2 * V7X_LANES
    n_chunks = QD // chunk
    bpg = B_PROJ_ROWS // WINDOW
    n_groups = tm // B_PROJ_ROWS
    items = [(j, p) for j in range(n_blocks) for p in range(n_pairs)]
    q_chunks, gate_chunks, gated, y_chunks = {}, {}, {}, {}

    def block_rows(j):
        return slice(j * WINDOW, (j + 1) * WINDOW)

    def group_rows(g):
        return slice(g * B_PROJ_ROWS, (g + 1) * B_PROJ_ROWS)

    def project_q(g, c):
        q_chunks[g, c] = _dot(h[group_rows(g)], win_ref[:, c * chunk:(c + 1) * chunk])

    def project_gate(g, c):
        c0 = QD + 2 * KD + c * chunk
        gate_chunks[g, c] = _silu(_dot(h[group_rows(g)], win_ref[:, c0:c0 + chunk]))

    def project_out(g, c):
        if c == 0:
            gated[g] = jnp.concatenate(
                [att_ref[group_rows(g), cc * chunk:(cc + 1) * chunk] * gate_chunks.pop((g, cc))
                 for cc in range(n_chunks)], axis=1).astype(BF16)
        y_chunks[g, c] = _dot(gated[g], wout_ref[:, c * chunk:(c + 1) * chunk])
        if c == n_chunks - 1:
            y = jnp.concatenate([y_chunks.pop((g, cc)) for cc in range(n_chunks)], axis=1)
            rows = group_rows(g)
            o_ref[rows, :] = x[rows] + _rms(y, npost_ref[...])

    def scores(j, p):
        rows = block_rows(j)
        kref = k0_ref if (2 * p) // Q_PER_KV == 0 else k1_ref
        half = (p % 2) * V7X_LANES
        r0 = (j % bpg) * WINDOW
        qp = q_chunks[j // bpg, p // 2][r0:r0 + WINDOW, half:half + V7X_LANES]
        qp = _rope(qp, cos[rows], sin[rows]) * scale
        qs = jnp.concatenate([jnp.where(low_lanes, qp, 0.0), jnp.where(low_lanes, 0.0, qp)],
                             axis=0).astype(BF16)
        st = lax.dot_general(kref[j * WINDOW:(j + 2) * WINDOW, :], qs, (((1,), (1,)), ((), ())),
                             preferred_element_type=F32)
        return st + (bias_first if j == 0 else bias)

    def attend(j, p, st):
        vtref = vt0_ref if (2 * p) // Q_PER_KV == 0 else vt1_ref
        sink = jnp.where(head_lane, sinks_ref[0, 2 * p], sinks_ref[0, 2 * p + 1])
        m = jnp.maximum(jnp.max(st, axis=0, keepdims=True), sink)
        e = jnp.exp(st - m)
        inv = 1.0 / (jnp.sum(e, axis=0, keepdims=True) + jnp.exp(sink - m))
        return _dot(vtref[:, j * WINDOW:(j + 2) * WINDOW], e.astype(BF16)), inv

    def finish(j, p, ot, inv):
        ot = ot * inv
        pair = jnp.where(top_rows, ot[:, :WINDOW], ot[:, WINDOW:])
        att_ref[block_rows(j), p * V7X_LANES:(p + 1) * V7X_LANES] = pair.T

    items_per_group = bpg * n_pairs
    side = {}
    for g in range(n_groups):
        tasks = []
        for c in range(n_chunks):
            if g + 1 < n_groups:
                tasks.append((project_q, g + 1, c))
            tasks.append((project_gate, g, c))
            if g > 0:
                tasks.append((project_out, g - 1, c))
        for t, task in enumerate(tasks):
            at = g * items_per_group + (t * items_per_group) // len(tasks)
            side.setdefault(at, []).append(task)

    for c in range(n_chunks):
        project_q(0, c)
    st_next = scores(*items[0])
    pending = None
    for i, (j, p) in enumerate(items):
        st = st_next
        for fn, g, c in side.get(i, []):
            if fn is not project_out:
                fn(g, c)
        if i + 1 < len(items):
            st_next = scores(*items[i + 1])
        ot, inv = attend(j, p, st)
        if pending is not None:
            finish(*pending)
        pending = (j, p, ot, inv)
        for fn, g, c in side.get(i, []):
            if fn is project_out:
                fn(g, c)
    finish(*pending)
    for c in range(n_chunks):
        project_out(n_groups - 1, c)

    for ref in (k0_ref, k1_ref):
        ref[0:WINDOW, :] = ref[tm:tm + WINDOW, :]
    for ref in (vt0_ref, vt1_ref):
        ref[:, 0:WINDOW] = ref[:, tm:tm + WINDOW]


def _mixer_b_prompt(x, npre, npost, win, sinks, cos, sin, wout, *, tm, seq):
    n = x.shape[0]
    tiles_per_seq = seq // tm
    n_seq = n // seq
    row_spec = pl.BlockSpec((tm, D_MODEL), lambda i: (i, 0))
    rope_spec = pl.BlockSpec((tm, V7X_LANES), lambda i: (i % tiles_per_seq, 0))
    kv_spec = pl.BlockSpec((WINDOW, KD), lambda i: (i // tiles_per_seq, 0))
    return pl.pallas_call(
        functools.partial(_mixer_b_prompt_kernel, tm=tm, tiles_per_seq=tiles_per_seq),
        grid=(n // tm,),
        in_specs=[row_spec, _const_spec((1, D_MODEL)), _const_spec((1, D_MODEL)),
                  _const_spec((D_MODEL, 2 * QD + 2 * KD)),
                  pl.BlockSpec(memory_space=pltpu.SMEM),
                  rope_spec, rope_spec, _const_spec((QD, D_MODEL))],
        out_specs=[row_spec, kv_spec, kv_spec],
        out_shape=[jax.ShapeDtypeStruct((n, D_MODEL), F32),
                   jax.ShapeDtypeStruct((n_seq * WINDOW, KD), F32),
                   jax.ShapeDtypeStruct((n_seq * WINDOW, KD), F32)],
        scratch_shapes=[pltpu.VMEM((tm + WINDOW, V7X_LANES), BF16)] * 2
        + [pltpu.VMEM((V7X_LANES, tm + WINDOW), BF16)] * 2 + [pltpu.VMEM((tm, QD), F32)],
        compiler_params=_params(),
        name="mixer_b_prompt",
    )(x, npre, npost, win, sinks, cos, sin, wout)


def _mixer_b_sample_proj_kernel(x_ref, npre_ref, win_ref, cos_ref, sin_ref,
                                q_ref, k_ref, v_ref, gate_ref):
    n = x_ref.shape[0]
    h = _rms(x_ref[...], npre_ref[...]).astype(BF16)
    cos, sin = cos_ref[...], sin_ref[...]
    scale = HEAD_DIM ** -0.5
    low_lanes = lax.broadcasted_iota(jnp.int32, (n, V7X_LANES), 1) < HEAD_DIM
    for p in range(N_HEADS // 2):
        qp = _rope(_dot(h, win_ref[:, p * V7X_LANES:(p + 1) * V7X_LANES]), cos, sin) * scale
        swapped = pltpu.roll(qp, HEAD_DIM, 1)
        if (2 * p) // Q_PER_KV == 0:
            even, odd = jnp.where(low_lanes, qp, 0.0), jnp.where(low_lanes, swapped, 0.0)
        else:
            even, odd = jnp.where(low_lanes, 0.0, swapped), jnp.where(low_lanes, 0.0, qp)
        q_ref[pl.ds(2 * p, n, stride=N_HEADS), :] = even
        q_ref[pl.ds(2 * p + 1, n, stride=N_HEADS), :] = odd
    k_ref[...] = _rope(_dot(h, win_ref[:, QD:QD + KD]), cos, sin)
    v_ref[...] = _dot(h, win_ref[:, QD + KD:QD + 2 * KD])
    gate_ref[...] = _silu(_dot(h, win_ref[:, QD + 2 * KD:]))


def _mixer_b_sample_proj(x, npre, win, cos_row, sin_row):
    n = x.shape[0]
    return pl.pallas_call(
        _mixer_b_sample_proj_kernel,
        out_shape=[jax.ShapeDtypeStruct((n * N_HEADS, V7X_LANES), F32),
                   jax.ShapeDtypeStruct((n, KD), F32),
                   jax.ShapeDtypeStruct((n, KD), F32), jax.ShapeDtypeStruct((n, QD), F32)],
        compiler_params=pltpu.CompilerParams(vmem_limit_bytes=VMEM_LIMIT_BYTES),
        name="mixer_b_sample_proj",
    )(x, npre, win, cos_row, sin_row)


def _mixer_b_sample_attn_kernel(q_ref, kc_ref, vc_ref, kn_ref, vn_ref, sinks_ref,
                                o_ref, ko_ref, vo_ref):
    q = q_ref[...]
    kc, vc = kc_ref[...], vc_ref[...]
    kn, vn = kn_ref[...], vn_ref[...]
    qb = q.astype(BF16)
    s_c = jnp.einsum("bhl,bsl->bhs", qb, kc.astype(BF16), preferred_element_type=F32)
    s_n = jnp.sum(qb.astype(F32) * kn.astype(BF16).astype(F32), axis=-1, keepdims=True)
    sink = sinks_ref[...][None]
    m = jnp.maximum(jnp.maximum(jnp.max(s_c, axis=-1, keepdims=True), s_n), sink)
    e_c = jnp.exp(s_c - m)
    e_n = jnp.exp(s_n - m)
    inv = 1.0 / (jnp.sum(e_c, axis=-1, keepdims=True) + e_n + jnp.exp(sink - m))
    o = jnp.einsum("bhs,bsl->bhl", (e_c * inv).astype(BF16), vc.astype(BF16),
                   preferred_element_type=F32)
    p_n = (e_n * inv).astype(BF16).astype(F32)
    o_ref[...] = o + p_n * vn.astype(BF16).astype(F32)
    w = kc.shape[1]
    ko_ref[:, 0:w - 1, :] = kc[:, 1:w, :]
    ko_ref[:, w - 1:w, :] = kn
    vo_ref[:, 0:w - 1, :] = vc[:, 1:w, :]
    vo_ref[:, w - 1:w, :] = vn


def _mixer_b_sample_attn(qm, kc, vc, kn, vn, sinks_col, *, bt):
    b, w, _ = kc.shape
    q_spec = pl.BlockSpec((bt, N_HEADS, V7X_LANES), lambda i: (i, 0, 0))
    c_spec = pl.BlockSpec((bt, w, KD), lambda i: (i, 0, 0))
    n_spec = pl.BlockSpec((bt, 1, KD), lambda i: (i, 0, 0))
    return pl.pallas_call(
        _mixer_b_sample_attn_kernel,
        grid=(b // bt,),
        in_specs=[q_spec, c_spec, c_spec, n_spec, n_spec,
                  pl.BlockSpec((N_HEADS, 1), lambda i: (0, 0))],
        out_specs=[q_spec, c_spec, c_spec],
        out_shape=[jax.ShapeDtypeStruct((b, N_HEADS, V7X_LANES), F32),
                   jax.ShapeDtypeStruct((b, w, KD), F32), jax.ShapeDtypeStruct((b, w, KD), F32)],
        compiler_params=_params(),
        name="mixer_b_sample_attn",
    )(qm, kc, vc, kn, vn, sinks_col)


def _mixer_b_sample_out_kernel(x_ref, att_ref, gate_ref, wout_ref, npost_ref, o_ref):
    n = x_ref.shape[0]
    low_lanes = lax.broadcasted_iota(jnp.int32, (n, V7X_LANES), 1) < HEAD_DIM
    pairs = []
    for p in range(N_HEADS // 2):
        even = att_ref[pl.ds(2 * p, n, stride=N_HEADS), :]
        odd = att_ref[pl.ds(2 * p + 1, n, stride=N_HEADS), :]
        if (2 * p) // Q_PER_KV == 0:
            odd = pltpu.roll(odd, HEAD_DIM, 1)
        else:
            even = pltpu.roll(even, HEAD_DIM, 1)
        pairs.append(jnp.where(low_lanes, even, odd))
    att = jnp.concatenate(pairs, axis=1)
    y = _dot((att * gate_ref[...]).astype(BF16), wout_ref[...])
    o_ref[...] = x_ref[...] + _rms(y, npost_ref[...])


def _mixer_b_sample_out(x, att, gate, wout, npost):
    return pl.pallas_call(
        _mixer_b_sample_out_kernel,
        out_shape=jax.ShapeDtypeStruct(x.shape, F32),
        compiler_params=pltpu.CompilerParams(vmem_limit_bytes=VMEM_LIMIT_BYTES),
        name="mixer_b_sample_out",
    )(x, att, gate, wout, npost)


def _rglru_gates(xc, wa_ref, ba_ref, wx_ref, bx_ref, lam_ref):
    xcb = xc.astype(BF16)
    ra, rx = [], []
    for blk in range(C_BLOCKS):
        cols = slice(blk * C_BLOCK_W, (blk + 1) * C_BLOCK_W)
        ra.append(_dot(xcb[:, cols], wa_ref[blk]))
        rx.append(_dot(xcb[:, cols], wx_ref[blk]))
    r = _sigmoid(jnp.concatenate(ra, axis=-1) + ba_ref[...])
    i_gate = _sigmoid(jnp.concatenate(rx, axis=-1) + bx_ref[...])
    log_a = (-LRU_C * r) * _softplus(-lam_ref[...])
    a = jnp.exp(log_a)
    mult = jnp.sqrt(-jnp.tanh(log_a) * (a * a + 1.0))
    return a, mult, i_gate * xc


def _mixer_c_prompt_kernel(x_ref, npre_ref, npost_ref, win_ref, cw_ref, cb_ref, wa_ref, ba_ref,
                           wx_ref, bx_ref, lam_ref, wout_ref, o_ref, conv_ref, hlast_ref,
                           xpad_ref, h_ref, *, tm, tiles_per_seq):
    first = (pl.program_id(0) % tiles_per_seq) == 0

    @pl.when(first)
    def _():
        xpad_ref[0:V7X_SUBLANES, :] = jnp.zeros((V7X_SUBLANES, D_RNN), F32)
        h_ref[...] = jnp.zeros((V7X_SUBLANES, D_RNN), F32)

    x = x_ref[...]
    h = _rms(x, npre_ref[...]).astype(BF16)
    xr = _dot(h, win_ref[:, :D_RNN])
    xpad_ref[V7X_SUBLANES:, :] = xr
    conv_ref[...] = xr[tm - V7X_SUBLANES:, :]
    cw = cw_ref[...]
    xc = cb_ref[...] + xr * cw[CONV_W - 1:CONV_W, :]
    for tap in range(CONV_W - 1):
        back = CONV_W - 1 - tap
        xc = xc + xpad_ref[V7X_SUBLANES - back:V7X_SUBLANES - back + tm, :] * cw[tap:tap + 1, :]
    xpad_ref[0:V7X_SUBLANES, :] = xr[tm - V7X_SUBLANES:, :]

    a, mult, gx = _rglru_gates(xc, wa_ref, ba_ref, wx_ref, bx_ref, lam_ref)
    row = lax.broadcasted_iota(jnp.int32, (tm, 1), 0)
    mult = jnp.where(first & (row == 0), 1.0, mult)
    b = mult * gx

    n_groups = tm // V7X_SUBLANES
    a = a.reshape(n_groups, V7X_SUBLANES, D_RNN)
    b = b.reshape(n_groups, V7X_SUBLANES, D_RNN)
    sub = lax.broadcasted_iota(jnp.int32, (n_groups, V7X_SUBLANES, D_RNN), 1)
    for dist in (1, 2, 4):
        keep = sub >= dist
        a_prev = pltpu.roll(a, dist, 1)
        b_prev = pltpu.roll(b, dist, 1)
        b = jnp.where(keep, a * b_prev + b, b)
        a = jnp.where(keep, a * a_prev, a)
    carry = h_ref[V7X_SUBLANES - 1:V7X_SUBLANES, :]
    groups = []
    for gi in range(n_groups):
        hs_g = a[gi] * carry + b[gi]
        carry = hs_g[V7X_SUBLANES - 1:V7X_SUBLANES, :]
        groups.append(hs_g)
    h_ref[...] = groups[-1]
    hlast_ref[...] = groups[-1]
    hs = jnp.concatenate(groups, axis=0)

    gate = _silu(_dot(h, win_ref[:, D_RNN:]))
    y = _dot((hs * gate).astype(BF16), wout_ref[...])
    o_ref[...] = x + _rms(y, npost_ref[...])


def _mixer_c_prompt(x, npre, npost, win, cw, cb, wa, ba, wx, bx, lam, wout, *, tm, seq):
    n = x.shape[0]
    tiles_per_seq = seq // tm
    n_seq = n // seq
    row_spec = pl.BlockSpec((tm, D_MODEL), lambda i: (i, 0))
    tail_spec = pl.BlockSpec((V7X_SUBLANES, D_RNN), lambda i: (i // tiles_per_seq, 0))
    vec = _const_spec((1, D_RNN))
    blk = _const_spec((C_BLOCKS, C_BLOCK_W, C_BLOCK_W))
    return pl.pallas_call(
        functools.partial(_mixer_c_prompt_kernel, tm=tm, tiles_per_seq=tiles_per_seq),
        grid=(n // tm,),
        in_specs=[row_spec, vec, vec, _const_spec((D_MODEL, 2 * D_RNN)),
                  _const_spec((CONV_W, D_RNN)), vec, blk, vec, blk, vec, vec,
                  _const_spec((D_RNN, D_MODEL))],
        out_specs=[row_spec, tail_spec, tail_spec],
        out_shape=[jax.ShapeDtypeStruct((n, D_MODEL), F32),
                   jax.ShapeDtypeStruct((n_seq * V7X_SUBLANES, D_RNN), F32),
                   jax.ShapeDtypeStruct((n_seq * V7X_SUBLANES, D_RNN), F32)],
        scratch_shapes=[pltpu.VMEM((tm + V7X_SUBLANES, D_RNN), F32),
                        pltpu.VMEM((V7X_SUBLANES, D_RNN), F32)],
        compiler_params=_params(),
        name="mixer_c_prompt",
    )(x, npre, npost, win, cw, cb, wa, ba, wx, bx, lam, wout)


def _mixer_c_sample_kernel(x_ref, npre_ref, npost_ref, win_ref, cw_ref, cb_ref, wa_ref, ba_ref,
                           wx_ref, bx_ref, lam_ref, wout_ref, conv_ref, h0_ref,
                           o_ref, conv_out_ref, h_out_ref):
    x = x_ref[...]
    h = _rms(x, npre_ref[...]).astype(BF16)
    xr = _dot(h, win_ref[:, :D_RNN])
    cw = cw_ref[...]
    xc = cb_ref[...] + xr * cw[CONV_W - 1:CONV_W, :]
    for tap in range(CONV_W - 1):
        xc = xc + conv_ref[tap] * cw[tap:tap + 1, :]
        if tap > 0:
            conv_out_ref[tap - 1] = conv_ref[tap]
    conv_out_ref[CONV_W - 2] = xr
    a, mult, gx = _rglru_gates(xc, wa_ref, ba_ref, wx_ref, bx_ref, lam_ref)
    hs = a * h0_ref[...] + mult * gx
    h_out_ref[...] = hs
    gate = _silu(_dot(h, win_ref[:, D_RNN:]))
    y = _dot((hs * gate).astype(BF16), wout_ref[...])
    o_ref[...] = x + _rms(y, npost_ref[...])


def _mixer_c_sample(x, npre, npost, win, cw, cb, wa, ba, wx, bx, lam, wout, conv_t, h0):
    n = x.shape[0]
    return pl.pallas_call(
        _mixer_c_sample_kernel,
        out_shape=[jax.ShapeDtypeStruct((n, D_MODEL), F32),
                   jax.ShapeDtypeStruct((CONV_W - 1, n, D_RNN), F32),
                   jax.ShapeDtypeStruct((n, D_RNN), F32)],
        compiler_params=pltpu.CompilerParams(vmem_limit_bytes=VMEM_LIMIT_BYTES),
        name="mixer_c_sample",
    )(x, npre, npost, win, cw, cb, wa, ba, wx, bx, lam, wout, conv_t, h0)


def kernel(x_prompt, x_sample, cache_b_k, cache_b_v, state_c_conv, state_c_h, norm_pre, norm_post,
           a_w_in, a_ln_g, a_ln_b, a_w_s, a_b_s, a_w_out, b_w_in, b_sinks, b_w_out, c_w_in,
           c_conv_w, c_conv_b, c_w_a, c_b_a, c_w_x, c_b_x, c_lam, c_w_out):
    batch, seq, _ = x_prompt.shape
    dec_batch, dec_seq, _ = x_sample.shape
    past_len = PAST_LEN
    w_buf = cache_b_k.shape[2]
    assert dec_seq == 1 and w_buf == WINDOW and seq % ROW_TILE == 0 and seq % A_ROW_TILE == 0
    assert ROW_TILE % B_PROJ_ROWS == 0 and B_PROJ_ROWS % WINDOW == 0
    assert A_ROW_TILE % A_SUB_ROWS == 0 and A_SUB_ROWS % CHUNK == 0

    xp = x_prompt.reshape(batch * seq, D_MODEL)
    xs = x_sample.reshape(dec_batch, D_MODEL)
    row = lambda a: a.reshape(1, -1)

    cos_p, sin_p = _rope_tables(jnp.arange(seq, dtype=jnp.int32))
    cos_s, sin_s = _rope_tables(past_len + jnp.arange(dec_seq, dtype=jnp.int32))

    a_v_s = []
    b_kp, b_vp, b_ks, b_vs = [], [], [], []
    c_cp, c_hp, c_cs, c_hs = [], [], [], []
    for i in range(DEPTH):
        kind, j = i % N_MIXERS, i // N_MIXERS
        npre, npost = row(norm_pre[i]), row(norm_post[i])
        if kind == 0:
            args = (npre, npost, a_w_in[j].astype(BF16), row(a_ln_g[j]), row(a_ln_b[j]),
                    a_w_s[j], a_b_s[j].T, a_w_out[j].astype(BF16))
            (xp,) = _mixer_a(xp, *args, tm=A_ROW_TILE, sample=False)
            xs, vs = _mixer_a(xs, *args, tm=dec_batch, sample=True)
            a_v_s.append(vs.reshape(dec_batch, dec_seq, D_A))
        elif kind == 1:
            win, wout = b_w_in[j].astype(BF16), b_w_out[j].astype(BF16)
            xp, kp, vp = _mixer_b_prompt(xp, npre, npost, win, row(b_sinks[j]), cos_p, sin_p, wout,
                                         tm=ROW_TILE, seq=seq)
            b_kp.append(kp.reshape(batch, WINDOW, N_KV_HEADS, HEAD_DIM))
            b_vp.append(vp.reshape(batch, WINDOW, N_KV_HEADS, HEAD_DIM))

            qm, kn, vn, gate = _mixer_b_sample_proj(xs, npre, win, cos_s, sin_s)
            qm = qm.reshape(dec_batch, N_HEADS, V7X_LANES)
            om, ks_new, vs_new = _mixer_b_sample_attn(
                qm, cache_b_k[j].reshape(dec_batch, w_buf, KD),
                cache_b_v[j].reshape(dec_batch, w_buf, KD),
                kn.reshape(dec_batch, 1, KD), vn.reshape(dec_batch, 1, KD),
                b_sinks[j].reshape(N_HEADS, 1), bt=SAMPLE_BATCH_TILE)
            xs = _mixer_b_sample_out(xs, om.reshape(dec_batch * N_HEADS, V7X_LANES), gate, wout,
                                     npost)
            b_ks.append(ks_new.reshape(dec_batch, WINDOW, N_KV_HEADS, HEAD_DIM))
            b_vs.append(vs_new.reshape(dec_batch, WINDOW, N_KV_HEADS, HEAD_DIM))
        else:
            args = (npre, npost, c_w_in[j].astype(BF16), c_conv_w[j], row(c_conv_b[j]),
                    c_w_a[j].astype(BF16), row(c_b_a[j]), c_w_x[j].astype(BF16), row(c_b_x[j]),
                    row(c_lam[j]), c_w_out[j].astype(BF16))
            xp, conv_tail, h_tail = _mixer_c_prompt(xp, *args, tm=ROW_TILE, seq=seq)
            c_cp.append(conv_tail.reshape(batch, V7X_SUBLANES, D_RNN)[:, V7X_SUBLANES - (CONV_W - 1):])
            c_hp.append(h_tail.reshape(batch, V7X_SUBLANES, D_RNN)[:, V7X_SUBLANES - 1])
            xs, conv_new, h_new = _mixer_c_sample(
                xs, *args, jnp.transpose(state_c_conv[j], (1, 0, 2)), state_c_h[j])
            c_cs.append(jnp.transpose(conv_new, (1, 0, 2)))
            c_hs.append(h_new)

    return (xp.reshape(batch, seq, D_MODEL), xs.reshape(dec_batch, dec_seq, D_MODEL),
            jnp.stack(a_v_s), jnp.stack(b_kp), jnp.stack(b_vp), jnp.stack(b_ks), jnp.stack(b_vs),
            jnp.stack(c_cp), jnp.stack(c_hp), jnp.stack(c_cs), jnp.stack(c_hs))
```

```python
import functools

import jax
import jax.numpy as jnp
import numpy as np
from jax import lax
from jax.experimental import pallas as pl
from jax.experimental.pallas import tpu as pltpu

D_MODEL = 1024
DEPTH = 4
N_MIXERS = 3
NORM_EPS = 1e-6
LN_EPS = 1e-5

D_A = 2 * D_MODEL
CHUNK = 128
A_GROUPS = 4
A_GROUP_W = D_A // A_GROUPS

HEAD_DIM = 64
N_HEADS = D_MODEL // HEAD_DIM
N_KV_HEADS = N_HEADS // 8
Q_PER_KV = N_HEADS // N_KV_HEADS
WINDOW = 128
ROPE_THETA = 10000.0
QD = N_HEADS * HEAD_DIM
KD = N_KV_HEADS * HEAD_DIM

D_RNN = D_MODEL
C_BLOCKS = 4
C_BLOCK_W = D_RNN // C_BLOCKS
CONV_W = 4
LRU_C = 8.0

PAST_LEN = 8192

V7X_LANES = 128
V7X_SUBLANES = 8
V7X_VMEM_BYTES = 64 * 1024 * 1024
VMEM_LIMIT_BYTES = V7X_VMEM_BYTES - 8 * 1024 * 1024

ROW_TILE = 1024
A_SUB_ROWS = 256
B_PROJ_ROWS = 256
SAMPLE_BATCH_TILE = 32

BF16 = jnp.bfloat16
F32 = jnp.float32
SQRT_2_OVER_PI = np.float32(np.sqrt(2.0 / np.pi))


def _dot(a, b):
    return jnp.dot(a, b.astype(BF16), preferred_element_type=F32)


def _cast_weights(dst_ref, src_ref):
    if len(src_ref.shape) == 3:
        for blk in range(src_ref.shape[0]):
            dst_ref[blk] = src_ref[blk].astype(BF16)
    else:
        for c0 in range(0, src_ref.shape[1], 2 * V7X_LANES):
            dst_ref[:, c0:c0 + 2 * V7X_LANES] = src_ref[:, c0:c0 + 2 * V7X_LANES].astype(BF16)


def _rms(x, g):
    return x * lax.rsqrt(jnp.mean(x * x, axis=-1, keepdims=True) + NORM_EPS) * g


def _gelu(x):
    return x * (0.5 * (1.0 + jnp.tanh(SQRT_2_OVER_PI * (x + 0.044715 * (x * x * x)))))


def _sigmoid(x):
    return 1.0 / (1.0 + jnp.exp(-x))


def _silu(x):
    return x * _sigmoid(x)


def _softplus(x):
    return jnp.maximum(x, 0.0) + jnp.log1p(jnp.exp(-jnp.abs(x)))


def _const_spec(shape):
    zeros = (0,) * len(shape)
    return pl.BlockSpec(shape, lambda i: zeros, pipeline_mode=pl.Buffered(1))


def _params():
    return pltpu.CompilerParams(dimension_semantics=("arbitrary",),
                                vmem_limit_bytes=VMEM_LIMIT_BYTES)


def _layer_norm_a(v, lng_ref, lnb_ref):
    d = v - jnp.mean(v, axis=-1, keepdims=True)
    var = jnp.mean(d * d, axis=-1, keepdims=True)
    return d * lax.rsqrt(var + LN_EPS) * lng_ref[...] + lnb_ref[...]


def _mixer_a_sample_kernel(x_ref, npre_ref, npost_ref, win_ref, lng_ref, lnb_ref, ws_ref, bst_ref,
                           wout_ref, o_ref, v_ref):
    x = x_ref[...]
    h = _rms(x, npre_ref[...]).astype(BF16)
    vn = _layer_norm_a(_gelu(_dot(h, win_ref[:, D_A:2 * D_A])), lng_ref, lnb_ref)
    v_ref[...] = vn
    acc = jnp.zeros(x.shape, F32)
    for g in range(A_GROUPS):
        c0 = g * A_GROUP_W
        u = _gelu(_dot(h, win_ref[:, c0:c0 + A_GROUP_W]))
        gate = _silu(_dot(h, win_ref[:, 2 * D_A + c0:2 * D_A + c0 + A_GROUP_W]))
        mixed = ws_ref[g][0:1, 0:1] * vn[:, c0:c0 + A_GROUP_W] + bst_ref[0:1, g:g + 1]
        acc = acc + _dot((u * mixed * gate).astype(BF16), wout_ref[c0:c0 + A_GROUP_W, :])
    o_ref[...] = x + _rms(acc, npost_ref[...])


def _mixer_a_prompt_kernel(x_ref, npre_ref, npost_ref, win_ref, lng_ref, lnb_ref, ws_ref, bst_ref,
                           wout_ref, o_ref, *, tm, sub):
    x = x_ref[...]
    h = _rms(x, npre_ref[...]).astype(BF16)
    row = lax.broadcasted_iota(jnp.int32, (CHUNK, CHUNK), 0)
    col = lax.broadcasted_iota(jnp.int32, (CHUNK, CHUNK), 1)
    causal = row >= col
    n_sub = tm // sub
    items = [(r, g) for r in range(n_sub) for g in range(A_GROUPS)]
    v_parts = {r: [] for r in range(n_sub)}
    vnb, acc, ws_masked = {}, {}, {}

    def sub_rows(r):
        return slice(r * sub, (r + 1) * sub)

    def v_part(r, c):
        c0 = D_A + c * A_GROUP_W
        v_parts[r].append(_gelu(_dot(h[sub_rows(r)], win_ref[:, c0:c0 + A_GROUP_W])))

    def v_finish(r):
        v = jnp.concatenate(v_parts.pop(r), axis=1)
        vnb[r] = _layer_norm_a(v, lng_ref, lnb_ref).astype(BF16)

    def front(r, g):
        c0 = g * A_GROUP_W
        hr = h[sub_rows(r)]
        u = _dot(hr, win_ref[:, c0:c0 + A_GROUP_W])
        gate = _dot(hr, win_ref[:, 2 * D_A + c0:2 * D_A + c0 + A_GROUP_W])
        if g not in ws_masked:
            ws_masked[g] = jnp.where(causal, ws_ref[g], 0.0).astype(BF16)
        mixed = jnp.concatenate(
            [_dot(ws_masked[g], vnb[r][c * CHUNK:(c + 1) * CHUNK, c0:c0 + A_GROUP_W])
             for c in range(sub // CHUNK)], axis=0)
        return u, gate, mixed

    def back(r, g, u, gate, mixed):
        c0 = g * A_GROUP_W
        bias = jnp.concatenate([bst_ref[:, g:g + 1]] * (sub // CHUNK), axis=0)
        z = (_gelu(u) * (mixed + bias) * _silu(gate)).astype(BF16)
        y = _dot(z, wout_ref[c0:c0 + A_GROUP_W, :])
        acc[r] = y if g == 0 else acc[r] + y
        if g == A_GROUPS - 1:
            rows = sub_rows(r)
            o_ref[rows, :] = x[rows] + _rms(acc.pop(r), npost_ref[...])

    for c in range(A_GROUPS):
        v_part(0, c)
    v_finish(0)
    nxt = front(*items[0])
    for k, (r, g) in enumerate(items):
        cur = nxt
        if r + 1 < n_sub:
            v_part(r + 1, g)
            if g == A_GROUPS - 1:
                v_finish(r + 1)
        if k + 1 < len(items):
            nxt = front(*items[k + 1])
        back(r, g, *cur)


def _layer_spec(shape, layer):
    zeros = (0,) * len(shape)
    return pl.BlockSpec((None,) + tuple(shape), lambda i: (layer,) + zeros,
                        pipeline_mode=pl.Buffered(1))


def _mixer_a(x, npre, npost, win, lng, lnb, ws, bst, wout, *, layer, tm, sample):
    n = x.shape[0]
    row_spec = pl.BlockSpec((tm, D_MODEL), lambda i: (i, 0))
    out_shape = [jax.ShapeDtypeStruct((n, D_MODEL), F32)]
    out_specs = [row_spec]
    if sample:
        out_shape.append(jax.ShapeDtypeStruct((n, D_A), F32))
        out_specs.append(pl.BlockSpec((tm, D_A), lambda i: (i, 0)))
    return pl.pallas_call(
        _mixer_a_sample_kernel if sample
        else functools.partial(_mixer_a_prompt_kernel, tm=tm, sub=A_SUB_ROWS),
        grid=(n // tm,),
        in_specs=[row_spec, _const_spec((1, D_MODEL)), _const_spec((1, D_MODEL)),
                  _layer_spec((D_MODEL, 3 * D_A), layer), _const_spec((1, D_A)),
                  _const_spec((1, D_A)), _const_spec((A_GROUPS, CHUNK, CHUNK)),
                  _const_spec((CHUNK, A_GROUPS)), _layer_spec((D_A, D_MODEL), layer)],
        out_specs=out_specs, out_shape=out_shape,
        compiler_params=_params(),
        name="mixer_a_sample" if sample else "mixer_a_prompt",
    )(x, npre, npost, win, lng, lnb, ws, bst, wout)


def _rope_tables(positions):
    half = HEAD_DIM // 2
    inv_freq = ROPE_THETA ** (-jnp.arange(half, dtype=F32) / half)
    ang = positions.astype(F32)[:, None] * inv_freq[None, :]
    cos, sin = jnp.cos(ang), jnp.sin(ang)
    reps = V7X_LANES // HEAD_DIM
    cos_t = jnp.tile(jnp.concatenate([cos, cos], axis=-1), (1, reps))
    sin_t = jnp.tile(jnp.concatenate([-sin, sin], axis=-1), (1, reps))
    return cos_t, sin_t


def _rope(x, cos, sin_signed):
    lane = lax.broadcasted_iota(jnp.int32, x.shape, 1)
    first_half = (lane % HEAD_DIM) < (HEAD_DIM // 2)
    rot = jnp.where(first_half, pltpu.roll(x, V7X_LANES - HEAD_DIM // 2, 1),
                    pltpu.roll(x, HEAD_DIM // 2, 1))
    return x * cos + rot * sin_signed


def _dup_kv_halves(x, lane):
    swapped = pltpu.roll(x, HEAD_DIM, 1)
    low = lane < HEAD_DIM
    return jnp.where(low, x, swapped), jnp.where(low, swapped, x)


def _mixer_b_prompt_kernel(x_ref, npre_ref, npost_ref, win_ref, sinks_ref, cos_ref, sin_ref,
                           wout_ref, o_ref, kout_ref, vout_ref,
                           k0_ref, k1_ref, vt0_ref, vt1_ref, att_ref, win_bf_ref, wout_bf_ref,
                           *, tm, tiles_per_seq):
    @pl.when(pl.program_id(0) == 0)
    def _():
        _cast_weights(win_bf_ref, win_ref)
        _cast_weights(wout_bf_ref, wout_ref)

    win_ref, wout_ref = win_bf_ref, wout_bf_ref
    first = (pl.program_id(0) % tiles_per_seq) == 0

    @pl.when(first)
    def _():
        for ref in (k0_ref, k1_ref):
            ref[0:WINDOW, :] = jnp.zeros((WINDOW, V7X_LANES), BF16)
        for ref in (vt0_ref, vt1_ref):
            ref[:, 0:WINDOW] = jnp.zeros((V7X_LANES, WINDOW), BF16)

    x = x_ref[...]
    h = _rms(x, npre_ref[...]).astype(BF16)
    lane = lax.broadcasted_iota(jnp.int32, (tm, V7X_LANES), 1)
    cos, sin = cos_ref[...], sin_ref[...]

    k = _rope(_dot(h, win_ref[:, QD:QD + KD]), cos, sin)
    v = _dot(h, win_ref[:, QD + KD:QD + 2 * KD])
    kout_ref[...] = k[tm - WINDOW:, :]
    vout_ref[...] = v[tm - WINDOW:, :]
    kd0, kd1 = _dup_kv_halves(k, lane)
    k0_ref[WINDOW:, :] = kd0.astype(BF16)
    k1_ref[WINDOW:, :] = kd1.astype(BF16)
    vt = v.T
    vt0_ref[:, WINDOW:] = jnp.concatenate([vt[:HEAD_DIM], vt[:HEAD_DIM]], axis=0).astype(BF16)
    vt1_ref[:, WINDOW:] = jnp.concatenate([vt[HEAD_DIM:], vt[HEAD_DIM:]], axis=0).astype(BF16)

    ci = lax.broadcasted_iota(jnp.int32, (2 * WINDOW, 2 * WINDOW), 0)
    qi = lax.broadcasted_iota(jnp.int32, (2 * WINDOW, 2 * WINDOW), 1) % WINDOW
    band = (ci >= qi) & (ci <= qi + WINDOW)
    bias = jnp.where(band, 0.0, -jnp.inf)
    bias_first = jnp.where(band & (ci >= jnp.where(first, WINDOW, 0)), 0.0, -jnp.inf)
    head_lane = lax.broadcasted_iota(jnp.int32, (1, 2 * WINDOW), 1) < WINDOW
    low_lanes = lax.broadcasted_iota(jnp.int32, (WINDOW, V7X_LANES), 1) < HEAD_DIM
    top_rows = lax.broadcasted_iota(jnp.int32, (V7X_LANES, WINDOW), 0) < HEAD_DIM
    scale = HEAD_DIM ** -0.5

    n_blocks = tm // WINDOW
    n_pairs = N_HEADS // 2
    chunk = 2 * V7X_LANES
    n_chunks = QD // chunk
    bpg = B_PROJ_ROWS // WINDOW
    n_groups = tm // B_PROJ_ROWS
    items = [(j, p) for j in range(n_blocks) for p in range(n_pairs)]
    q_chunks, gate_chunks, gated, y_chunks = {}, {}, {}, {}

    def block_rows(j):
        return slice(j * WINDOW, (j + 1) * WINDOW)

    def group_rows(g):
        return slice(g * B_PROJ_ROWS, (g + 1) * B_PROJ_ROWS)

    def project_q(g, c):
        q_chunks[g, c] = _dot(h[group_rows(g)], win_ref[:, c * chunk:(c + 1) * chunk])

    def project_gate(g, c):
        c0 = QD + 2 * KD + c * chunk
        gate_chunks[g, c] = _silu(_dot(h[group_rows(g)], win_ref[:, c0:c0 + chunk]))

    def project_out(g, c):
        if c == 0:
            gated[g] = jnp.concatenate(
                [att_ref[group_rows(g), cc * chunk:(cc + 1) * chunk] * gate_chunks.pop((g, cc))
                 for cc in range(n_chunks)], axis=1).astype(BF16)
        y_chunks[g, c] = _dot(gated[g], wout_ref[:, c * chunk:(c + 1) * chunk])
        if c == n_chunks - 1:
            y = jnp.concatenate([y_chunks.pop((g, cc)) for cc in range(n_chunks)], axis=1)
            rows = group_rows(g)
            o_ref[rows, :] = x[rows] + _rms(y, npost_ref[...])

    def scores(j, p):
        rows = block_rows(j)
        kref = k0_ref if (2 * p) // Q_PER_KV == 0 else k1_ref
        half = (p % 2) * V7X_LANES
        r0 = (j % bpg) * WINDOW
        qp = q_chunks[j // bpg, p // 2][r0:r0 + WINDOW, half:half + V7X_LANES]
        qp = _rope(qp, cos[rows], sin[rows]) * scale
        qs = jnp.concatenate([jnp.where(low_lanes, qp, 0.0), jnp.where(low_lanes, 0.0, qp)],
                             axis=0).astype(BF16)
        st = lax.dot_general(kref[j * WINDOW:(j + 2) * WINDOW, :], qs, (((1,), (1,)), ((), ())),
                             preferred_element_type=F32)
        return st + (bias_first if j == 0 else bias)

    def attend(j, p, st):
        vtref = vt0_ref if (2 * p) // Q_PER_KV == 0 else vt1_ref
        sink = jnp.where(head_lane, sinks_ref[0, 2 * p], sinks_ref[0, 2 * p + 1])
        m = jnp.maximum(jnp.max(st, axis=0, keepdims=True), sink)
        e = jnp.exp(st - m)
        inv = 1.0 / (jnp.sum(e, axis=0, keepdims=True) + jnp.exp(sink - m))
        return _dot(vtref[:, j * WINDOW:(j + 2) * WINDOW], e.astype(BF16)), inv

    def finish(j, p, ot, inv):
        ot = ot * inv
        pair = jnp.where(top_rows, ot[:, :WINDOW], ot[:, WINDOW:])
        att_ref[block_rows(j), p * V7X_LANES:(p + 1) * V7X_LANES] = pair.T

    items_per_group = bpg * n_pairs
    side = {}
    for g in range(n_groups):
        tasks = []
        for c in range(n_chunks):
            if g + 1 < n_groups:
                tasks.append((project_q, g + 1, c))
            tasks.append((project_gate, g, c))
            if g > 0:
                tasks.append((project_out, g - 1, c))
        for t, task in enumerate(tasks):
            at = g * items_per_group + (t * items_per_group) // len(tasks)
            side.setdefault(at, []).append(task)

    for c in range(n_chunks):
        project_q(0, c)
    st_next = scores(*items[0])
    pending = None
    for i, (j, p) in enumerate(items):
        st = st_next
        for fn, g, c in side.get(i, []):
            if fn is not project_out:
                fn(g, c)
        if i + 1 < len(items):
            st_next = scores(*items[i + 1])
        ot, inv = attend(j, p, st)
        if pending is not None:
            finish(*pending)
        pending = (j, p, ot, inv)
        for fn, g, c in side.get(i, []):
            if fn is project_out:
                fn(g, c)
    finish(*pending)
    for c in range(n_chunks):
        project_out(n_groups - 1, c)

    for ref in (k0_ref, k1_ref):
        ref[0:WINDOW, :] = ref[tm:tm + WINDOW, :]
    for ref in (vt0_ref, vt1_ref):
        ref[:, 0:WINDOW] = ref[:, tm:tm + WINDOW]


def _mixer_b_prompt(x, npre, npost, win, sinks, cos, sin, wout, *, tm, seq):
    n = x.shape[0]
    tiles_per_seq = seq // tm
    n_seq = n // seq
    row_spec = pl.BlockSpec((tm, D_MODEL), lambda i: (i, 0))
    rope_spec = pl.BlockSpec((tm, V7X_LANES), lambda i: (i % tiles_per_seq, 0))
    kv_spec = pl.BlockSpec((WINDOW, KD), lambda i: (i // tiles_per_seq, 0))
    return pl.pallas_call(
        functools.partial(_mixer_b_prompt_kernel, tm=tm, tiles_per_seq=tiles_per_seq),
        grid=(n // tm,),
        in_specs=[row_spec, _const_spec((1, D_MODEL)), _const_spec((1, D_MODEL)),
                  _const_spec((D_MODEL, 2 * QD + 2 * KD)),
                  pl.BlockSpec(memory_space=pltpu.SMEM),
                  rope_spec, rope_spec, _const_spec((QD, D_MODEL))],
        out_specs=[row_spec, kv_spec, kv_spec],
        out_shape=[jax.ShapeDtypeStruct((n, D_MODEL), F32),
                   jax.ShapeDtypeStruct((n_seq * WINDOW, KD), F32),
                   jax.ShapeDtypeStruct((n_seq * WINDOW, KD), F32)],
        scratch_shapes=[pltpu.VMEM((tm + WINDOW, V7X_LANES), BF16)] * 2
        + [pltpu.VMEM((V7X_LANES, tm + WINDOW), BF16)] * 2 + [pltpu.VMEM((tm, QD), F32)]
        + [pltpu.VMEM(win.shape, BF16), pltpu.VMEM(wout.shape, BF16)],
        compiler_params=_params(),
        name="mixer_b_prompt",
    )(x, npre, npost, win, sinks, cos, sin, wout)


def _mixer_b_sample_proj_kernel(x_ref, npre_ref, win_ref, cos_ref, sin_ref,
                                q_ref, k_ref, v_ref, gate_ref):
    n = x_ref.shape[0]
    h = _rms(x_ref[...], npre_ref[...]).astype(BF16)
    cos, sin = cos_ref[...], sin_ref[...]
    scale = HEAD_DIM ** -0.5
    low_lanes = lax.broadcasted_iota(jnp.int32, (n, V7X_LANES), 1) < HEAD_DIM
    for p in range(N_HEADS // 2):
        qp = _rope(_dot(h, win_ref[:, p * V7X_LANES:(p + 1) * V7X_LANES]), cos, sin) * scale
        swapped = pltpu.roll(qp, HEAD_DIM, 1)
        if (2 * p) // Q_PER_KV == 0:
            even, odd = jnp.where(low_lanes, qp, 0.0), jnp.where(low_lanes, swapped, 0.0)
        else:
            even, odd = jnp.where(low_lanes, 0.0, swapped), jnp.where(low_lanes, 0.0, qp)
        q_ref[pl.ds(2 * p, n, stride=N_HEADS), :] = even
        q_ref[pl.ds(2 * p + 1, n, stride=N_HEADS), :] = odd
    k_ref[...] = _rope(_dot(h, win_ref[:, QD:QD + KD]), cos, sin)
    v_ref[...] = _dot(h, win_ref[:, QD + KD:QD + 2 * KD])
    gate_ref[...] = _silu(_dot(h, win_ref[:, QD + 2 * KD:]))


def _mixer_b_sample_proj(x, npre, win, cos_row, sin_row):
    n = x.shape[0]
    return pl.pallas_call(
        _mixer_b_sample_proj_kernel,
        out_shape=[jax.ShapeDtypeStruct((n * N_HEADS, V7X_LANES), F32),
                   jax.ShapeDtypeStruct((n, KD), F32),
                   jax.ShapeDtypeStruct((n, KD), F32), jax.ShapeDtypeStruct((n, QD), F32)],
        compiler_params=pltpu.CompilerParams(vmem_limit_bytes=VMEM_LIMIT_BYTES),
        name="mixer_b_sample_proj",
    )(x, npre, win, cos_row, sin_row)


def _mixer_b_sample_attn_kernel(q_ref, kc_ref, vc_ref, kn_ref, vn_ref, sinks_ref,
                                o_ref, ko_ref, vo_ref):
    q = q_ref[...]
    kc, vc = kc_ref[...], vc_ref[...]
    kn, vn = kn_ref[...], vn_ref[...]
    qb = q.astype(BF16)
    s_c = jnp.einsum("bhl,bsl->bhs", qb, kc.astype(BF16), preferred_element_type=F32)
    s_n = jnp.sum(qb.astype(F32) * kn.astype(BF16).astype(F32), axis=-1, keepdims=True)
    sink = sinks_ref[...][None]
    m = jnp.maximum(jnp.maximum(jnp.max(s_c, axis=-1, keepdims=True), s_n), sink)
    e_c = jnp.exp(s_c - m)
    e_n = jnp.exp(s_n - m)
    inv = 1.0 / (jnp.sum(e_c, axis=-1, keepdims=True) + e_n + jnp.exp(sink - m))
    o = jnp.einsum("bhs,bsl->bhl", (e_c * inv).astype(BF16), vc.astype(BF16),
                   preferred_element_type=F32)
    p_n = (e_n * inv).astype(BF16).astype(F32)
    o_ref[...] = o + p_n * vn.astype(BF16).astype(F32)
    w = kc.shape[1]
    ko_ref[:, 0:w - 1, :] = kc[:, 1:w, :]
    ko_ref[:, w - 1:w, :] = kn
    vo_ref[:, 0:w - 1, :] = vc[:, 1:w, :]
    vo_ref[:, w - 1:w, :] = vn


def _mixer_b_sample_attn(qm, kc, vc, kn, vn, sinks_col, *, bt):
    b, w, _ = kc.shape
    q_spec = pl.BlockSpec((bt, N_HEADS, V7X_LANES), lambda i: (i, 0, 0))
    c_spec = pl.BlockSpec((bt, w, KD), lambda i: (i, 0, 0))
    n_spec = pl.BlockSpec((bt, 1, KD), lambda i: (i, 0, 0))
    return pl.pallas_call(
        _mixer_b_sample_attn_kernel,
        grid=(b // bt,),
        in_specs=[q_spec, c_spec, c_spec, n_spec, n_spec,
                  pl.BlockSpec((N_HEADS, 1), lambda i: (0, 0))],
        out_specs=[q_spec, c_spec, c_spec],
        out_shape=[jax.ShapeDtypeStruct((b, N_HEADS, V7X_LANES), F32),
                   jax.ShapeDtypeStruct((b, w, KD), F32), jax.ShapeDtypeStruct((b, w, KD), F32)],
        compiler_params=_params(),
        name="mixer_b_sample_attn",
    )(qm, kc, vc, kn, vn, sinks_col)


def _mixer_b_sample_out_kernel(x_ref, att_ref, gate_ref, wout_ref, npost_ref, o_ref):
    n = x_ref.shape[0]
    low_lanes = lax.broadcasted_iota(jnp.int32, (n, V7X_LANES), 1) < HEAD_DIM
    pairs = []
    for p in range(N_HEADS // 2):
        even = att_ref[pl.ds(2 * p, n, stride=N_HEADS), :]
        odd = att_ref[pl.ds(2 * p + 1, n, stride=N_HEADS), :]
        if (2 * p) // Q_PER_KV == 0:
            odd = pltpu.roll(odd, HEAD_DIM, 1)
        else:
            even = pltpu.roll(even, HEAD_DIM, 1)
        pairs.append(jnp.where(low_lanes, even, odd))
    att = jnp.concatenate(pairs, axis=1)
    y = _dot((att * gate_ref[...]).astype(BF16), wout_ref[...])
    o_ref[...] = x_ref[...] + _rms(y, npost_ref[...])


def _mixer_b_sample_out(x, att, gate, wout, npost):
    return pl.pallas_call(
        _mixer_b_sample_out_kernel,
        out_shape=jax.ShapeDtypeStruct(x.shape, F32),
        compiler_params=pltpu.CompilerParams(vmem_limit_bytes=VMEM_LIMIT_BYTES),
        name="mixer_b_sample_out",
    )(x, att, gate, wout, npost)


def _rglru_gates(xc, wa_ref, ba_ref, wx_ref, bx_ref, lam_ref):
    xcb = xc.astype(BF16)
    ra, rx = [], []
    for blk in range(C_BLOCKS):
        cols = slice(blk * C_BLOCK_W, (blk + 1) * C_BLOCK_W)
        ra.append(_dot(xcb[:, cols], wa_ref[blk]))
        rx.append(_dot(xcb[:, cols], wx_ref[blk]))
    r = _sigmoid(jnp.concatenate(ra, axis=-1) + ba_ref[...])
    i_gate = _sigmoid(jnp.concatenate(rx, axis=-1) + bx_ref[...])
    log_a = (-LRU_C * r) * _softplus(-lam_ref[...])
    a = jnp.exp(log_a)
    mult = jnp.sqrt(-jnp.tanh(log_a) * (a * a + 1.0))
    return a, mult, i_gate * xc


def _mixer_c_prompt_kernel(x_ref, npre_ref, npost_ref, win_ref, cw_ref, cb_ref, wa_ref, ba_ref,
                           wx_ref, bx_ref, lam_ref, wout_ref, o_ref, conv_ref, hlast_ref,
                           xpad_ref, h_ref, win_bf_ref, wa_bf_ref, wx_bf_ref, wout_bf_ref,
                           *, tm, tiles_per_seq):
    @pl.when(pl.program_id(0) == 0)
    def _():
        for dst, src in ((win_bf_ref, win_ref), (wa_bf_ref, wa_ref), (wx_bf_ref, wx_ref),
                         (wout_bf_ref, wout_ref)):
            _cast_weights(dst, src)

    win_ref, wa_ref, wx_ref, wout_ref = win_bf_ref, wa_bf_ref, wx_bf_ref, wout_bf_ref
    first = (pl.program_id(0) % tiles_per_seq) == 0

    @pl.when(first)
    def _():
        xpad_ref[0:V7X_SUBLANES, :] = jnp.zeros((V7X_SUBLANES, D_RNN), F32)
        h_ref[...] = jnp.zeros((V7X_SUBLANES, D_RNN), F32)

    x = x_ref[...]
    h = _rms(x, npre_ref[...]).astype(BF16)
    xr = _dot(h, win_ref[:, :D_RNN])
    xpad_ref[V7X_SUBLANES:, :] = xr
    conv_ref[...] = xr[tm - V7X_SUBLANES:, :]
    cw = cw_ref[...]
    xc = cb_ref[...] + xr * cw[CONV_W - 1:CONV_W, :]
    for tap in range(CONV_W - 1):
        back = CONV_W - 1 - tap
        xc = xc + xpad_ref[V7X_SUBLANES - back:V7X_SUBLANES - back + tm, :] * cw[tap:tap + 1, :]
    xpad_ref[0:V7X_SUBLANES, :] = xr[tm - V7X_SUBLANES:, :]

    a, mult, gx = _rglru_gates(xc, wa_ref, ba_ref, wx_ref, bx_ref, lam_ref)
    row = lax.broadcasted_iota(jnp.int32, (tm, 1), 0)
    mult = jnp.where(first & (row == 0), 1.0, mult)
    b = mult * gx

    n_groups = tm // V7X_SUBLANES
    a = a.reshape(n_groups, V7X_SUBLANES, D_RNN)
    b = b.reshape(n_groups, V7X_SUBLANES, D_RNN)
    sub = lax.broadcasted_iota(jnp.int32, (n_groups, V7X_SUBLANES, D_RNN), 1)
    for dist in (1, 2, 4):
        keep = sub >= dist
        a_prev = pltpu.roll(a, dist, 1)
        b_prev = pltpu.roll(b, dist, 1)
        b = jnp.where(keep, a * b_prev + b, b)
        a = jnp.where(keep, a * a_prev, a)
    carry = h_ref[V7X_SUBLANES - 1:V7X_SUBLANES, :]
    groups = []
    for gi in range(n_groups):
        hs_g = a[gi] * carry + b[gi]
        carry = hs_g[V7X_SUBLANES - 1:V7X_SUBLANES, :]
        groups.append(hs_g)
    h_ref[...] = groups[-1]
    hlast_ref[...] = groups[-1]
    hs = jnp.concatenate(groups, axis=0)

    gate = _silu(_dot(h, win_ref[:, D_RNN:]))
    y = _dot((hs * gate).astype(BF16), wout_ref[...])
    o_ref[...] = x + _rms(y, npost_ref[...])


def _mixer_c_prompt(x, npre, npost, win, cw, cb, wa, ba, wx, bx, lam, wout, *, tm, seq):
    n = x.shape[0]
    tiles_per_seq = seq // tm
    n_seq = n // seq
    row_spec = pl.BlockSpec((tm, D_MODEL), lambda i: (i, 0))
    tail_spec = pl.BlockSpec((V7X_SUBLANES, D_RNN), lambda i: (i // tiles_per_seq, 0))
    vec = _const_spec((1, D_RNN))
    blk = _const_spec((C_BLOCKS, C_BLOCK_W, C_BLOCK_W))
    return pl.pallas_call(
        functools.partial(_mixer_c_prompt_kernel, tm=tm, tiles_per_seq=tiles_per_seq),
        grid=(n // tm,),
        in_specs=[row_spec, vec, vec, _const_spec((D_MODEL, 2 * D_RNN)),
                  _const_spec((CONV_W, D_RNN)), vec, blk, vec, blk, vec, vec,
                  _const_spec((D_RNN, D_MODEL))],
        out_specs=[row_spec, tail_spec, tail_spec],
        out_shape=[jax.ShapeDtypeStruct((n, D_MODEL), F32),
                   jax.ShapeDtypeStruct((n_seq * V7X_SUBLANES, D_RNN), F32),
                   jax.ShapeDtypeStruct((n_seq * V7X_SUBLANES, D_RNN), F32)],
        scratch_shapes=[pltpu.VMEM((tm + V7X_SUBLANES, D_RNN), F32),
                        pltpu.VMEM((V7X_SUBLANES, D_RNN), F32)]
        + [pltpu.VMEM(w.shape, BF16) for w in (win, wa, wx, wout)],
        compiler_params=_params(),
        name="mixer_c_prompt",
    )(x, npre, npost, win, cw, cb, wa, ba, wx, bx, lam, wout)


def _mixer_c_sample_kernel(x_ref, npre_ref, npost_ref, win_ref, cw_ref, cb_ref, wa_ref, ba_ref,
                           wx_ref, bx_ref, lam_ref, wout_ref, conv_ref, h0_ref,
                           o_ref, conv_out_ref, h_out_ref):
    x = x_ref[...]
    h = _rms(x, npre_ref[...]).astype(BF16)
    xr = _dot(h, win_ref[:, :D_RNN])
    cw = cw_ref[...]
    xc = cb_ref[...] + xr * cw[CONV_W - 1:CONV_W, :]
    for tap in range(CONV_W - 1):
        xc = xc + conv_ref[tap] * cw[tap:tap + 1, :]
        if tap > 0:
            conv_out_ref[tap - 1] = conv_ref[tap]
    conv_out_ref[CONV_W - 2] = xr
    a, mult, gx = _rglru_gates(xc, wa_ref, ba_ref, wx_ref, bx_ref, lam_ref)
    hs = a * h0_ref[...] + mult * gx
    h_out_ref[...] = hs
    gate = _silu(_dot(h, win_ref[:, D_RNN:]))
    y = _dot((hs * gate).astype(BF16), wout_ref[...])
    o_ref[...] = x + _rms(y, npost_ref[...])


def _mixer_c_sample(x, npre, npost, win, cw, cb, wa, ba, wx, bx, lam, wout, conv_t, h0):
    n = x.shape[0]
    return pl.pallas_call(
        _mixer_c_sample_kernel,
        out_shape=[jax.ShapeDtypeStruct((n, D_MODEL), F32),
                   jax.ShapeDtypeStruct((CONV_W - 1, n, D_RNN), F32),
                   jax.ShapeDtypeStruct((n, D_RNN), F32)],
        compiler_params=pltpu.CompilerParams(vmem_limit_bytes=VMEM_LIMIT_BYTES),
        name="mixer_c_sample",
    )(x, npre, npost, win, cw, cb, wa, ba, wx, bx, lam, wout, conv_t, h0)


def kernel(x_prompt, x_sample, cache_b_k, cache_b_v, state_c_conv, state_c_h, norm_pre, norm_post,
           a_w_in, a_ln_g, a_ln_b, a_w_s, a_b_s, a_w_out, b_w_in, b_sinks, b_w_out, c_w_in,
           c_conv_w, c_conv_b, c_w_a, c_b_a, c_w_x, c_b_x, c_lam, c_w_out):
    batch, seq, _ = x_prompt.shape
    dec_batch, dec_seq, _ = x_sample.shape
    past_len = PAST_LEN
    w_buf = cache_b_k.shape[2]
    assert dec_seq == 1 and w_buf == WINDOW and seq % ROW_TILE == 0
    assert ROW_TILE % B_PROJ_ROWS == 0 and B_PROJ_ROWS % WINDOW == 0
    assert ROW_TILE % A_SUB_ROWS == 0 and A_SUB_ROWS % CHUNK == 0

    xp = x_prompt.reshape(batch * seq, D_MODEL)
    xs = x_sample.reshape(dec_batch, D_MODEL)
    row = lambda a: a.reshape(1, -1)

    cos_p, sin_p = _rope_tables(jnp.arange(seq, dtype=jnp.int32))
    cos_s, sin_s = _rope_tables(past_len + jnp.arange(dec_seq, dtype=jnp.int32))

    a_w_in_bf, a_w_out_bf = a_w_in.astype(BF16), a_w_out.astype(BF16)
    a_v_s = []
    b_kp, b_vp, b_ks, b_vs = [], [], [], []
    c_cp, c_hp, c_cs, c_hs = [], [], [], []
    for i in range(DEPTH):
        kind, j = i % N_MIXERS, i // N_MIXERS
        npre, npost = row(norm_pre[i]), row(norm_post[i])
        if kind == 0:
            args = (npre, npost, a_w_in_bf, row(a_ln_g[j]), row(a_ln_b[j]),
                    a_w_s[j], a_b_s[j].T, a_w_out_bf)
            (xp,) = _mixer_a(xp, *args, layer=j, tm=ROW_TILE, sample=False)
            xs, vs = _mixer_a(xs, *args, layer=j, tm=dec_batch, sample=True)
            a_v_s.append(vs.reshape(dec_batch, dec_seq, D_A))
        elif kind == 1:
            win, wout = b_w_in[j], b_w_out[j]
            xp, kp, vp = _mixer_b_prompt(xp, npre, npost, win, row(b_sinks[j]), cos_p, sin_p, wout,
                                         tm=ROW_TILE, seq=seq)
            b_kp.append(kp.reshape(batch, WINDOW, N_KV_HEADS, HEAD_DIM))
            b_vp.append(vp.reshape(batch, WINDOW, N_KV_HEADS, HEAD_DIM))

            qm, kn, vn, gate = _mixer_b_sample_proj(xs, npre, win, cos_s, sin_s)
            qm = qm.reshape(dec_batch, N_HEADS, V7X_LANES)
            om, ks_new, vs_new = _mixer_b_sample_attn(
                qm, cache_b_k[j].reshape(dec_batch, w_buf, KD),
                cache_b_v[j].reshape(dec_batch, w_buf, KD),
                kn.reshape(dec_batch, 1, KD), vn.reshape(dec_batch, 1, KD),
                b_sinks[j].reshape(N_HEADS, 1), bt=SAMPLE_BATCH_TILE)
            xs = _mixer_b_sample_out(xs, om.reshape(dec_batch * N_HEADS, V7X_LANES), gate, wout,
                                     npost)
            b_ks.append(ks_new.reshape(dec_batch, WINDOW, N_KV_HEADS, HEAD_DIM))
            b_vs.append(vs_new.reshape(dec_batch, WINDOW, N_KV_HEADS, HEAD_DIM))
        else:
            args = (npre, npost, c_w_in[j], c_conv_w[j], row(c_conv_b[j]),
                    c_w_a[j], row(c_b_a[j]), c_w_x[j], row(c_b_x[j]),
                    row(c_lam[j]), c_w_out[j])
            xp, conv_tail, h_tail = _mixer_c_prompt(xp, *args, tm=ROW_TILE, seq=seq)
            c_cp.append(conv_tail.reshape(batch, V7X_SUBLANES, D_RNN)[:, V7X_SUBLANES - (CONV_W - 1):])
            c_hp.append(h_tail.reshape(batch, V7X_SUBLANES, D_RNN)[:, V7X_SUBLANES - 1])
            xs, conv_new, h_new = _mixer_c_sample(
                xs, *args, jnp.transpose(state_c_conv[j], (1, 0, 2)), state_c_h[j])
            c_cs.append(jnp.transpose(conv_new, (1, 0, 2)))
            c_hs.append(h_new)

    return (xp.reshape(batch, seq, D_MODEL), xs.reshape(dec_batch, dec_seq, D_MODEL),
            jnp.stack(a_v_s), jnp.stack(b_kp), jnp.stack(b_vp), jnp.stack(b_ks), jnp.stack(b_vs),
            jnp.stack(c_cp), jnp.stack(c_hp), jnp.stack(c_cs), jnp.stack(c_hs))
```

```python
import functools

import jax
import jax.numpy as jnp
import numpy as np
from jax import lax
from jax.experimental import pallas as pl
from jax.experimental.pallas import tpu as pltpu

D_MODEL = 1024
DEPTH = 4
N_MIXERS = 3
NORM_EPS = 1e-6
LN_EPS = 1e-5

D_A = 2 * D_MODEL
CHUNK = 128
A_GROUPS = 4
A_GROUP_W = D_A // A_GROUPS

HEAD_DIM = 64
N_HEADS = D_MODEL // HEAD_DIM
N_KV_HEADS = N_HEADS // 8
Q_PER_KV = N_HEADS // N_KV_HEADS
WINDOW = 128
ROPE_THETA = 10000.0
QD = N_HEADS * HEAD_DIM
KD = N_KV_HEADS * HEAD_DIM

D_RNN = D_MODEL
C_BLOCKS = 4
C_BLOCK_W = D_RNN // C_BLOCKS
CONV_W = 4
LRU_C = 8.0

PAST_LEN = 8192

V7X_LANES = 128
V7X_SUBLANES = 8
V7X_VMEM_BYTES = 64 * 1024 * 1024
VMEM_LIMIT_BYTES = V7X_VMEM_BYTES - 8 * 1024 * 1024

ROW_TILE = 1024
A_SUB_ROWS = 256
A_STAGE_ROWS, A_STAGE_COLS = 1024, 512
B_PROJ_ROWS = 256
SAMPLE_BATCH_TILE = 32

BF16 = jnp.bfloat16
F32 = jnp.float32
SQRT_2_OVER_PI = np.float32(np.sqrt(2.0 / np.pi))
GELU_CUBIC = np.float32(np.sqrt(2.0 / np.pi) * 0.044715)


def _dot(a, b):
    return jnp.dot(a, b.astype(BF16), preferred_element_type=F32)


def _cast_weights(dst_ref, src_ref):
    if len(src_ref.shape) == 3:
        for blk in range(src_ref.shape[0]):
            dst_ref[blk] = src_ref[blk].astype(BF16)
    else:
        for c0 in range(0, src_ref.shape[1], 2 * V7X_LANES):
            dst_ref[:, c0:c0 + 2 * V7X_LANES] = src_ref[:, c0:c0 + 2 * V7X_LANES].astype(BF16)


def _rms(x, g):
    return x * lax.rsqrt(jnp.mean(x * x, axis=-1, keepdims=True) + NORM_EPS) * g


def _gelu(x):
    inner = x * (SQRT_2_OVER_PI + GELU_CUBIC * (x * x))
    return x * (0.5 + 0.5 * jnp.tanh(inner))


def _sigmoid(x):
    return 1.0 / (1.0 + jnp.exp(-x))


def _silu(x):
    return x * _sigmoid(x)


def _softplus(x):
    return jnp.maximum(x, 0.0) + jnp.log1p(jnp.exp(-jnp.abs(x)))


def _const_spec(shape):
    zeros = (0,) * len(shape)
    return pl.BlockSpec(shape, lambda i: zeros, pipeline_mode=pl.Buffered(1))


def _params():
    return pltpu.CompilerParams(dimension_semantics=("arbitrary",),
                                vmem_limit_bytes=VMEM_LIMIT_BYTES)


def _layer_norm_a(v, lng_ref, lnb_ref):
    d = v - jnp.mean(v, axis=-1, keepdims=True)
    var = jnp.mean(d * d, axis=-1, keepdims=True)
    return d * lax.rsqrt(var + LN_EPS) * lng_ref[...] + lnb_ref[...]


def _mixer_a_sample_kernel(x_ref, npre_ref, npost_ref, win_ref, lng_ref, lnb_ref, ws_ref, bst_ref,
                           wout_ref, o_ref, v_ref):
    x = x_ref[...]
    h = _rms(x, npre_ref[...]).astype(BF16)
    vn = _layer_norm_a(_gelu(_dot(h, win_ref[:, D_A:2 * D_A])), lng_ref, lnb_ref)
    v_ref[...] = vn
    acc = jnp.zeros(x.shape, F32)
    for g in range(A_GROUPS):
        c0 = g * A_GROUP_W
        u = _gelu(_dot(h, win_ref[:, c0:c0 + A_GROUP_W]))
        gate = _silu(_dot(h, win_ref[:, 2 * D_A + c0:2 * D_A + c0 + A_GROUP_W]))
        mixed = ws_ref[g][0:1, 0:1] * vn[:, c0:c0 + A_GROUP_W] + bst_ref[0:1, g:g + 1]
        acc = acc + _dot((u * mixed * gate).astype(BF16), wout_ref[c0:c0 + A_GROUP_W, :])
    o_ref[...] = x + _rms(acc, npost_ref[...])


def _mixer_a_prompt_kernel(x_ref, npre_ref, npost_ref, win_ref, lng_ref, lnb_ref, ws_ref, bst_ref,
                           wout_ref, o_ref, *, tm, sub):
    x = x_ref[...]
    h = _rms(x, npre_ref[...]).astype(BF16)
    row = lax.broadcasted_iota(jnp.int32, (CHUNK, CHUNK), 0)
    col = lax.broadcasted_iota(jnp.int32, (CHUNK, CHUNK), 1)
    causal = row >= col
    n_sub = tm // sub
    items = [(r, g) for r in range(n_sub) for g in range(A_GROUPS)]
    v_parts = {r: [] for r in range(n_sub)}
    vnb, acc, ws_masked = {}, {}, {}

    def sub_rows(r):
        return slice(r * sub, (r + 1) * sub)

    def v_part(r, c):
        c0 = D_A + c * A_GROUP_W
        v_parts[r].append(_gelu(_dot(h[sub_rows(r)], win_ref[:, c0:c0 + A_GROUP_W])))

    def v_finish(r):
        v = jnp.concatenate(v_parts.pop(r), axis=1)
        vnb[r] = _layer_norm_a(v, lng_ref, lnb_ref).astype(BF16)

    def front(r, g):
        c0 = g * A_GROUP_W
        hr = h[sub_rows(r)]
        u = _dot(hr, win_ref[:, c0:c0 + A_GROUP_W])
        gate = _dot(hr, win_ref[:, 2 * D_A + c0:2 * D_A + c0 + A_GROUP_W])
        if g not in ws_masked:
            ws_masked[g] = jnp.where(causal, ws_ref[g], 0.0).astype(BF16)
        mixed = jnp.concatenate(
            [_dot(ws_masked[g], vnb[r][c * CHUNK:(c + 1) * CHUNK, c0:c0 + A_GROUP_W])
             for c in range(sub // CHUNK)], axis=0)
        return u, gate, mixed

    def back(r, g, u, gate, mixed):
        c0 = g * A_GROUP_W
        bias = jnp.concatenate([bst_ref[:, g:g + 1]] * (sub // CHUNK), axis=0)
        z = (_gelu(u) * (mixed + bias) * _silu(gate)).astype(BF16)
        y = _dot(z, wout_ref[c0:c0 + A_GROUP_W, :])
        acc[r] = y if g == 0 else acc[r] + y
        if g == A_GROUPS - 1:
            rows = sub_rows(r)
            o_ref[rows, :] = x[rows] + _rms(acc.pop(r), npost_ref[...])

    for c in range(A_GROUPS):
        v_part(0, c)
    v_finish(0)
    nxt = front(*items[0])
    for k, (r, g) in enumerate(items):
        cur = nxt
        if r + 1 < n_sub:
            v_part(r + 1, g)
            if g == A_GROUPS - 1:
                v_finish(r + 1)
        if k + 1 < len(items):
            nxt = front(*items[k + 1])
        back(r, g, *cur)


def _stage_weights(hbm_ref, layer, dst_ref, stage_ref, sem_ref):
    rows, cols = dst_ref.shape
    _, piece_rows, piece_cols = stage_ref.shape
    pieces = [(r0, c0) for r0 in range(0, rows, piece_rows) for c0 in range(0, cols, piece_cols)]

    def copy(k):
        r0, c0 = pieces[k]
        src = hbm_ref.at[layer, pl.ds(r0, piece_rows), pl.ds(c0, piece_cols)]
        return pltpu.make_async_copy(src, stage_ref.at[k % 2], sem_ref.at[k % 2])

    copy(0).start()
    for k, (r0, c0) in enumerate(pieces):
        if k + 1 < len(pieces):
            copy(k + 1).start()
        copy(k).wait()
        dst_ref[r0:r0 + piece_rows, c0:c0 + piece_cols] = stage_ref[k % 2].astype(BF16)


def _mixer_a_kernel(x_ref, xs_ref, npre_ref, npost_ref, win_hbm, lng_ref, lnb_ref, ws_ref, bst_ref,
                    wout_hbm, o_ref, os_ref, vs_ref, win_ref, wout_ref, stage_ref, sem_ref,
                    *, layer, tm, sub, n_tiles):
    step = pl.program_id(0)

    @pl.when(step == 0)
    def _():
        _stage_weights(win_hbm, layer, win_ref, stage_ref, sem_ref)
        _stage_weights(wout_hbm, layer, wout_ref, stage_ref, sem_ref)

    @pl.when(step < n_tiles)
    def _():
        _mixer_a_prompt_kernel(x_ref, npre_ref, npost_ref, win_ref, lng_ref, lnb_ref, ws_ref,
                               bst_ref, wout_ref, o_ref, tm=tm, sub=sub)

    @pl.when(step == n_tiles)
    def _():
        _mixer_a_sample_kernel(xs_ref, npre_ref, npost_ref, win_ref, lng_ref, lnb_ref, ws_ref,
                               bst_ref, wout_ref, os_ref, vs_ref)


def _mixer_a(x, xs, npre, npost, win, lng, lnb, ws, bst, wout, *, layer, tm):
    n, ns = x.shape[0], xs.shape[0]
    n_tiles = n // tm
    row_spec = pl.BlockSpec((tm, D_MODEL), lambda i: (jnp.minimum(i, n_tiles - 1), 0))
    hbm_spec = pl.BlockSpec(memory_space=pl.ANY)
    return pl.pallas_call(
        functools.partial(_mixer_a_kernel, layer=layer, tm=tm, sub=A_SUB_ROWS, n_tiles=n_tiles),
        grid=(n_tiles + 1,),
        in_specs=[row_spec, _const_spec((ns, D_MODEL)), _const_spec((1, D_MODEL)),
                  _const_spec((1, D_MODEL)), hbm_spec, _const_spec((1, D_A)),
                  _const_spec((1, D_A)), _const_spec((A_GROUPS, CHUNK, CHUNK)),
                  _const_spec((CHUNK, A_GROUPS)), hbm_spec],
        out_specs=[row_spec, pl.BlockSpec((ns, D_MODEL), lambda i: (0, 0)),
                   pl.BlockSpec((ns, D_A), lambda i: (0, 0))],
        out_shape=[jax.ShapeDtypeStruct((n, D_MODEL), F32), jax.ShapeDtypeStruct((ns, D_MODEL), F32),
                   jax.ShapeDtypeStruct((ns, D_A), F32)],
        scratch_shapes=[pltpu.VMEM((D_MODEL, 3 * D_A), BF16), pltpu.VMEM((D_A, D_MODEL), BF16),
                        pltpu.VMEM((2, A_STAGE_ROWS, A_STAGE_COLS), F32),
                        pltpu.SemaphoreType.DMA((2,))],
        compiler_params=_params(),
        name="mixer_a",
    )(x, xs, npre, npost, win, lng, lnb, ws, bst, wout)


def _rope_tables(positions):
    half = HEAD_DIM // 2
    inv_freq = ROPE_THETA ** (-jnp.arange(half, dtype=F32) / half)
    ang = positions.astype(F32)[:, None] * inv_freq[None, :]
    cos, sin = jnp.cos(ang), jnp.sin(ang)
    reps = V7X_LANES // HEAD_DIM
    cos_t = jnp.tile(jnp.concatenate([cos, cos], axis=-1), (1, reps))
    sin_t = jnp.tile(jnp.concatenate([-sin, sin], axis=-1), (1, reps))
    return cos_t, sin_t


def _rope(x, cos, sin_signed):
    lane = lax.broadcasted_iota(jnp.int32, x.shape, 1)
    first_half = (lane % HEAD_DIM) < (HEAD_DIM // 2)
    rot = jnp.where(first_half, pltpu.roll(x, V7X_LANES - HEAD_DIM // 2, 1),
                    pltpu.roll(x, HEAD_DIM // 2, 1))
    return x * cos + rot * sin_signed


def _dup_kv_halves(x, lane):
    swapped = pltpu.roll(x, HEAD_DIM, 1)
    low = lane < HEAD_DIM
    return jnp.where(low, x, swapped), jnp.where(low, swapped, x)


def _mixer_b_prompt_kernel(x_ref, npre_ref, npost_ref, win_ref, sinks_ref, cos_ref, sin_ref,
                           wout_ref, o_ref, kout_ref, vout_ref,
                           k0_ref, k1_ref, vt0_ref, vt1_ref, att_ref, win_bf_ref, wout_bf_ref,
                           *, tm, tiles_per_seq):
    @pl.when(pl.program_id(0) == 0)
    def _():
        _cast_weights(win_bf_ref, win_ref)
        _cast_weights(wout_bf_ref, wout_ref)

    win_ref, wout_ref = win_bf_ref, wout_bf_ref
    first = (pl.program_id(0) % tiles_per_seq) == 0

    @pl.when(first)
    def _():
        for ref in (k0_ref, k1_ref):
            ref[0:WINDOW, :] = jnp.zeros((WINDOW, V7X_LANES), BF16)
        for ref in (vt0_ref, vt1_ref):
            ref[:, 0:WINDOW] = jnp.zeros((V7X_LANES, WINDOW), BF16)

    x = x_ref[...]
    h = _rms(x, npre_ref[...]).astype(BF16)
    lane = lax.broadcasted_iota(jnp.int32, (tm, V7X_LANES), 1)
    cos, sin = cos_ref[...], sin_ref[...]

    k = _rope(_dot(h, win_ref[:, QD:QD + KD]), cos, sin)
    v = _dot(h, win_ref[:, QD + KD:QD + 2 * KD])
    kout_ref[...] = k[tm - WINDOW:, :]
    vout_ref[...] = v[tm - WINDOW:, :]
    kd0, kd1 = _dup_kv_halves(k, lane)
    k0_ref[WINDOW:, :] = kd0.astype(BF16)
    k1_ref[WINDOW:, :] = kd1.astype(BF16)
    vt = v.T
    vt0_ref[:, WINDOW:] = jnp.concatenate([vt[:HEAD_DIM], vt[:HEAD_DIM]], axis=0).astype(BF16)
    vt1_ref[:, WINDOW:] = jnp.concatenate([vt[HEAD_DIM:], vt[HEAD_DIM:]], axis=0).astype(BF16)

    ci = lax.broadcasted_iota(jnp.int32, (2 * WINDOW, 2 * WINDOW), 0)
    qi = lax.broadcasted_iota(jnp.int32, (2 * WINDOW, 2 * WINDOW), 1) % WINDOW
    band = (ci >= qi) & (ci <= qi + WINDOW)
    bias = jnp.where(band, 0.0, -jnp.inf)
    bias_first = jnp.where(band & (ci >= jnp.where(first, WINDOW, 0)), 0.0, -jnp.inf)
    head_lane = lax.broadcasted_iota(jnp.int32, (1, 2 * WINDOW), 1) < WINDOW
    low_lanes = lax.broadcasted_iota(jnp.int32, (WINDOW, V7X_LANES), 1) < HEAD_DIM
    top_rows = lax.broadcasted_iota(jnp.int32, (V7X_LANES, WINDOW), 0) < HEAD_DIM
    scale = HEAD_DIM ** -0.5

    n_blocks = tm // WINDOW
    n_pairs = N_HEADS // 2
    chunk = 2 * V7X_LANES
    n_chunks = QD // chunk
    bpg = B_PROJ_ROWS // WINDOW
    n_groups = tm // B_PROJ_ROWS
    items = [(j, p) for j in range(n_blocks) for p in range(n_pairs)]
    q_chunks, gate_chunks, gated, y_chunks = {}, {}, {}, {}

    def block_rows(j):
        return slice(j * WINDOW, (j + 1) * WINDOW)

    def group_rows(g):
        return slice(g * B_PROJ_ROWS, (g + 1) * B_PROJ_ROWS)

    def project_q(g, c):
        q_chunks[g, c] = _dot(h[group_rows(g)], win_ref[:, c * chunk:(c + 1) * chunk])

    def project_gate(g, c):
        c0 = QD + 2 * KD + c * chunk
        gate_chunks[g, c] = _silu(_dot(h[group_rows(g)], win_ref[:, c0:c0 + chunk]))

    def project_out(g, c):
        if c == 0:
            gated[g] = jnp.concatenate(
                [att_ref[group_rows(g), cc * chunk:(cc + 1) * chunk] * gate_chunks.pop((g, cc))
                 for cc in range(n_chunks)], axis=1).astype(BF16)
        y_chunks[g, c] = _dot(gated[g], wout_ref[:, c * chunk:(c + 1) * chunk])
        if c == n_chunks - 1:
            y = jnp.concatenate([y_chunks.pop((g, cc)) for cc in range(n_chunks)], axis=1)
            rows = group_rows(g)
            o_ref[rows, :] = x[rows] + _rms(y, npost_ref[...])

    def scores(j, p):
        rows = block_rows(j)
        kref = k0_ref if (2 * p) // Q_PER_KV == 0 else k1_ref
        half = (p % 2) * V7X_LANES
        r0 = (j % bpg) * WINDOW
        qp = q_chunks[j // bpg, p // 2][r0:r0 + WINDOW, half:half + V7X_LANES]
        qp = _rope(qp, cos[rows], sin[rows]) * scale
        qs = jnp.concatenate([jnp.where(low_lanes, qp, 0.0), jnp.where(low_lanes, 0.0, qp)],
                             axis=0).astype(BF16)
        st = lax.dot_general(kref[j * WINDOW:(j + 2) * WINDOW, :], qs, (((1,), (1,)), ((), ())),
                             preferred_element_type=F32)
        return st + (bias_first if j == 0 else bias)

    def attend(j, p, st):
        vtref = vt0_ref if (2 * p) // Q_PER_KV == 0 else vt1_ref
        sink = jnp.where(head_lane, sinks_ref[0, 2 * p], sinks_ref[0, 2 * p + 1])
        m = jnp.maximum(jnp.max(st, axis=0, keepdims=True), sink)
        e = jnp.exp(st - m)
        inv = 1.0 / (jnp.sum(e, axis=0, keepdims=True) + jnp.exp(sink - m))
        return _dot(vtref[:, j * WINDOW:(j + 2) * WINDOW], e.astype(BF16)), inv

    def finish(j, p, ot, inv):
        ot = ot * inv
        pair = jnp.where(top_rows, ot[:, :WINDOW], ot[:, WINDOW:])
        att_ref[block_rows(j), p * V7X_LANES:(p + 1) * V7X_LANES] = pair.T

    items_per_group = bpg * n_pairs
    side = {}
    for g in range(n_groups):
        tasks = []
        for c in range(n_chunks):
            if g + 1 < n_groups:
                tasks.append((project_q, g + 1, c))
            tasks.append((project_gate, g, c))
            if g > 0:
                tasks.append((project_out, g - 1, c))
        for t, task in enumerate(tasks):
            at = g * items_per_group + (t * items_per_group) // len(tasks)
            side.setdefault(at, []).append(task)

    for c in range(n_chunks):
        project_q(0, c)
    st_next = scores(*items[0])
    pending = None
    for i, (j, p) in enumerate(items):
        st = st_next
        for fn, g, c in side.get(i, []):
            if fn is not project_out:
                fn(g, c)
        if i + 1 < len(items):
            st_next = scores(*items[i + 1])
        ot, inv = attend(j, p, st)
        if pending is not None:
            finish(*pending)
        pending = (j, p, ot, inv)
        for fn, g, c in side.get(i, []):
            if fn is project_out:
                fn(g, c)
    finish(*pending)
    for c in range(n_chunks):
        project_out(n_groups - 1, c)

    for ref in (k0_ref, k1_ref):
        ref[0:WINDOW, :] = ref[tm:tm + WINDOW, :]
    for ref in (vt0_ref, vt1_ref):
        ref[:, 0:WINDOW] = ref[:, tm:tm + WINDOW]


def _mixer_b_prompt(x, npre, npost, win, sinks, cos, sin, wout, *, tm, seq):
    n = x.shape[0]
    tiles_per_seq = seq // tm
    n_seq = n // seq
    row_spec = pl.BlockSpec((tm, D_MODEL), lambda i: (i, 0))
    rope_spec = pl.BlockSpec((tm, V7X_LANES), lambda i: (i % tiles_per_seq, 0))
    kv_spec = pl.BlockSpec((WINDOW, KD), lambda i: (i // tiles_per_seq, 0))
    return pl.pallas_call(
        functools.partial(_mixer_b_prompt_kernel, tm=tm, tiles_per_seq=tiles_per_seq),
        grid=(n // tm,),
        in_specs=[row_spec, _const_spec((1, D_MODEL)), _const_spec((1, D_MODEL)),
                  _const_spec((D_MODEL, 2 * QD + 2 * KD)),
                  pl.BlockSpec(memory_space=pltpu.SMEM),
                  rope_spec, rope_spec, _const_spec((QD, D_MODEL))],
        out_specs=[row_spec, kv_spec, kv_spec],
        out_shape=[jax.ShapeDtypeStruct((n, D_MODEL), F32),
                   jax.ShapeDtypeStruct((n_seq * WINDOW, KD), F32),
                   jax.ShapeDtypeStruct((n_seq * WINDOW, KD), F32)],
        scratch_shapes=[pltpu.VMEM((tm + WINDOW, V7X_LANES), BF16)] * 2
        + [pltpu.VMEM((V7X_LANES, tm + WINDOW), BF16)] * 2 + [pltpu.VMEM((tm, QD), F32)]
        + [pltpu.VMEM(win.shape, BF16), pltpu.VMEM(wout.shape, BF16)],
        compiler_params=_params(),
        name="mixer_b_prompt",
    )(x, npre, npost, win, sinks, cos, sin, wout)


def _mixer_b_sample_proj_kernel(x_ref, npre_ref, win_ref, cos_ref, sin_ref,
                                q_ref, k_ref, v_ref, gate_ref):
    n = x_ref.shape[0]
    h = _rms(x_ref[...], npre_ref[...]).astype(BF16)
    cos, sin = cos_ref[...], sin_ref[...]
    scale = HEAD_DIM ** -0.5
    low_lanes = lax.broadcasted_iota(jnp.int32, (n, V7X_LANES), 1) < HEAD_DIM
    for p in range(N_HEADS // 2):
        qp = _rope(_dot(h, win_ref[:, p * V7X_LANES:(p + 1) * V7X_LANES]), cos, sin) * scale
        swapped = pltpu.roll(qp, HEAD_DIM, 1)
        if (2 * p) // Q_PER_KV == 0:
            even, odd = jnp.where(low_lanes, qp, 0.0), jnp.where(low_lanes, swapped, 0.0)
        else:
            even, odd = jnp.where(low_lanes, 0.0, swapped), jnp.where(low_lanes, 0.0, qp)
        q_ref[pl.ds(2 * p, n, stride=N_HEADS), :] = even
        q_ref[pl.ds(2 * p + 1, n, stride=N_HEADS), :] = odd
    k_ref[...] = _rope(_dot(h, win_ref[:, QD:QD + KD]), cos, sin)
    v_ref[...] = _dot(h, win_ref[:, QD + KD:QD + 2 * KD])
    gate_ref[...] = _silu(_dot(h, win_ref[:, QD + 2 * KD:]))


def _mixer_b_sample_proj(x, npre, win, cos_row, sin_row):
    n = x.shape[0]
    return pl.pallas_call(
        _mixer_b_sample_proj_kernel,
        out_shape=[jax.ShapeDtypeStruct((n * N_HEADS, V7X_LANES), F32),
                   jax.ShapeDtypeStruct((n, KD), F32),
                   jax.ShapeDtypeStruct((n, KD), F32), jax.ShapeDtypeStruct((n, QD), F32)],
        compiler_params=pltpu.CompilerParams(vmem_limit_bytes=VMEM_LIMIT_BYTES),
        name="mixer_b_sample_proj",
    )(x, npre, win, cos_row, sin_row)


def _mixer_b_sample_attn_kernel(q_ref, kc_ref, vc_ref, kn_ref, vn_ref, sinks_ref,
                                o_ref, ko_ref, vo_ref):
    q = q_ref[...]
    kc, vc = kc_ref[...], vc_ref[...]
    kn, vn = kn_ref[...], vn_ref[...]
    qb = q.astype(BF16)
    s_c = jnp.einsum("bhl,bsl->bhs", qb, kc.astype(BF16), preferred_element_type=F32)
    s_n = jnp.sum(qb.astype(F32) * kn.astype(BF16).astype(F32), axis=-1, keepdims=True)
    sink = sinks_ref[...][None]
    m = jnp.maximum(jnp.maximum(jnp.max(s_c, axis=-1, keepdims=True), s_n), sink)
    e_c = jnp.exp(s_c - m)
    e_n = jnp.exp(s_n - m)
    inv = 1.0 / (jnp.sum(e_c, axis=-1, keepdims=True) + e_n + jnp.exp(sink - m))
    o = jnp.einsum("bhs,bsl->bhl", (e_c * inv).astype(BF16), vc.astype(BF16),
                   preferred_element_type=F32)
    p_n = (e_n * inv).astype(BF16).astype(F32)
    o_ref[...] = o + p_n * vn.astype(BF16).astype(F32)
    w = kc.shape[1]
    ko_ref[:, 0:w - 1, :] = kc[:, 1:w, :]
    ko_ref[:, w - 1:w, :] = kn
    vo_ref[:, 0:w - 1, :] = vc[:, 1:w, :]
    vo_ref[:, w - 1:w, :] = vn


def _mixer_b_sample_attn(qm, kc, vc, kn, vn, sinks_col, *, bt):
    b, w, _ = kc.shape
    q_spec = pl.BlockSpec((bt, N_HEADS, V7X_LANES), lambda i: (i, 0, 0))
    c_spec = pl.BlockSpec((bt, w, KD), lambda i: (i, 0, 0))
    n_spec = pl.BlockSpec((bt, 1, KD), lambda i: (i, 0, 0))
    return pl.pallas_call(
        _mixer_b_sample_attn_kernel,
        grid=(b // bt,),
        in_specs=[q_spec, c_spec, c_spec, n_spec, n_spec,
                  pl.BlockSpec((N_HEADS, 1), lambda i: (0, 0))],
        out_specs=[q_spec, c_spec, c_spec],
        out_shape=[jax.ShapeDtypeStruct((b, N_HEADS, V7X_LANES), F32),
                   jax.ShapeDtypeStruct((b, w, KD), F32), jax.ShapeDtypeStruct((b, w, KD), F32)],
        compiler_params=_params(),
        name="mixer_b_sample_attn",
    )(qm, kc, vc, kn, vn, sinks_col)


def _mixer_b_sample_out_kernel(x_ref, att_ref, gate_ref, wout_ref, npost_ref, o_ref):
    n = x_ref.shape[0]
    low_lanes = lax.broadcasted_iota(jnp.int32, (n, V7X_LANES), 1) < HEAD_DIM
    pairs = []
    for p in range(N_HEADS // 2):
        even = att_ref[pl.ds(2 * p, n, stride=N_HEADS), :]
        odd = att_ref[pl.ds(2 * p + 1, n, stride=N_HEADS), :]
        if (2 * p) // Q_PER_KV == 0:
            odd = pltpu.roll(odd, HEAD_DIM, 1)
        else:
            even = pltpu.roll(even, HEAD_DIM, 1)
        pairs.append(jnp.where(low_lanes, even, odd))
    att = jnp.concatenate(pairs, axis=1)
    y = _dot((att * gate_ref[...]).astype(BF16), wout_ref[...])
    o_ref[...] = x_ref[...] + _rms(y, npost_ref[...])


def _mixer_b_sample_out(x, att, gate, wout, npost):
    return pl.pallas_call(
        _mixer_b_sample_out_kernel,
        out_shape=jax.ShapeDtypeStruct(x.shape, F32),
        compiler_params=pltpu.CompilerParams(vmem_limit_bytes=VMEM_LIMIT_BYTES),
        name="mixer_b_sample_out",
    )(x, att, gate, wout, npost)


def _rglru_gates(xc, wa_ref, ba_ref, wx_ref, bx_ref, lam_ref):
    xcb = xc.astype(BF16)
    ra, rx = [], []
    for blk in range(C_BLOCKS):
        cols = slice(blk * C_BLOCK_W, (blk + 1) * C_BLOCK_W)
        ra.append(_dot(xcb[:, cols], wa_ref[blk]))
        rx.append(_dot(xcb[:, cols], wx_ref[blk]))
    r = _sigmoid(jnp.concatenate(ra, axis=-1) + ba_ref[...])
    i_gate = _sigmoid(jnp.concatenate(rx, axis=-1) + bx_ref[...])
    log_a = (-LRU_C * r) * _softplus(-lam_ref[...])
    a = jnp.exp(log_a)
    mult = jnp.sqrt(-jnp.tanh(log_a) * (a * a + 1.0))
    return a, mult, i_gate * xc


def _mixer_c_prompt_kernel(x_ref, npre_ref, npost_ref, win_ref, cw_ref, cb_ref, wa_ref, ba_ref,
                           wx_ref, bx_ref, lam_ref, wout_ref, o_ref, conv_ref, hlast_ref,
                           xpad_ref, h_ref, win_bf_ref, wa_bf_ref, wx_bf_ref, wout_bf_ref,
                           *, tm, tiles_per_seq):
    @pl.when(pl.program_id(0) == 0)
    def _():
        for dst, src in ((win_bf_ref, win_ref), (wa_bf_ref, wa_ref), (wx_bf_ref, wx_ref),
                         (wout_bf_ref, wout_ref)):
            _cast_weights(dst, src)

    win_ref, wa_ref, wx_ref, wout_ref = win_bf_ref, wa_bf_ref, wx_bf_ref, wout_bf_ref
    first = (pl.program_id(0) % tiles_per_seq) == 0

    @pl.when(first)
    def _():
        xpad_ref[0:V7X_SUBLANES, :] = jnp.zeros((V7X_SUBLANES, D_RNN), F32)
        h_ref[...] = jnp.zeros((V7X_SUBLANES, D_RNN), F32)

    x = x_ref[...]
    h = _rms(x, npre_ref[...]).astype(BF16)
    xr = _dot(h, win_ref[:, :D_RNN])
    xpad_ref[V7X_SUBLANES:, :] = xr
    conv_ref[...] = xr[tm - V7X_SUBLANES:, :]
    cw = cw_ref[...]
    xc = cb_ref[...] + xr * cw[CONV_W - 1:CONV_W, :]
    for tap in range(CONV_W - 1):
        back = CONV_W - 1 - tap
        xc = xc + xpad_ref[V7X_SUBLANES - back:V7X_SUBLANES - back + tm, :] * cw[tap:tap + 1, :]
    xpad_ref[0:V7X_SUBLANES, :] = xr[tm - V7X_SUBLANES:, :]

    a, mult, gx = _rglru_gates(xc, wa_ref, ba_ref, wx_ref, bx_ref, lam_ref)
    row = lax.broadcasted_iota(jnp.int32, (tm, 1), 0)
    mult = jnp.where(first & (row == 0), 1.0, mult)
    b = mult * gx

    n_groups = tm // V7X_SUBLANES
    a = a.reshape(n_groups, V7X_SUBLANES, D_RNN)
    b = b.reshape(n_groups, V7X_SUBLANES, D_RNN)
    sub = lax.broadcasted_iota(jnp.int32, (n_groups, V7X_SUBLANES, D_RNN), 1)
    for dist in (1, 2, 4):
        keep = sub >= dist
        a_prev = pltpu.roll(a, dist, 1)
        b_prev = pltpu.roll(b, dist, 1)
        b = jnp.where(keep, a * b_prev + b, b)
        a = jnp.where(keep, a * a_prev, a)
    carry = h_ref[V7X_SUBLANES - 1:V7X_SUBLANES, :]
    groups = []
    for gi in range(n_groups):
        hs_g = a[gi] * carry + b[gi]
        carry = hs_g[V7X_SUBLANES - 1:V7X_SUBLANES, :]
        groups.append(hs_g)
    h_ref[...] = groups[-1]
    hlast_ref[...] = groups[-1]
    hs = jnp.concatenate(groups, axis=0)

    gate = _silu(_dot(h, win_ref[:, D_RNN:]))
    y = _dot((hs * gate).astype(BF16), wout_ref[...])
    o_ref[...] = x + _rms(y, npost_ref[...])


def _mixer_c_prompt(x, npre, npost, win, cw, cb, wa, ba, wx, bx, lam, wout, *, tm, seq):
    n = x.shape[0]
    tiles_per_seq = seq // tm
    n_seq = n // seq
    row_spec = pl.BlockSpec((tm, D_MODEL), lambda i: (i, 0))
    tail_spec = pl.BlockSpec((V7X_SUBLANES, D_RNN), lambda i: (i // tiles_per_seq, 0))
    vec = _const_spec((1, D_RNN))
    blk = _const_spec((C_BLOCKS, C_BLOCK_W, C_BLOCK_W))
    return pl.pallas_call(
        functools.partial(_mixer_c_prompt_kernel, tm=tm, tiles_per_seq=tiles_per_seq),
        grid=(n // tm,),
        in_specs=[row_spec, vec, vec, _const_spec((D_MODEL, 2 * D_RNN)),
                  _const_spec((CONV_W, D_RNN)), vec, blk, vec, blk, vec, vec,
                  _const_spec((D_RNN, D_MODEL))],
        out_specs=[row_spec, tail_spec, tail_spec],
        out_shape=[jax.ShapeDtypeStruct((n, D_MODEL), F32),
                   jax.ShapeDtypeStruct((n_seq * V7X_SUBLANES, D_RNN), F32),
                   jax.ShapeDtypeStruct((n_seq * V7X_SUBLANES, D_RNN), F32)],
        scratch_shapes=[pltpu.VMEM((tm + V7X_SUBLANES, D_RNN), F32),
                        pltpu.VMEM((V7X_SUBLANES, D_RNN), F32)]
        + [pltpu.VMEM(w.shape, BF16) for w in (win, wa, wx, wout)],
        compiler_params=_params(),
        name="mixer_c_prompt",
    )(x, npre, npost, win, cw, cb, wa, ba, wx, bx, lam, wout)


def _mixer_c_sample_kernel(x_ref, npre_ref, npost_ref, win_ref, cw_ref, cb_ref, wa_ref, ba_ref,
                           wx_ref, bx_ref, lam_ref, wout_ref, conv_ref, h0_ref,
                           o_ref, conv_out_ref, h_out_ref):
    x = x_ref[...]
    h = _rms(x, npre_ref[...]).astype(BF16)
    xr = _dot(h, win_ref[:, :D_RNN])
    cw = cw_ref[...]
    xc = cb_ref[...] + xr * cw[CONV_W - 1:CONV_W, :]
    for tap in range(CONV_W - 1):
        xc = xc + conv_ref[tap] * cw[tap:tap + 1, :]
        if tap > 0:
            conv_out_ref[tap - 1] = conv_ref[tap]
    conv_out_ref[CONV_W - 2] = xr
    a, mult, gx = _rglru_gates(xc, wa_ref, ba_ref, wx_ref, bx_ref, lam_ref)
    hs = a * h0_ref[...] + mult * gx
    h_out_ref[...] = hs
    gate = _silu(_dot(h, win_ref[:, D_RNN:]))
    y = _dot((hs * gate).astype(BF16), wout_ref[...])
    o_ref[...] = x + _rms(y, npost_ref[...])


def _mixer_c_sample(x, npre, npost, win, cw, cb, wa, ba, wx, bx, lam, wout, conv_t, h0):
    n = x.shape[0]
    return pl.pallas_call(
        _mixer_c_sample_kernel,
        out_shape=[jax.ShapeDtypeStruct((n, D_MODEL), F32),
                   jax.ShapeDtypeStruct((CONV_W - 1, n, D_RNN), F32),
                   jax.ShapeDtypeStruct((n, D_RNN), F32)],
        compiler_params=pltpu.CompilerParams(vmem_limit_bytes=VMEM_LIMIT_BYTES),
        name="mixer_c_sample",
    )(x, npre, npost, win, cw, cb, wa, ba, wx, bx, lam, wout, conv_t, h0)


def kernel(x_prompt, x_sample, cache_b_k, cache_b_v, state_c_conv, state_c_h, norm_pre, norm_post,
           a_w_in, a_ln_g, a_ln_b, a_w_s, a_b_s, a_w_out, b_w_in, b_sinks, b_w_out, c_w_in,
           c_conv_w, c_conv_b, c_w_a, c_b_a, c_w_x, c_b_x, c_lam, c_w_out):
    batch, seq, _ = x_prompt.shape
    dec_batch, dec_seq, _ = x_sample.shape
    past_len = PAST_LEN
    w_buf = cache_b_k.shape[2]
    assert dec_seq == 1 and w_buf == WINDOW and seq % ROW_TILE == 0
    assert ROW_TILE % B_PROJ_ROWS == 0 and B_PROJ_ROWS % WINDOW == 0
    assert ROW_TILE % A_SUB_ROWS == 0 and A_SUB_ROWS % CHUNK == 0

    xp = x_prompt.reshape(batch * seq, D_MODEL)
    xs = x_sample.reshape(dec_batch, D_MODEL)
    row = lambda a: a.reshape(1, -1)

    cos_p, sin_p = _rope_tables(jnp.arange(seq, dtype=jnp.int32))
    cos_s, sin_s = _rope_tables(past_len + jnp.arange(dec_seq, dtype=jnp.int32))

    a_v_s = []
    b_kp, b_vp, b_ks, b_vs = [], [], [], []
    c_cp, c_hp, c_cs, c_hs = [], [], [], []
    for i in range(DEPTH):
        kind, j = i % N_MIXERS, i // N_MIXERS
        npre, npost = row(norm_pre[i]), row(norm_post[i])
        if kind == 0:
            xp, xs, vs = _mixer_a(xp, xs, npre, npost, a_w_in, row(a_ln_g[j]), row(a_ln_b[j]),
                                  a_w_s[j], a_b_s[j].T, a_w_out, layer=j, tm=ROW_TILE)
            a_v_s.append(vs.reshape(dec_batch, dec_seq, D_A))
        elif kind == 1:
            win, wout = b_w_in[j], b_w_out[j]
            xp, kp, vp = _mixer_b_prompt(xp, npre, npost, win, row(b_sinks[j]), cos_p, sin_p, wout,
                                         tm=ROW_TILE, seq=seq)
            b_kp.append(kp.reshape(batch, WINDOW, N_KV_HEADS, HEAD_DIM))
            b_vp.append(vp.reshape(batch, WINDOW, N_KV_HEADS, HEAD_DIM))

            qm, kn, vn, gate = _mixer_b_sample_proj(xs, npre, win, cos_s, sin_s)
            qm = qm.reshape(dec_batch, N_HEADS, V7X_LANES)
            om, ks_new, vs_new = _mixer_b_sample_attn(
                qm, cache_b_k[j].reshape(dec_batch, w_buf, KD),
                cache_b_v[j].reshape(dec_batch, w_buf, KD),
                kn.reshape(dec_batch, 1, KD), vn.reshape(dec_batch, 1, KD),
                b_sinks[j].reshape(N_HEADS, 1), bt=SAMPLE_BATCH_TILE)
            xs = _mixer_b_sample_out(xs, om.reshape(dec_batch * N_HEADS, V7X_LANES), gate, wout,
                                     npost)
            b_ks.append(ks_new.reshape(dec_batch, WINDOW, N_KV_HEADS, HEAD_DIM))
            b_vs.append(vs_new.reshape(dec_batch, WINDOW, N_KV_HEADS, HEAD_DIM))
        else:
            args = (npre, npost, c_w_in[j], c_conv_w[j], row(c_conv_b[j]),
                    c_w_a[j], row(c_b_a[j]), c_w_x[j], row(c_b_x[j]),
                    row(c_lam[j]), c_w_out[j])
            xp, conv_tail, h_tail = _mixer_c_prompt(xp, *args, tm=ROW_TILE, seq=seq)
            c_cp.append(conv_tail.reshape(batch, V7X_SUBLANES, D_RNN)[:, V7X_SUBLANES - (CONV_W - 1):])
            c_hp.append(h_tail.reshape(batch, V7X_SUBLANES, D_RNN)[:, V7X_SUBLANES - 1])
            xs, conv_new, h_new = _mixer_c_sample(
                xs, *args, jnp.transpose(state_c_conv[j], (1, 0, 2)), state_c_h[j])
            c_cs.append(jnp.transpose(conv_new, (1, 0, 2)))
            c_hs.append(h_new)

    return (xp.reshape(batch, seq, D_MODEL), xs.reshape(dec_batch, dec_seq, D_MODEL),
            jnp.stack(a_v_s), jnp.stack(b_kp), jnp.stack(b_vp), jnp.stack(b_ks), jnp.stack(b_vs),
            jnp.stack(c_cp), jnp.stack(c_hp), jnp.stack(c_cs), jnp.stack(c_hs))
```

```python
import functools

import jax
import jax.numpy as jnp
import numpy as np
from jax import lax
from jax.experimental import pallas as pl
from jax.experimental.pallas import tpu as pltpu

D_MODEL = 1024
DEPTH = 4
N_MIXERS = 3
NORM_EPS = 1e-6
LN_EPS = 1e-5

D_A = 2 * D_MODEL
CHUNK = 128
A_GROUPS = 4
A_GROUP_W = D_A // A_GROUPS

HEAD_DIM = 64
N_HEADS = D_MODEL // HEAD_DIM
N_KV_HEADS = N_HEADS // 8
Q_PER_KV = N_HEADS // N_KV_HEADS
WINDOW = 128
ROPE_THETA = 10000.0
QD = N_HEADS * HEAD_DIM
KD = N_KV_HEADS * HEAD_DIM

D_RNN = D_MODEL
C_BLOCKS = 4
C_BLOCK_W = D_RNN // C_BLOCKS
CONV_W = 4
LRU_C = 8.0

PAST_LEN = 8192

V7X_LANES = 128
V7X_SUBLANES = 8
V7X_VMEM_BYTES = 64 * 1024 * 1024
VMEM_LIMIT_BYTES = V7X_VMEM_BYTES - 8 * 1024 * 1024

ROW_TILE = 1024
A_SUB_ROWS = 512
A_STAGE_SLOTS, A_STAGE_ROWS, A_STAGE_COLS = 4, 1024, 256
B_PROJ_ROWS = 256
SAMPLE_BATCH_TILE = 64

BF16 = jnp.bfloat16
F32 = jnp.float32
SQRT_2_OVER_PI = np.float32(np.sqrt(2.0 / np.pi))
GELU_CUBIC = np.float32(np.sqrt(2.0 / np.pi) * 0.044715)


def _dot(a, b):
    return jnp.dot(a, b.astype(BF16), preferred_element_type=F32)


def _cast_weights(dst_ref, src_ref):
    if len(src_ref.shape) == 3:
        for blk in range(src_ref.shape[0]):
            dst_ref[blk] = src_ref[blk].astype(BF16)
    else:
        for c0 in range(0, src_ref.shape[1], 2 * V7X_LANES):
            dst_ref[:, c0:c0 + 2 * V7X_LANES] = src_ref[:, c0:c0 + 2 * V7X_LANES].astype(BF16)


def _rms(x, g):
    return x * lax.rsqrt(jnp.mean(x * x, axis=-1, keepdims=True) + NORM_EPS) * g


def _gelu(x):
    inner = x * (SQRT_2_OVER_PI + GELU_CUBIC * (x * x))
    return x * (0.5 + 0.5 * jnp.tanh(inner))


def _sigmoid(x):
    return 1.0 / (1.0 + jnp.exp(-x))


def _silu(x):
    return x * _sigmoid(x)


def _softplus(x):
    return jnp.maximum(x, 0.0) + jnp.log1p(jnp.exp(-jnp.abs(x)))


def _const_spec(shape):
    zeros = (0,) * len(shape)
    return pl.BlockSpec(shape, lambda i: zeros, pipeline_mode=pl.Buffered(1))


def _params():
    return pltpu.CompilerParams(dimension_semantics=("arbitrary",),
                                vmem_limit_bytes=VMEM_LIMIT_BYTES)


def _layer_norm_a(v, lng_ref, lnb_ref):
    d = v - jnp.mean(v, axis=-1, keepdims=True)
    var = jnp.mean(d * d, axis=-1, keepdims=True)
    return d * lax.rsqrt(var + LN_EPS) * lng_ref[...] + lnb_ref[...]


def _mixer_a_sample_kernel(x_ref, npre_ref, npost_ref, win_ref, lng_ref, lnb_ref, ws_ref, bst_ref,
                           wout_ref, o_ref, v_ref):
    x = x_ref[...]
    h = _rms(x, npre_ref[...]).astype(BF16)
    vn = _layer_norm_a(_gelu(_dot(h, win_ref[:, D_A:2 * D_A])), lng_ref, lnb_ref)
    v_ref[...] = vn
    acc = jnp.zeros(x.shape, F32)
    for g in range(A_GROUPS):
        c0 = g * A_GROUP_W
        u = _gelu(_dot(h, win_ref[:, c0:c0 + A_GROUP_W]))
        gate = _silu(_dot(h, win_ref[:, 2 * D_A + c0:2 * D_A + c0 + A_GROUP_W]))
        mixed = ws_ref[g][0:1, 0:1] * vn[:, c0:c0 + A_GROUP_W] + bst_ref[0:1, g:g + 1]
        acc = acc + _dot((u * mixed * gate).astype(BF16), wout_ref[c0:c0 + A_GROUP_W, :])
    o_ref[...] = x + _rms(acc, npost_ref[...])


def _mixer_a_prompt_kernel(x_ref, npre_ref, npost_ref, win_ref, lng_ref, lnb_ref, ws_ref, bst_ref,
                           wout_ref, o_ref, *, tm, sub):
    x = x_ref[...]
    h = _rms(x, npre_ref[...]).astype(BF16)
    row = lax.broadcasted_iota(jnp.int32, (CHUNK, CHUNK), 0)
    col = lax.broadcasted_iota(jnp.int32, (CHUNK, CHUNK), 1)
    causal = row >= col
    n_sub = tm // sub
    items = [(r, g) for r in range(n_sub) for g in range(A_GROUPS)]
    v_parts = {r: [] for r in range(n_sub)}
    vnb, acc, ws_masked = {}, {}, {}

    def sub_rows(r):
        return slice(r * sub, (r + 1) * sub)

    def v_part(r, c):
        c0 = D_A + c * A_GROUP_W
        v_parts[r].append(_gelu(_dot(h[sub_rows(r)], win_ref[:, c0:c0 + A_GROUP_W])))

    def v_finish(r):
        v = jnp.concatenate(v_parts.pop(r), axis=1)
        vnb[r] = _layer_norm_a(v, lng_ref, lnb_ref).astype(BF16)

    def front(r, g):
        c0 = g * A_GROUP_W
        hr = h[sub_rows(r)]
        u = _dot(hr, win_ref[:, c0:c0 + A_GROUP_W])
        gate = _dot(hr, win_ref[:, 2 * D_A + c0:2 * D_A + c0 + A_GROUP_W])
        if g not in ws_masked:
            ws_masked[g] = jnp.where(causal, ws_ref[g], 0.0).astype(BF16)
        mixed = jnp.concatenate(
            [_dot(ws_masked[g], vnb[r][c * CHUNK:(c + 1) * CHUNK, c0:c0 + A_GROUP_W])
             for c in range(sub // CHUNK)], axis=0)
        return u, gate, mixed

    def back(r, g, u, gate, mixed):
        c0 = g * A_GROUP_W
        bias = jnp.concatenate([bst_ref[:, g:g + 1]] * (sub // CHUNK), axis=0)
        z = (_gelu(u) * (mixed + bias) * _silu(gate)).astype(BF16)
        y = _dot(z, wout_ref[c0:c0 + A_GROUP_W, :])
        acc[r] = y if g == 0 else acc[r] + y
        if g == A_GROUPS - 1:
            rows = sub_rows(r)
            o_ref[rows, :] = x[rows] + _rms(acc.pop(r), npost_ref[...])

    for c in range(A_GROUPS):
        v_part(0, c)
    v_finish(0)
    nxt = front(*items[0])
    for k, (r, g) in enumerate(items):
        cur = nxt
        if r + 1 < n_sub:
            v_part(r + 1, g)
            if g == A_GROUPS - 1:
                v_finish(r + 1)
        if k + 1 < len(items):
            nxt = front(*items[k + 1])
        back(r, g, *cur)


def _stage_weights(hbm_ref, layer, dst_ref, stage_ref, sem_ref):
    rows, cols = dst_ref.shape
    n_slots, piece_rows, piece_cols = stage_ref.shape
    ahead = n_slots - 1
    pieces = [(r0, c0) for r0 in range(0, rows, piece_rows) for c0 in range(0, cols, piece_cols)]

    def copy(k):
        r0, c0 = pieces[k]
        src = hbm_ref.at[layer, pl.ds(r0, piece_rows), pl.ds(c0, piece_cols)]
        return pltpu.make_async_copy(src, stage_ref.at[k % n_slots], sem_ref.at[k % n_slots])

    for k in range(min(ahead, len(pieces))):
        copy(k).start()
    for k, (r0, c0) in enumerate(pieces):
        if k + ahead < len(pieces):
            copy(k + ahead).start()
        copy(k).wait()
        dst_ref[r0:r0 + piece_rows, c0:c0 + piece_cols] = stage_ref[k % n_slots].astype(BF16)


def _mixer_a_kernel(x_ref, xs_ref, npre_ref, npost_ref, win_hbm, lng_ref, lnb_ref, ws_ref, bst_ref,
                    wout_hbm, o_ref, os_ref, vs_ref, win_ref, wout_ref, stage_ref, sem_ref,
                    *, layer, tm, sub, n_tiles):
    step = pl.program_id(0)

    @pl.when(step == 0)
    def _():
        _stage_weights(win_hbm, layer, win_ref, stage_ref, sem_ref)
        _stage_weights(wout_hbm, layer, wout_ref, stage_ref, sem_ref)

    @pl.when(step < n_tiles)
    def _():
        _mixer_a_prompt_kernel(x_ref, npre_ref, npost_ref, win_ref, lng_ref, lnb_ref, ws_ref,
                               bst_ref, wout_ref, o_ref, tm=tm, sub=sub)

    @pl.when(step == n_tiles)
    def _():
        _mixer_a_sample_kernel(xs_ref, npre_ref, npost_ref, win_ref, lng_ref, lnb_ref, ws_ref,
                               bst_ref, wout_ref, os_ref, vs_ref)


def _mixer_a(x, xs, npre, npost, win, lng, lnb, ws, bst, wout, *, layer, tm):
    n, ns = x.shape[0], xs.shape[0]
    n_tiles = n // tm
    row_spec = pl.BlockSpec((tm, D_MODEL), lambda i: (jnp.minimum(i, n_tiles - 1), 0))
    hbm_spec = pl.BlockSpec(memory_space=pl.ANY)
    return pl.pallas_call(
        functools.partial(_mixer_a_kernel, layer=layer, tm=tm, sub=A_SUB_ROWS, n_tiles=n_tiles),
        grid=(n_tiles + 1,),
        in_specs=[row_spec, _const_spec((ns, D_MODEL)), _const_spec((1, D_MODEL)),
                  _const_spec((1, D_MODEL)), hbm_spec, _const_spec((1, D_A)),
                  _const_spec((1, D_A)), _const_spec((A_GROUPS, CHUNK, CHUNK)),
                  _const_spec((CHUNK, A_GROUPS)), hbm_spec],
        out_specs=[row_spec, pl.BlockSpec((ns, D_MODEL), lambda i: (0, 0)),
                   pl.BlockSpec((ns, D_A), lambda i: (0, 0))],
        out_shape=[jax.ShapeDtypeStruct((n, D_MODEL), F32), jax.ShapeDtypeStruct((ns, D_MODEL), F32),
                   jax.ShapeDtypeStruct((ns, D_A), F32)],
        scratch_shapes=[pltpu.VMEM((D_MODEL, 3 * D_A), BF16), pltpu.VMEM((D_A, D_MODEL), BF16),
                        pltpu.VMEM((A_STAGE_SLOTS, A_STAGE_ROWS, A_STAGE_COLS), F32),
                        pltpu.SemaphoreType.DMA((A_STAGE_SLOTS,))],
        compiler_params=_params(),
        name="mixer_a",
    )(x, xs, npre, npost, win, lng, lnb, ws, bst, wout)


def _rope_tables(positions):
    half = HEAD_DIM // 2
    lane = jnp.arange(V7X_LANES, dtype=jnp.int32)
    inv_freq = ROPE_THETA ** (-(lane % half).astype(F32) / half)
    ang = positions.astype(F32)[:, None] * inv_freq[None, :]
    first_half = (lane % HEAD_DIM) < half
    return jnp.cos(ang), jnp.where(first_half[None, :], -jnp.sin(ang), jnp.sin(ang))


def _rope(x, cos, sin_signed):
    lane = lax.broadcasted_iota(jnp.int32, x.shape, 1)
    first_half = (lane % HEAD_DIM) < (HEAD_DIM // 2)
    rot = jnp.where(first_half, pltpu.roll(x, V7X_LANES - HEAD_DIM // 2, 1),
                    pltpu.roll(x, HEAD_DIM // 2, 1))
    return x * cos + rot * sin_signed


def _dup_kv_halves(x, lane):
    swapped = pltpu.roll(x, HEAD_DIM, 1)
    low = lane < HEAD_DIM
    return jnp.where(low, x, swapped), jnp.where(low, swapped, x)


def _mixer_b_prompt_kernel(x_ref, npre_ref, npost_ref, win_ref, sinks_ref, cos_ref, sin_ref,
                           wout_ref, o_ref, kout_ref, vout_ref,
                           k0_ref, k1_ref, vt0_ref, vt1_ref, att_ref, win_bf_ref, wout_bf_ref,
                           *, tm, tiles_per_seq):
    @pl.when(pl.program_id(0) == 0)
    def _():
        _cast_weights(win_bf_ref, win_ref)
        _cast_weights(wout_bf_ref, wout_ref)

    win_ref, wout_ref = win_bf_ref, wout_bf_ref
    first = (pl.program_id(0) % tiles_per_seq) == 0

    @pl.when(first)
    def _():
        for ref in (k0_ref, k1_ref):
            ref[0:WINDOW, :] = jnp.zeros((WINDOW, V7X_LANES), BF16)
        for ref in (vt0_ref, vt1_ref):
            ref[:, 0:WINDOW] = jnp.zeros((V7X_LANES, WINDOW), BF16)

    x = x_ref[...]
    h = _rms(x, npre_ref[...]).astype(BF16)
    lane = lax.broadcasted_iota(jnp.int32, (tm, V7X_LANES), 1)
    cos, sin = cos_ref[...], sin_ref[...]

    k = _rope(_dot(h, win_ref[:, QD:QD + KD]), cos, sin)
    v = _dot(h, win_ref[:, QD + KD:QD + 2 * KD])
    kout_ref[...] = k[tm - WINDOW:, :]
    vout_ref[...] = v[tm - WINDOW:, :]
    kd0, kd1 = _dup_kv_halves(k, lane)
    k0_ref[WINDOW:, :] = kd0.astype(BF16)
    k1_ref[WINDOW:, :] = kd1.astype(BF16)
    vt = v.T
    vt0_ref[:, WINDOW:] = jnp.concatenate([vt[:HEAD_DIM], vt[:HEAD_DIM]], axis=0).astype(BF16)
    vt1_ref[:, WINDOW:] = jnp.concatenate([vt[HEAD_DIM:], vt[HEAD_DIM:]], axis=0).astype(BF16)

    ci = lax.broadcasted_iota(jnp.int32, (2 * WINDOW, 2 * WINDOW), 0)
    qi = lax.broadcasted_iota(jnp.int32, (2 * WINDOW, 2 * WINDOW), 1) % WINDOW
    band = (ci >= qi) & (ci <= qi + WINDOW)
    bias = jnp.where(band, 0.0, -jnp.inf)
    bias_first = jnp.where(band & (ci >= jnp.where(first, WINDOW, 0)), 0.0, -jnp.inf)
    head_lane = lax.broadcasted_iota(jnp.int32, (1, 2 * WINDOW), 1) < WINDOW
    low_lanes = lax.broadcasted_iota(jnp.int32, (WINDOW, V7X_LANES), 1) < HEAD_DIM
    top_rows = lax.broadcasted_iota(jnp.int32, (V7X_LANES, WINDOW), 0) < HEAD_DIM
    scale = HEAD_DIM ** -0.5

    n_blocks = tm // WINDOW
    n_pairs = N_HEADS // 2
    chunk = 2 * V7X_LANES
    n_chunks = QD // chunk
    bpg = B_PROJ_ROWS // WINDOW
    n_groups = tm // B_PROJ_ROWS
    items = [(j, p) for j in range(n_blocks) for p in range(n_pairs)]
    q_chunks, gate_chunks, gated, y_chunks = {}, {}, {}, {}

    def block_rows(j):
        return slice(j * WINDOW, (j + 1) * WINDOW)

    def group_rows(g):
        return slice(g * B_PROJ_ROWS, (g + 1) * B_PROJ_ROWS)

    def project_q(g, c):
        q_chunks[g, c] = _dot(h[group_rows(g)], win_ref[:, c * chunk:(c + 1) * chunk])

    def project_gate(g, c):
        c0 = QD + 2 * KD + c * chunk
        gate_chunks[g, c] = _silu(_dot(h[group_rows(g)], win_ref[:, c0:c0 + chunk]))

    def project_out(g, c):
        if c == 0:
            gated[g] = jnp.concatenate(
                [att_ref[group_rows(g), cc * chunk:(cc + 1) * chunk] * gate_chunks.pop((g, cc))
                 for cc in range(n_chunks)], axis=1).astype(BF16)
        y_chunks[g, c] = _dot(gated[g], wout_ref[:, c * chunk:(c + 1) * chunk])
        if c == n_chunks - 1:
            y = jnp.concatenate([y_chunks.pop((g, cc)) for cc in range(n_chunks)], axis=1)
            rows = group_rows(g)
            o_ref[rows, :] = x[rows] + _rms(y, npost_ref[...])

    def scores(j, p):
        rows = block_rows(j)
        kref = k0_ref if (2 * p) // Q_PER_KV == 0 else k1_ref
        half = (p % 2) * V7X_LANES
        r0 = (j % bpg) * WINDOW
        qp = q_chunks[j // bpg, p // 2][r0:r0 + WINDOW, half:half + V7X_LANES]
        qp = _rope(qp, cos[rows], sin[rows]) * scale
        qs = jnp.concatenate([jnp.where(low_lanes, qp, 0.0), jnp.where(low_lanes, 0.0, qp)],
                             axis=0).astype(BF16)
        st = lax.dot_general(kref[j * WINDOW:(j + 2) * WINDOW, :], qs, (((1,), (1,)), ((), ())),
                             preferred_element_type=F32)
        return st + (bias_first if j == 0 else bias)

    def attend(j, p, st):
        vtref = vt0_ref if (2 * p) // Q_PER_KV == 0 else vt1_ref
        sink = jnp.where(head_lane, sinks_ref[0, 2 * p], sinks_ref[0, 2 * p + 1])
        m = jnp.maximum(jnp.max(st, axis=0, keepdims=True), sink)
        e = jnp.exp(st - m)
        inv = 1.0 / (jnp.sum(e, axis=0, keepdims=True) + jnp.exp(sink - m))
        return _dot(vtref[:, j * WINDOW:(j + 2) * WINDOW], e.astype(BF16)), inv

    def finish(j, p, ot, inv):
        ot = ot * inv
        pair = jnp.where(top_rows, ot[:, :WINDOW], ot[:, WINDOW:])
        att_ref[block_rows(j), p * V7X_LANES:(p + 1) * V7X_LANES] = pair.T

    items_per_group = bpg * n_pairs
    side = {}
    for g in range(n_groups):
        tasks = []
        for c in range(n_chunks):
            if g + 1 < n_groups:
                tasks.append((project_q, g + 1, c))
            tasks.append((project_gate, g, c))
            if g > 0:
                tasks.append((project_out, g - 1, c))
        for t, task in enumerate(tasks):
            at = g * items_per_group + (t * items_per_group) // len(tasks)
            side.setdefault(at, []).append(task)

    for c in range(n_chunks):
        project_q(0, c)
    st_next = scores(*items[0])
    pending = None
    for i, (j, p) in enumerate(items):
        st = st_next
        for fn, g, c in side.get(i, []):
            if fn is not project_out:
                fn(g, c)
        if i + 1 < len(items):
            st_next = scores(*items[i + 1])
        ot, inv = attend(j, p, st)
        if pending is not None:
            finish(*pending)
        pending = (j, p, ot, inv)
        for fn, g, c in side.get(i, []):
            if fn is project_out:
                fn(g, c)
    finish(*pending)
    for c in range(n_chunks):
        project_out(n_groups - 1, c)

    for ref in (k0_ref, k1_ref):
        ref[0:WINDOW, :] = ref[tm:tm + WINDOW, :]
    for ref in (vt0_ref, vt1_ref):
        ref[:, 0:WINDOW] = ref[:, tm:tm + WINDOW]


def _mixer_b_prompt(x, npre, npost, win, sinks, cos, sin, wout, *, tm, seq):
    n = x.shape[0]
    tiles_per_seq = seq // tm
    n_seq = n // seq
    row_spec = pl.BlockSpec((tm, D_MODEL), lambda i: (i, 0))
    rope_spec = pl.BlockSpec((tm, V7X_LANES), lambda i: (i % tiles_per_seq, 0))
    kv_spec = pl.BlockSpec((WINDOW, KD), lambda i: (i // tiles_per_seq, 0))
    return pl.pallas_call(
        functools.partial(_mixer_b_prompt_kernel, tm=tm, tiles_per_seq=tiles_per_seq),
        grid=(n // tm,),
        in_specs=[row_spec, _const_spec((1, D_MODEL)), _const_spec((1, D_MODEL)),
                  _const_spec((D_MODEL, 2 * QD + 2 * KD)),
                  pl.BlockSpec(memory_space=pltpu.SMEM),
                  rope_spec, rope_spec, _const_spec((QD, D_MODEL))],
        out_specs=[row_spec, kv_spec, kv_spec],
        out_shape=[jax.ShapeDtypeStruct((n, D_MODEL), F32),
                   jax.ShapeDtypeStruct((n_seq * WINDOW, KD), F32),
                   jax.ShapeDtypeStruct((n_seq * WINDOW, KD), F32)],
        scratch_shapes=[pltpu.VMEM((tm + WINDOW, V7X_LANES), BF16)] * 2
        + [pltpu.VMEM((V7X_LANES, tm + WINDOW), BF16)] * 2 + [pltpu.VMEM((tm, QD), F32)]
        + [pltpu.VMEM(win.shape, BF16), pltpu.VMEM(wout.shape, BF16)],
        compiler_params=_params(),
        name="mixer_b_prompt",
    )(x, npre, npost, win, sinks, cos, sin, wout)


def _mixer_b_sample_proj_kernel(x_ref, npre_ref, win_ref, cos_ref, sin_ref,
                                q_ref, k_ref, v_ref, kt_ref, vt_ref, gate_ref):
    n = x_ref.shape[0]
    h = _rms(x_ref[...], npre_ref[...]).astype(BF16)
    cos, sin = cos_ref[...], sin_ref[...]
    scale = HEAD_DIM ** -0.5
    low_lanes = lax.broadcasted_iota(jnp.int32, (n, V7X_LANES), 1) < HEAD_DIM
    for p in range(N_HEADS // 2):
        qp = _rope(_dot(h, win_ref[:, p * V7X_LANES:(p + 1) * V7X_LANES]), cos, sin) * scale
        q_ref[pl.ds(2 * p, n, stride=N_HEADS), :] = jnp.where(low_lanes, qp, 0.0)
        q_ref[pl.ds(2 * p + 1, n, stride=N_HEADS), :] = jnp.where(
            low_lanes, pltpu.roll(qp, HEAD_DIM, 1), 0.0)
    k = _rope(_dot(h, win_ref[:, QD:QD + KD]), cos, sin)
    v = _dot(h, win_ref[:, QD + KD:QD + 2 * KD])
    k_ref[...] = k
    v_ref[...] = v
    kt_ref[...] = k.T
    vt_ref[...] = v.T
    gate_ref[...] = _silu(_dot(h, win_ref[:, QD + 2 * KD:]))


def _mixer_b_sample_proj(x, npre, win, cos_row, sin_row):
    n = x.shape[0]
    return pl.pallas_call(
        _mixer_b_sample_proj_kernel,
        out_shape=[jax.ShapeDtypeStruct((n * N_HEADS, V7X_LANES), F32),
                   jax.ShapeDtypeStruct((n, KD), F32), jax.ShapeDtypeStruct((n, KD), F32),
                   jax.ShapeDtypeStruct((KD, n), F32), jax.ShapeDtypeStruct((KD, n), F32),
                   jax.ShapeDtypeStruct((n, QD), F32)],
        compiler_params=pltpu.CompilerParams(vmem_limit_bytes=VMEM_LIMIT_BYTES),
        name="mixer_b_sample_proj",
    )(x, npre, win, cos_row, sin_row)


def _mixer_b_sample_attn_kernel(q_ref, kt_ref, vt_ref, kn_ref, vn_ref, knt_ref, vnt_ref, sinks_ref,
                                o_ref, kto_ref, vto_ref, *, bt):
    w = kt_ref.shape[3]
    n = bt * N_KV_HEADS
    kt = kt_ref[...].reshape(n, HEAD_DIM, w)
    vt = vt_ref[...].reshape(n, HEAD_DIM, w)
    q = q_ref[...].reshape(bt, N_HEADS, V7X_LANES)
    kn, vn = kn_ref[...], vn_ref[...]

    qb = q.astype(BF16)
    q_kv = qb.reshape(n, Q_PER_KV, V7X_LANES)[:, :, :HEAD_DIM]
    s_c = jnp.einsum("nhd,nds->nhs", q_kv, kt.astype(BF16),
                     preferred_element_type=F32).reshape(bt, N_HEADS, w)
    first_kv = lax.broadcasted_iota(jnp.int32, (bt, N_HEADS, V7X_LANES), 1) < Q_PER_KV
    kn_b = kn.astype(BF16).astype(F32)
    vn_b = vn.astype(BF16).astype(F32)
    k_sel = jnp.where(first_kv, kn_b[:, None, :], pltpu.roll(kn_b, HEAD_DIM, 1)[:, None, :])
    v_sel = jnp.where(first_kv, vn_b[:, None, :], pltpu.roll(vn_b, HEAD_DIM, 1)[:, None, :])
    s_n = jnp.sum(qb.astype(F32) * k_sel, axis=-1, keepdims=True)
    sink = sinks_ref[...][None]
    m = jnp.maximum(jnp.maximum(jnp.max(s_c, axis=-1, keepdims=True), s_n), sink)
    e_c = jnp.exp(s_c - m)
    e_n = jnp.exp(s_n - m)
    inv = 1.0 / (jnp.sum(e_c, axis=-1, keepdims=True) + e_n + jnp.exp(sink - m))
    p_c = (e_c * inv).astype(BF16).reshape(n, Q_PER_KV, w)
    o = jnp.einsum("nhs,nds->nhd", p_c, vt.astype(BF16),
                   preferred_element_type=F32).reshape(bt, N_HEADS, HEAD_DIM)
    p_n = (e_n * inv).astype(BF16).astype(F32)
    o = o + p_n * v_sel[:, :, :HEAD_DIM]
    o_ref[...] = jnp.zeros(o_ref.shape, F32)
    o_ref[:, 0:HEAD_DIM] = o.reshape(bt * N_HEADS, HEAD_DIM)

    is_first = pl.program_id(0) == 0
    knt = jnp.where(is_first, knt_ref[:, 0:bt], knt_ref[:, bt:])
    vnt = jnp.where(is_first, vnt_ref[:, 0:bt], vnt_ref[:, bt:])
    last = lax.broadcasted_iota(jnp.int32, (KD, w), 1) == w - 1
    for b in range(bt):
        for src, cols, dst in ((kt_ref, knt, kto_ref), (vt_ref, vnt, vto_ref)):
            old = src[b].reshape(KD, w)
            new = jnp.where(last, jnp.broadcast_to(cols[:, b:b + 1], (KD, w)),
                            pltpu.roll(old, w - 1, 1))
            dst[b] = new.reshape(N_KV_HEADS, HEAD_DIM, w)


def _mixer_b_sample_attn(q2, kt, vt, kn, vn, knt, vnt, sinks_col, *, bt):
    b, _, _, w = kt.shape
    assert b == 2 * bt
    q_spec = pl.BlockSpec((bt * N_HEADS, V7X_LANES), lambda i: (i, 0))
    c_spec = pl.BlockSpec((bt, N_KV_HEADS, HEAD_DIM, w), lambda i: (i, 0, 0, 0))
    n_spec = pl.BlockSpec((bt, KD), lambda i: (i, 0))
    t_spec = pl.BlockSpec((KD, b), lambda i: (0, 0))
    return pl.pallas_call(
        functools.partial(_mixer_b_sample_attn_kernel, bt=bt),
        grid=(b // bt,),
        in_specs=[q_spec, c_spec, c_spec, n_spec, n_spec, t_spec, t_spec,
                  pl.BlockSpec((N_HEADS, 1), lambda i: (0, 0))],
        out_specs=[q_spec, c_spec, c_spec],
        out_shape=[jax.ShapeDtypeStruct((b * N_HEADS, V7X_LANES), F32),
                   jax.ShapeDtypeStruct(kt.shape, F32), jax.ShapeDtypeStruct(vt.shape, F32)],
        compiler_params=_params(),
        name="mixer_b_sample_attn",
    )(q2, kt, vt, kn, vn, knt, vnt, sinks_col)


def _mixer_b_sample_out_kernel(x_ref, att_ref, gate_ref, wout_ref, npost_ref, o_ref):
    n = x_ref.shape[0]
    low_lanes = lax.broadcasted_iota(jnp.int32, (n, V7X_LANES), 1) < HEAD_DIM
    pairs = []
    for p in range(N_HEADS // 2):
        even = att_ref[pl.ds(2 * p, n, stride=N_HEADS), :]
        odd = att_ref[pl.ds(2 * p + 1, n, stride=N_HEADS), :]
        pairs.append(jnp.where(low_lanes, even, pltpu.roll(odd, HEAD_DIM, 1)))
    att = jnp.concatenate(pairs, axis=1)
    y = _dot((att * gate_ref[...]).astype(BF16), wout_ref[...])
    o_ref[...] = x_ref[...] + _rms(y, npost_ref[...])


def _mixer_b_sample_out(x, att, gate, wout, npost):
    return pl.pallas_call(
        _mixer_b_sample_out_kernel,
        out_shape=jax.ShapeDtypeStruct(x.shape, F32),
        compiler_params=pltpu.CompilerParams(vmem_limit_bytes=VMEM_LIMIT_BYTES),
        name="mixer_b_sample_out",
    )(x, att, gate, wout, npost)


def _rglru_gates(xc, wa_ref, ba_ref, wx_ref, bx_ref, lam_ref):
    xcb = xc.astype(BF16)
    ra, rx = [], []
    for blk in range(C_BLOCKS):
        cols = slice(blk * C_BLOCK_W, (blk + 1) * C_BLOCK_W)
        ra.append(_dot(xcb[:, cols], wa_ref[blk]))
        rx.append(_dot(xcb[:, cols], wx_ref[blk]))
    r = _sigmoid(jnp.concatenate(ra, axis=-1) + ba_ref[...])
    i_gate = _sigmoid(jnp.concatenate(rx, axis=-1) + bx_ref[...])
    log_a = (-LRU_C * r) * _softplus(-lam_ref[...])
    a = jnp.exp(log_a)
    mult = jnp.sqrt(-jnp.tanh(log_a) * (a * a + 1.0))
    return a, mult, i_gate * xc


def _mixer_c_prompt_kernel(x_ref, npre_ref, npost_ref, win_ref, cw_ref, cb_ref, wa_ref, ba_ref,
                           wx_ref, bx_ref, lam_ref, wout_ref, o_ref, conv_ref, hlast_ref,
                           xpad_ref, h_ref, win_bf_ref, wa_bf_ref, wx_bf_ref, wout_bf_ref,
                           *, tm, tiles_per_seq):
    @pl.when(pl.program_id(0) == 0)
    def _():
        for dst, src in ((win_bf_ref, win_ref), (wa_bf_ref, wa_ref), (wx_bf_ref, wx_ref),
                         (wout_bf_ref, wout_ref)):
            _cast_weights(dst, src)

    win_ref, wa_ref, wx_ref, wout_ref = win_bf_ref, wa_bf_ref, wx_bf_ref, wout_bf_ref
    first = (pl.program_id(0) % tiles_per_seq) == 0

    @pl.when(first)
    def _():
        xpad_ref[0:V7X_SUBLANES, :] = jnp.zeros((V7X_SUBLANES, D_RNN), F32)
        h_ref[...] = jnp.zeros((V7X_SUBLANES, D_RNN), F32)

    x = x_ref[...]
    h = _rms(x, npre_ref[...]).astype(BF16)
    xr = _dot(h, win_ref[:, :D_RNN])
    xpad_ref[V7X_SUBLANES:, :] = xr
    conv_ref[...] = xr[tm - V7X_SUBLANES:, :]
    cw = cw_ref[...]
    xc = cb_ref[...] + xr * cw[CONV_W - 1:CONV_W, :]
    for tap in range(CONV_W - 1):
        back = CONV_W - 1 - tap
        xc = xc + xpad_ref[V7X_SUBLANES - back:V7X_SUBLANES - back + tm, :] * cw[tap:tap + 1, :]
    xpad_ref[0:V7X_SUBLANES, :] = xr[tm - V7X_SUBLANES:, :]

    a, mult, gx = _rglru_gates(xc, wa_ref, ba_ref, wx_ref, bx_ref, lam_ref)
    row = lax.broadcasted_iota(jnp.int32, (tm, 1), 0)
    mult = jnp.where(first & (row == 0), 1.0, mult)
    b = mult * gx

    n_groups = tm // V7X_SUBLANES
    a = a.reshape(n_groups, V7X_SUBLANES, D_RNN)
    b = b.reshape(n_groups, V7X_SUBLANES, D_RNN)
    sub = lax.broadcasted_iota(jnp.int32, (n_groups, V7X_SUBLANES, D_RNN), 1)
    for dist in (1, 2, 4):
        keep = sub >= dist
        a_prev = pltpu.roll(a, dist, 1)
        b_prev = pltpu.roll(b, dist, 1)
        b = jnp.where(keep, a * b_prev + b, b)
        a = jnp.where(keep, a * a_prev, a)
    carry = h_ref[V7X_SUBLANES - 1:V7X_SUBLANES, :]
    groups = []
    for gi in range(n_groups):
        hs_g = a[gi] * carry + b[gi]
        carry = hs_g[V7X_SUBLANES - 1:V7X_SUBLANES, :]
        groups.append(hs_g)
    h_ref[...] = groups[-1]
    hlast_ref[...] = groups[-1]
    hs = jnp.concatenate(groups, axis=0)

    gate = _silu(_dot(h, win_ref[:, D_RNN:]))
    y = _dot((hs * gate).astype(BF16), wout_ref[...])
    o_ref[...] = x + _rms(y, npost_ref[...])


def _mixer_c_prompt(x, npre, npost, win, cw, cb, wa, ba, wx, bx, lam, wout, *, tm, seq):
    n = x.shape[0]
    tiles_per_seq = seq // tm
    n_seq = n // seq
    row_spec = pl.BlockSpec((tm, D_MODEL), lambda i: (i, 0))
    tail_spec = pl.BlockSpec((V7X_SUBLANES, D_RNN), lambda i: (i // tiles_per_seq, 0))
    vec = _const_spec((1, D_RNN))
    blk = _const_spec((C_BLOCKS, C_BLOCK_W, C_BLOCK_W))
    return pl.pallas_call(
        functools.partial(_mixer_c_prompt_kernel, tm=tm, tiles_per_seq=tiles_per_seq),
        grid=(n // tm,),
        in_specs=[row_spec, vec, vec, _const_spec((D_MODEL, 2 * D_RNN)),
                  _const_spec((CONV_W, D_RNN)), vec, blk, vec, blk, vec, vec,
                  _const_spec((D_RNN, D_MODEL))],
        out_specs=[row_spec, tail_spec, tail_spec],
        out_shape=[jax.ShapeDtypeStruct((n, D_MODEL), F32),
                   jax.ShapeDtypeStruct((n_seq * V7X_SUBLANES, D_RNN), F32),
                   jax.ShapeDtypeStruct((n_seq * V7X_SUBLANES, D_RNN), F32)],
        scratch_shapes=[pltpu.VMEM((tm + V7X_SUBLANES, D_RNN), F32),
                        pltpu.VMEM((V7X_SUBLANES, D_RNN), F32)]
        + [pltpu.VMEM(w.shape, BF16) for w in (win, wa, wx, wout)],
        compiler_params=_params(),
        name="mixer_c_prompt",
    )(x, npre, npost, win, cw, cb, wa, ba, wx, bx, lam, wout)


def _mixer_c_sample_kernel(x_ref, npre_ref, npost_ref, win_ref, cw_ref, cb_ref, wa_ref, ba_ref,
                           wx_ref, bx_ref, lam_ref, wout_ref, conv_ref, h0_ref,
                           o_ref, conv_out_ref, h_out_ref):
    x = x_ref[...]
    h = _rms(x, npre_ref[...]).astype(BF16)
    xr = _dot(h, win_ref[:, :D_RNN])
    cw = cw_ref[...]
    xc = cb_ref[...] + xr * cw[CONV_W - 1:CONV_W, :]
    for tap in range(CONV_W - 1):
        xc = xc + conv_ref[tap] * cw[tap:tap + 1, :]
        if tap > 0:
            conv_out_ref[tap - 1] = conv_ref[tap]
    conv_out_ref[CONV_W - 2] = xr
    a, mult, gx = _rglru_gates(xc, wa_ref, ba_ref, wx_ref, bx_ref, lam_ref)
    hs = a * h0_ref[...] + mult * gx
    h_out_ref[...] = hs
    gate = _silu(_dot(h, win_ref[:, D_RNN:]))
    y = _dot((hs * gate).astype(BF16), wout_ref[...])
    o_ref[...] = x + _rms(y, npost_ref[...])


def _mixer_c_sample(x, npre, npost, win, cw, cb, wa, ba, wx, bx, lam, wout, conv_t, h0):
    n = x.shape[0]
    return pl.pallas_call(
        _mixer_c_sample_kernel,
        out_shape=[jax.ShapeDtypeStruct((n, D_MODEL), F32),
                   jax.ShapeDtypeStruct((CONV_W - 1, n, D_RNN), F32),
                   jax.ShapeDtypeStruct((n, D_RNN), F32)],
        compiler_params=pltpu.CompilerParams(vmem_limit_bytes=VMEM_LIMIT_BYTES),
        name="mixer_c_sample",
    )(x, npre, npost, win, cw, cb, wa, ba, wx, bx, lam, wout, conv_t, h0)


def kernel(x_prompt, x_sample, cache_b_k, cache_b_v, state_c_conv, state_c_h, norm_pre, norm_post,
           a_w_in, a_ln_g, a_ln_b, a_w_s, a_b_s, a_w_out, b_w_in, b_sinks, b_w_out, c_w_in,
           c_conv_w, c_conv_b, c_w_a, c_b_a, c_w_x, c_b_x, c_lam, c_w_out):
    batch, seq, _ = x_prompt.shape
    dec_batch, dec_seq, _ = x_sample.shape
    past_len = PAST_LEN
    w_buf = cache_b_k.shape[2]
    assert dec_seq == 1 and w_buf == WINDOW and seq % ROW_TILE == 0
    assert ROW_TILE % B_PROJ_ROWS == 0 and B_PROJ_ROWS % WINDOW == 0
    assert ROW_TILE % A_SUB_ROWS == 0 and A_SUB_ROWS % CHUNK == 0

    xp = x_prompt.reshape(batch * seq, D_MODEL)
    xs = x_sample.reshape(dec_batch, D_MODEL)
    row = lambda a: a.reshape(1, -1)

    cos_p, sin_p = _rope_tables(jnp.arange(seq, dtype=jnp.int32))
    cos_s, sin_s = _rope_tables(past_len + jnp.arange(dec_seq, dtype=jnp.int32))

    a_v_s = []
    b_kp, b_vp, b_ks, b_vs = [], [], [], []
    c_cp, c_hp, c_cs, c_hs = [], [], [], []
    for i in range(DEPTH):
        kind, j = i % N_MIXERS, i // N_MIXERS
        npre, npost = row(norm_pre[i]), row(norm_post[i])
        if kind == 0:
            xp, xs, vs = _mixer_a(xp, xs, npre, npost, a_w_in, row(a_ln_g[j]), row(a_ln_b[j]),
                                  a_w_s[j], a_b_s[j].T, a_w_out, layer=j, tm=ROW_TILE)
            a_v_s.append(vs.reshape(dec_batch, dec_seq, D_A))
        elif kind == 1:
            win, wout = b_w_in[j], b_w_out[j]
            xp, kp, vp = _mixer_b_prompt(xp, npre, npost, win, row(b_sinks[j]), cos_p, sin_p, wout,
                                         tm=ROW_TILE, seq=seq)
            b_kp.append(kp.reshape(batch, WINDOW, N_KV_HEADS, HEAD_DIM))
            b_vp.append(vp.reshape(batch, WINDOW, N_KV_HEADS, HEAD_DIM))

            q2, kn, vn, knt, vnt, gate = _mixer_b_sample_proj(xs, npre, win, cos_s, sin_s)
            to_stored = lambda c: jnp.transpose(c, (0, 2, 3, 1))
            om, kt_new, vt_new = _mixer_b_sample_attn(
                q2, to_stored(cache_b_k[j]), to_stored(cache_b_v[j]), kn, vn, knt, vnt,
                b_sinks[j].reshape(N_HEADS, 1), bt=SAMPLE_BATCH_TILE)
            xs = _mixer_b_sample_out(xs, om, gate, wout, npost)
            b_ks.append(jnp.transpose(kt_new, (0, 3, 1, 2)))
            b_vs.append(jnp.transpose(vt_new, (0, 3, 1, 2)))
        else:
            args = (npre, npost, c_w_in[j], c_conv_w[j], row(c_conv_b[j]),
                    c_w_a[j], row(c_b_a[j]), c_w_x[j], row(c_b_x[j]),
                    row(c_lam[j]), c_w_out[j])
            xp, conv_tail, h_tail = _mixer_c_prompt(xp, *args, tm=ROW_TILE, seq=seq)
            c_cp.append(conv_tail.reshape(batch, V7X_SUBLANES, D_RNN)[:, V7X_SUBLANES - (CONV_W - 1):])
            c_hp.append(h_tail.reshape(batch, V7X_SUBLANES, D_RNN)[:, V7X_SUBLANES - 1])
            xs, conv_new, h_new = _mixer_c_sample(
                xs, *args, jnp.transpose(state_c_conv[j], (1, 0, 2)), state_c_h[j])
            c_cs.append(jnp.transpose(conv_new, (1, 0, 2)))
            c_hs.append(h_new)

    return (xp.reshape(batch, seq, D_MODEL), xs.reshape(dec_batch, dec_seq, D_MODEL),
            jnp.stack(a_v_s), jnp.stack(b_kp), jnp.stack(b_vp), jnp.stack(b_ks), jnp.stack(b_vs),
            jnp.stack(c_cp), jnp.stack(c_hp), jnp.stack(c_cs), jnp.stack(c_hs))
```

```python
import functools

import jax
import jax.numpy as jnp
import numpy as np
from jax import lax
from jax.experimental import pallas as pl
from jax.experimental.pallas import tpu as pltpu

D_MODEL = 1024
DEPTH = 4
N_MIXERS = 3
NORM_EPS = 1e-6
LN_EPS = 1e-5

D_A = 2 * D_MODEL
CHUNK = 128
A_GROUPS = 4
A_GROUP_W = D_A // A_GROUPS

HEAD_DIM = 64
N_HEADS = D_MODEL // HEAD_DIM
N_KV_HEADS = N_HEADS // 8
Q_PER_KV = N_HEADS // N_KV_HEADS
WINDOW = 128
ROPE_THETA = 10000.0
QD = N_HEADS * HEAD_DIM
KD = N_KV_HEADS * HEAD_DIM

D_RNN = D_MODEL
C_BLOCKS = 4
C_BLOCK_W = D_RNN // C_BLOCKS
CONV_W = 4
LRU_C = 8.0

PAST_LEN = 8192

V7X_LANES = 128
V7X_SUBLANES = 8
V7X_VMEM_BYTES = 64 * 1024 * 1024
VMEM_LIMIT_BYTES = V7X_VMEM_BYTES - 8 * 1024 * 1024

ROW_TILE = 1024
A_SUB_ROWS = 512
A_STAGE_SLOTS, A_STAGE_ROWS, A_STAGE_COLS = 4, 1024, 256
B_PROJ_ROWS = 256
SAMPLE_BATCH_TILE = 64

BF16 = jnp.bfloat16
F32 = jnp.float32
SQRT_2_OVER_PI = np.float32(np.sqrt(2.0 / np.pi))
GELU_CUBIC = np.float32(np.sqrt(2.0 / np.pi) * 0.044715)


def _dot(a, b):
    return jnp.dot(a, b.astype(BF16), preferred_element_type=F32)


def _cast_weights(dst_ref, src_ref):
    if len(src_ref.shape) == 3:
        for blk in range(src_ref.shape[0]):
            dst_ref[blk] = src_ref[blk].astype(BF16)
    else:
        for c0 in range(0, src_ref.shape[1], 2 * V7X_LANES):
            dst_ref[:, c0:c0 + 2 * V7X_LANES] = src_ref[:, c0:c0 + 2 * V7X_LANES].astype(BF16)


def _rms(x, g):
    return x * lax.rsqrt(jnp.mean(x * x, axis=-1, keepdims=True) + NORM_EPS) * g


def _gelu(x):
    inner = x * (SQRT_2_OVER_PI + GELU_CUBIC * (x * x))
    return x * (0.5 + 0.5 * jnp.tanh(inner))


def _sigmoid(x):
    return 1.0 / (1.0 + jnp.exp(-x))


def _silu(x):
    return x * _sigmoid(x)


def _softplus(x):
    return jnp.maximum(x, 0.0) + jnp.log1p(jnp.exp(-jnp.abs(x)))


def _const_spec(shape):
    zeros = (0,) * len(shape)
    return pl.BlockSpec(shape, lambda i: zeros, pipeline_mode=pl.Buffered(1))


def _params():
    return pltpu.CompilerParams(dimension_semantics=("arbitrary",),
                                vmem_limit_bytes=VMEM_LIMIT_BYTES)


def _layer_norm_a(v, lng_ref, lnb_ref):
    d = v - jnp.mean(v, axis=-1, keepdims=True)
    var = jnp.mean(d * d, axis=-1, keepdims=True)
    return d * lax.rsqrt(var + LN_EPS) * lng_ref[...] + lnb_ref[...]


def _mixer_a_sample_kernel(x_ref, npre_ref, npost_ref, win_ref, lng_ref, lnb_ref, ws_ref, bst_ref,
                           wout_ref, o_ref, v_ref):
    x = x_ref[...]
    h = _rms(x, npre_ref[...]).astype(BF16)
    vn = _layer_norm_a(_gelu(_dot(h, win_ref[:, D_A:2 * D_A])), lng_ref, lnb_ref)
    v_ref[...] = vn
    acc = jnp.zeros(x.shape, F32)
    for g in range(A_GROUPS):
        c0 = g * A_GROUP_W
        u = _gelu(_dot(h, win_ref[:, c0:c0 + A_GROUP_W]))
        gate = _silu(_dot(h, win_ref[:, 2 * D_A + c0:2 * D_A + c0 + A_GROUP_W]))
        mixed = ws_ref[g][0:1, 0:1] * vn[:, c0:c0 + A_GROUP_W] + bst_ref[0:1, g:g + 1]
        acc = acc + _dot((u * mixed * gate).astype(BF16), wout_ref[c0:c0 + A_GROUP_W, :])
    o_ref[...] = x + _rms(acc, npost_ref[...])


def _mixer_a_prompt_kernel(x_ref, npre_ref, npost_ref, win_ref, lng_ref, lnb_ref, ws_ref, bst_ref,
                           wout_ref, o_ref, *, tm, sub):
    x = x_ref[...]
    h = _rms(x, npre_ref[...]).astype(BF16)
    row = lax.broadcasted_iota(jnp.int32, (CHUNK, CHUNK), 0)
    col = lax.broadcasted_iota(jnp.int32, (CHUNK, CHUNK), 1)
    causal = row >= col
    n_sub = tm // sub
    items = [(r, g) for r in range(n_sub) for g in range(A_GROUPS)]
    v_parts = {r: [] for r in range(n_sub)}
    vnb, acc, ws_masked = {}, {}, {}

    def sub_rows(r):
        return slice(r * sub, (r + 1) * sub)

    def v_part(r, c):
        c0 = D_A + c * A_GROUP_W
        v_parts[r].append(_gelu(_dot(h[sub_rows(r)], win_ref[:, c0:c0 + A_GROUP_W])))

    def v_finish(r):
        v = jnp.concatenate(v_parts.pop(r), axis=1)
        vnb[r] = _layer_norm_a(v, lng_ref, lnb_ref).astype(BF16)

    def front(r, g):
        c0 = g * A_GROUP_W
        hr = h[sub_rows(r)]
        u = _dot(hr, win_ref[:, c0:c0 + A_GROUP_W])
        gate = _dot(hr, win_ref[:, 2 * D_A + c0:2 * D_A + c0 + A_GROUP_W])
        if g not in ws_masked:
            ws_masked[g] = jnp.where(causal, ws_ref[g], 0.0).astype(BF16)
        mixed = jnp.concatenate(
            [_dot(ws_masked[g], vnb[r][c * CHUNK:(c + 1) * CHUNK, c0:c0 + A_GROUP_W])
             for c in range(sub // CHUNK)], axis=0)
        return u, gate, mixed

    def back(r, g, u, gate, mixed):
        c0 = g * A_GROUP_W
        bias = jnp.concatenate([bst_ref[:, g:g + 1]] * (sub // CHUNK), axis=0)
        z = (_gelu(u) * (mixed + bias) * _silu(gate)).astype(BF16)
        y = _dot(z, wout_ref[c0:c0 + A_GROUP_W, :])
        acc[r] = y if g == 0 else acc[r] + y
        if g == A_GROUPS - 1:
            rows = sub_rows(r)
            o_ref[rows, :] = x[rows] + _rms(acc.pop(r), npost_ref[...])

    for c in range(A_GROUPS):
        v_part(0, c)
    v_finish(0)
    nxt = front(*items[0])
    for k, (r, g) in enumerate(items):
        cur = nxt
        if r + 1 < n_sub:
            v_part(r + 1, g)
            if g == A_GROUPS - 1:
                v_finish(r + 1)
        if k + 1 < len(items):
            nxt = front(*items[k + 1])
        back(r, g, *cur)


def _stage_weights(hbm_ref, layer, dst_ref, stage_ref, sem_ref):
    rows, cols = dst_ref.shape
    n_slots, piece_rows, piece_cols = stage_ref.shape
    ahead = n_slots - 1
    pieces = [(r0, c0) for r0 in range(0, rows, piece_rows) for c0 in range(0, cols, piece_cols)]

    def copy(k):
        r0, c0 = pieces[k]
        src = hbm_ref.at[layer, pl.ds(r0, piece_rows), pl.ds(c0, piece_cols)]
        return pltpu.make_async_copy(src, stage_ref.at[k % n_slots], sem_ref.at[k % n_slots])

    for k in range(min(ahead, len(pieces))):
        copy(k).start()
    for k, (r0, c0) in enumerate(pieces):
        if k + ahead < len(pieces):
            copy(k + ahead).start()
        copy(k).wait()
        dst_ref[r0:r0 + piece_rows, c0:c0 + piece_cols] = stage_ref[k % n_slots].astype(BF16)


def _mixer_a_kernel(x_ref, xs_ref, npre_ref, npost_ref, win_hbm, lng_ref, lnb_ref, ws_ref, bst_ref,
                    wout_hbm, o_ref, os_ref, vs_ref, win_ref, wout_ref, stage_ref, sem_ref,
                    *, layer, tm, sub, n_tiles):
    step = pl.program_id(0)

    @pl.when(step == 0)
    def _():
        _stage_weights(win_hbm, layer, win_ref, stage_ref, sem_ref)
        _stage_weights(wout_hbm, layer, wout_ref, stage_ref, sem_ref)

    @pl.when(step < n_tiles)
    def _():
        _mixer_a_prompt_kernel(x_ref, npre_ref, npost_ref, win_ref, lng_ref, lnb_ref, ws_ref,
                               bst_ref, wout_ref, o_ref, tm=tm, sub=sub)

    @pl.when(step == n_tiles)
    def _():
        _mixer_a_sample_kernel(xs_ref, npre_ref, npost_ref, win_ref, lng_ref, lnb_ref, ws_ref,
                               bst_ref, wout_ref, os_ref, vs_ref)


def _mixer_a(x, xs, npre, npost, win, lng, lnb, ws, bst, wout, *, layer, tm):
    n, ns = x.shape[0], xs.shape[0]
    n_tiles = n // tm
    row_spec = pl.BlockSpec((tm, D_MODEL), lambda i: (jnp.minimum(i, n_tiles - 1), 0))
    hbm_spec = pl.BlockSpec(memory_space=pl.ANY)
    return pl.pallas_call(
        functools.partial(_mixer_a_kernel, layer=layer, tm=tm, sub=A_SUB_ROWS, n_tiles=n_tiles),
        grid=(n_tiles + 1,),
        in_specs=[row_spec, _const_spec((ns, D_MODEL)), _const_spec((1, D_MODEL)),
                  _const_spec((1, D_MODEL)), hbm_spec, _const_spec((1, D_A)),
                  _const_spec((1, D_A)), _const_spec((A_GROUPS, CHUNK, CHUNK)),
                  _const_spec((CHUNK, A_GROUPS)), hbm_spec],
        out_specs=[row_spec, pl.BlockSpec((ns, D_MODEL), lambda i: (0, 0)),
                   pl.BlockSpec((ns, D_A), lambda i: (0, 0))],
        out_shape=[jax.ShapeDtypeStruct((n, D_MODEL), F32), jax.ShapeDtypeStruct((ns, D_MODEL), F32),
                   jax.ShapeDtypeStruct((ns, D_A), F32)],
        scratch_shapes=[pltpu.VMEM((D_MODEL, 3 * D_A), BF16), pltpu.VMEM((D_A, D_MODEL), BF16),
                        pltpu.VMEM((A_STAGE_SLOTS, A_STAGE_ROWS, A_STAGE_COLS), F32),
                        pltpu.SemaphoreType.DMA((A_STAGE_SLOTS,))],
        compiler_params=_params(),
        name="mixer_a",
    )(x, xs, npre, npost, win, lng, lnb, ws, bst, wout)


def _rope_tables(positions):
    half = HEAD_DIM // 2
    inv_freq = ROPE_THETA ** (-jnp.arange(half, dtype=F32) / half)
    ang = positions.astype(F32)[:, None] * inv_freq[None, :]
    cos, sin = jnp.cos(ang), jnp.sin(ang)
    reps = V7X_LANES // HEAD_DIM
    cos_t = jnp.tile(jnp.concatenate([cos, cos], axis=-1), (1, reps))
    sin_t = jnp.tile(jnp.concatenate([-sin, sin], axis=-1), (1, reps))
    return cos_t, sin_t


def _rope(x, cos, sin_signed):
    lane = lax.broadcasted_iota(jnp.int32, x.shape, 1)
    first_half = (lane % HEAD_DIM) < (HEAD_DIM // 2)
    rot = jnp.where(first_half, pltpu.roll(x, V7X_LANES - HEAD_DIM // 2, 1),
                    pltpu.roll(x, HEAD_DIM // 2, 1))
    return x * cos + rot * sin_signed


def _dup_kv_halves(x, lane):
    swapped = pltpu.roll(x, HEAD_DIM, 1)
    low = lane < HEAD_DIM
    return jnp.where(low, x, swapped), jnp.where(low, swapped, x)


def _mixer_b_prompt_kernel(x_ref, npre_ref, npost_ref, win_ref, sinks_ref, cos_ref, sin_ref,
                           wout_ref, o_ref, kout_ref, vout_ref,
                           k0_ref, k1_ref, vt0_ref, vt1_ref, att_ref, win_bf_ref, wout_bf_ref,
                           *, tm, tiles_per_seq):
    @pl.when(pl.program_id(0) == 0)
    def _():
        _cast_weights(win_bf_ref, win_ref)
        _cast_weights(wout_bf_ref, wout_ref)

    win_ref, wout_ref = win_bf_ref, wout_bf_ref
    first = (pl.program_id(0) % tiles_per_seq) == 0

    @pl.when(first)
    def _():
        for ref in (k0_ref, k1_ref):
            ref[0:WINDOW, :] = jnp.zeros((WINDOW, V7X_LANES), BF16)
        for ref in (vt0_ref, vt1_ref):
            ref[:, 0:WINDOW] = jnp.zeros((V7X_LANES, WINDOW), BF16)

    x = x_ref[...]
    h = _rms(x, npre_ref[...]).astype(BF16)
    lane = lax.broadcasted_iota(jnp.int32, (tm, V7X_LANES), 1)
    cos, sin = cos_ref[...], sin_ref[...]

    k = _rope(_dot(h, win_ref[:, QD:QD + KD]), cos, sin)
    v = _dot(h, win_ref[:, QD + KD:QD + 2 * KD])
    kout_ref[...] = k[tm - WINDOW:, :]
    vout_ref[...] = v[tm - WINDOW:, :]
    kd0, kd1 = _dup_kv_halves(k, lane)
    k0_ref[WINDOW:, :] = kd0.astype(BF16)
    k1_ref[WINDOW:, :] = kd1.astype(BF16)
    vt = v.T
    vt0_ref[:, WINDOW:] = jnp.concatenate([vt[:HEAD_DIM], vt[:HEAD_DIM]], axis=0).astype(BF16)
    vt1_ref[:, WINDOW:] = jnp.concatenate([vt[HEAD_DIM:], vt[HEAD_DIM:]], axis=0).astype(BF16)

    ci = lax.broadcasted_iota(jnp.int32, (2 * WINDOW, 2 * WINDOW), 0)
    qi = lax.broadcasted_iota(jnp.int32, (2 * WINDOW, 2 * WINDOW), 1) % WINDOW
    band = (ci >= qi) & (ci <= qi + WINDOW)
    bias = jnp.where(band, 0.0, -jnp.inf)
    bias_first = jnp.where(band & (ci >= jnp.where(first, WINDOW, 0)), 0.0, -jnp.inf)
    head_lane = lax.broadcasted_iota(jnp.int32, (1, 2 * WINDOW), 1) < WINDOW
    low_lanes = lax.broadcasted_iota(jnp.int32, (WINDOW, V7X_LANES), 1) < HEAD_DIM
    top_rows = lax.broadcasted_iota(jnp.int32, (V7X_LANES, WINDOW), 0) < HEAD_DIM
    scale = HEAD_DIM ** -0.5

    n_blocks = tm // WINDOW
    n_pairs = N_HEADS // 2
    chunk = 2 * V7X_LANES
    n_chunks = QD // chunk
    bpg = B_PROJ_ROWS // WINDOW
    n_groups = tm // B_PROJ_ROWS
    items = [(j, p) for j in range(n_blocks) for p in range(n_pairs)]
    q_chunks, gate_chunks, gated, y_chunks = {}, {}, {}, {}

    def block_rows(j):
        return slice(j * WINDOW, (j + 1) * WINDOW)

    def group_rows(g):
        return slice(g * B_PROJ_ROWS, (g + 1) * B_PROJ_ROWS)

    def project_q(g, c):
        q_chunks[g, c] = _dot(h[group_rows(g)], win_ref[:, c * chunk:(c + 1) * chunk])

    def project_gate(g, c):
        c0 = QD + 2 * KD + c * chunk
        gate_chunks[g, c] = _silu(_dot(h[group_rows(g)], win_ref[:, c0:c0 + chunk]))

    def project_out(g, c):
        if c == 0:
            gated[g] = jnp.concatenate(
                [att_ref[group_rows(g), cc * chunk:(cc + 1) * chunk] * gate_chunks.pop((g, cc))
                 for cc in range(n_chunks)], axis=1).astype(BF16)
        y_chunks[g, c] = _dot(gated[g], wout_ref[:, c * chunk:(c + 1) * chunk])
        if c == n_chunks - 1:
            y = jnp.concatenate([y_chunks.pop((g, cc)) for cc in range(n_chunks)], axis=1)
            rows = group_rows(g)
            o_ref[rows, :] = x[rows] + _rms(y, npost_ref[...])

    def scores(j, p):
        rows = block_rows(j)
        kref = k0_ref if (2 * p) // Q_PER_KV == 0 else k1_ref
        half = (p % 2) * V7X_LANES
        r0 = (j % bpg) * WINDOW
        qp = q_chunks[j // bpg, p // 2][r0:r0 + WINDOW, half:half + V7X_LANES]
        qp = _rope(qp, cos[rows], sin[rows]) * scale
        qs = jnp.concatenate([jnp.where(low_lanes, qp, 0.0), jnp.where(low_lanes, 0.0, qp)],
                             axis=0).astype(BF16)
        st = lax.dot_general(kref[j * WINDOW:(j + 2) * WINDOW, :], qs, (((1,), (1,)), ((), ())),
                             preferred_element_type=F32)
        return st + (bias_first if j == 0 else bias)

    def attend(j, p, st):
        vtref = vt0_ref if (2 * p) // Q_PER_KV == 0 else vt1_ref
        sink = jnp.where(head_lane, sinks_ref[0, 2 * p], sinks_ref[0, 2 * p + 1])
        m = jnp.maximum(jnp.max(st, axis=0, keepdims=True), sink)
        e = jnp.exp(st - m)
        inv = 1.0 / (jnp.sum(e, axis=0, keepdims=True) + jnp.exp(sink - m))
        return _dot(vtref[:, j * WINDOW:(j + 2) * WINDOW], e.astype(BF16)), inv

    def finish(j, p, ot, inv):
        ot = ot * inv
        pair = jnp.where(top_rows, ot[:, :WINDOW], ot[:, WINDOW:])
        att_ref[block_rows(j), p * V7X_LANES:(p + 1) * V7X_LANES] = pair.T

    items_per_group = bpg * n_pairs
    side = {}
    for g in range(n_groups):
        tasks = []
        for c in range(n_chunks):
            if g + 1 < n_groups:
                tasks.append((project_q, g + 1, c))
            tasks.append((project_gate, g, c))
            if g > 0:
                tasks.append((project_out, g - 1, c))
        for t, task in enumerate(tasks):
            at = g * items_per_group + (t * items_per_group) // len(tasks)
            side.setdefault(at, []).append(task)

    for c in range(n_chunks):
        project_q(0, c)
    st_next = scores(*items[0])
    pending = None
    for i, (j, p) in enumerate(items):
        st = st_next
        for fn, g, c in side.get(i, []):
            if fn is not project_out:
                fn(g, c)
        if i + 1 < len(items):
            st_next = scores(*items[i + 1])
        ot, inv = attend(j, p, st)
        if pending is not None:
            finish(*pending)
        pending = (j, p, ot, inv)
        for fn, g, c in side.get(i, []):
            if fn is project_out:
                fn(g, c)
    finish(*pending)
    for c in range(n_chunks):
        project_out(n_groups - 1, c)

    for ref in (k0_ref, k1_ref):
        ref[0:WINDOW, :] = ref[tm:tm + WINDOW, :]
    for ref in (vt0_ref, vt1_ref):
        ref[:, 0:WINDOW] = ref[:, tm:tm + WINDOW]


def _mixer_b_prompt(x, npre, npost, win, sinks, cos, sin, wout, *, tm, seq):
    n = x.shape[0]
    tiles_per_seq = seq // tm
    n_seq = n // seq
    row_spec = pl.BlockSpec((tm, D_MODEL), lambda i: (i, 0))
    rope_spec = pl.BlockSpec((tm, V7X_LANES), lambda i: (i % tiles_per_seq, 0))
    kv_spec = pl.BlockSpec((WINDOW, KD), lambda i: (i // tiles_per_seq, 0))
    return pl.pallas_call(
        functools.partial(_mixer_b_prompt_kernel, tm=tm, tiles_per_seq=tiles_per_seq),
        grid=(n // tm,),
        in_specs=[row_spec, _const_spec((1, D_MODEL)), _const_spec((1, D_MODEL)),
                  _const_spec((D_MODEL, 2 * QD + 2 * KD)),
                  pl.BlockSpec(memory_space=pltpu.SMEM),
                  rope_spec, rope_spec, _const_spec((QD, D_MODEL))],
        out_specs=[row_spec, kv_spec, kv_spec],
        out_shape=[jax.ShapeDtypeStruct((n, D_MODEL), F32),
                   jax.ShapeDtypeStruct((n_seq * WINDOW, KD), F32),
                   jax.ShapeDtypeStruct((n_seq * WINDOW, KD), F32)],
        scratch_shapes=[pltpu.VMEM((tm + WINDOW, V7X_LANES), BF16)] * 2
        + [pltpu.VMEM((V7X_LANES, tm + WINDOW), BF16)] * 2 + [pltpu.VMEM((tm, QD), F32)]
        + [pltpu.VMEM(win.shape, BF16), pltpu.VMEM(wout.shape, BF16)],
        compiler_params=_params(),
        name="mixer_b_prompt",
    )(x, npre, npost, win, sinks, cos, sin, wout)


def _mixer_b_sample_proj_kernel(x_ref, npre_ref, win_ref, cos_ref, sin_ref,
                                q_ref, k_ref, v_ref, kt_ref, vt_ref, gate_ref):
    n = x_ref.shape[0]
    h = _rms(x_ref[...], npre_ref[...]).astype(BF16)
    cos, sin = cos_ref[...], sin_ref[...]
    scale = HEAD_DIM ** -0.5
    low_lanes = lax.broadcasted_iota(jnp.int32, (n, V7X_LANES), 1) < HEAD_DIM
    for p in range(N_HEADS // 2):
        qp = _rope(_dot(h, win_ref[:, p * V7X_LANES:(p + 1) * V7X_LANES]), cos, sin) * scale
        q_ref[pl.ds(2 * p, n, stride=N_HEADS), :] = jnp.where(low_lanes, qp, 0.0)
        q_ref[pl.ds(2 * p + 1, n, stride=N_HEADS), :] = jnp.where(
            low_lanes, pltpu.roll(qp, HEAD_DIM, 1), 0.0)
    k = _rope(_dot(h, win_ref[:, QD:QD + KD]), cos, sin)
    v = _dot(h, win_ref[:, QD + KD:QD + 2 * KD])
    k_ref[...] = k
    v_ref[...] = v
    kt_ref[...] = k.T
    vt_ref[...] = v.T
    gate_ref[...] = _silu(_dot(h, win_ref[:, QD + 2 * KD:]))


def _mixer_b_sample_proj(x, npre, win, cos_row, sin_row):
    n = x.shape[0]
    return pl.pallas_call(
        _mixer_b_sample_proj_kernel,
        out_shape=[jax.ShapeDtypeStruct((n * N_HEADS, V7X_LANES), F32),
                   jax.ShapeDtypeStruct((n, KD), F32), jax.ShapeDtypeStruct((n, KD), F32),
                   jax.ShapeDtypeStruct((KD, n), F32), jax.ShapeDtypeStruct((KD, n), F32),
                   jax.ShapeDtypeStruct((n, QD), F32)],
        compiler_params=pltpu.CompilerParams(vmem_limit_bytes=VMEM_LIMIT_BYTES),
        name="mixer_b_sample_proj",
    )(x, npre, win, cos_row, sin_row)


def _mixer_b_sample_attn_kernel(q_ref, kt_ref, vt_ref, kn_ref, vn_ref, knt_ref, vnt_ref, sinks_ref,
                                o_ref, kto_ref, vto_ref, *, bt):
    w = kt_ref.shape[3]
    n = bt * N_KV_HEADS
    kt = kt_ref[...].reshape(n, HEAD_DIM, w)
    vt = vt_ref[...].reshape(n, HEAD_DIM, w)
    q = q_ref[...].reshape(bt, N_HEADS, V7X_LANES)
    kn, vn = kn_ref[...], vn_ref[...]

    qb = q.astype(BF16)
    q_kv = qb.reshape(n, Q_PER_KV, V7X_LANES)[:, :, :HEAD_DIM]
    s_c = jnp.einsum("nhd,nds->nhs", q_kv, kt.astype(BF16),
                     preferred_element_type=F32).reshape(bt, N_HEADS, w)
    first_kv = lax.broadcasted_iota(jnp.int32, (bt, N_HEADS, V7X_LANES), 1) < Q_PER_KV
    kn_b = kn.astype(BF16).astype(F32)
    vn_b = vn.astype(BF16).astype(F32)
    k_sel = jnp.where(first_kv, kn_b[:, None, :], pltpu.roll(kn_b, HEAD_DIM, 1)[:, None, :])
    v_sel = jnp.where(first_kv, vn_b[:, None, :], pltpu.roll(vn_b, HEAD_DIM, 1)[:, None, :])
    s_n = jnp.sum(qb.astype(F32) * k_sel, axis=-1, keepdims=True)
    sink = sinks_ref[...][None]
    m = jnp.maximum(jnp.maximum(jnp.max(s_c, axis=-1, keepdims=True), s_n), sink)
    e_c = jnp.exp(s_c - m)
    e_n = jnp.exp(s_n - m)
    inv = 1.0 / (jnp.sum(e_c, axis=-1, keepdims=True) + e_n + jnp.exp(sink - m))
    p_c = (e_c * inv).astype(BF16).reshape(n, Q_PER_KV, w)
    o = jnp.einsum("nhs,nds->nhd", p_c, vt.astype(BF16),
                   preferred_element_type=F32).reshape(bt, N_HEADS, HEAD_DIM)
    p_n = (e_n * inv).astype(BF16).astype(F32)
    o = o + p_n * v_sel[:, :, :HEAD_DIM]
    o_ref[...] = jnp.zeros(o_ref.shape, F32)
    o_ref[:, 0:HEAD_DIM] = o.reshape(bt * N_HEADS, HEAD_DIM)

    is_first = pl.program_id(0) == 0
    knt = jnp.where(is_first, knt_ref[:, 0:bt], knt_ref[:, bt:])
    vnt = jnp.where(is_first, vnt_ref[:, 0:bt], vnt_ref[:, bt:])
    last = lax.broadcasted_iota(jnp.int32, (KD, w), 1) == w - 1
    for b in range(bt):
        for src, cols, dst in ((kt_ref, knt, kto_ref), (vt_ref, vnt, vto_ref)):
            old = src[b].reshape(KD, w)
            new = jnp.where(last, jnp.broadcast_to(cols[:, b:b + 1], (KD, w)),
                            pltpu.roll(old, w - 1, 1))
            dst[b] = new.reshape(N_KV_HEADS, HEAD_DIM, w)


def _mixer_b_sample_attn(q2, kt, vt, kn, vn, knt, vnt, sinks_col, *, bt):
    b, _, _, w = kt.shape
    assert b == 2 * bt
    q_spec = pl.BlockSpec((bt * N_HEADS, V7X_LANES), lambda i: (i, 0))
    c_spec = pl.BlockSpec((bt, N_KV_HEADS, HEAD_DIM, w), lambda i: (i, 0, 0, 0))
    n_spec = pl.BlockSpec((bt, KD), lambda i: (i, 0))
    t_spec = pl.BlockSpec((KD, b), lambda i: (0, 0))
    return pl.pallas_call(
        functools.partial(_mixer_b_sample_attn_kernel, bt=bt),
        grid=(b // bt,),
        in_specs=[q_spec, c_spec, c_spec, n_spec, n_spec, t_spec, t_spec,
                  pl.BlockSpec((N_HEADS, 1), lambda i: (0, 0))],
        out_specs=[q_spec, c_spec, c_spec],
        out_shape=[jax.ShapeDtypeStruct((b * N_HEADS, V7X_LANES), F32),
                   jax.ShapeDtypeStruct(kt.shape, F32), jax.ShapeDtypeStruct(vt.shape, F32)],
        compiler_params=_params(),
        name="mixer_b_sample_attn",
    )(q2, kt, vt, kn, vn, knt, vnt, sinks_col)


def _mixer_b_sample_out_kernel(x_ref, att_ref, gate_ref, wout_ref, npost_ref, o_ref):
    n = x_ref.shape[0]
    low_lanes = lax.broadcasted_iota(jnp.int32, (n, V7X_LANES), 1) < HEAD_DIM
    pairs = []
    for p in range(N_HEADS // 2):
        even = att_ref[pl.ds(2 * p, n, stride=N_HEADS), :]
        odd = att_ref[pl.ds(2 * p + 1, n, stride=N_HEADS), :]
        pairs.append(jnp.where(low_lanes, even, pltpu.roll(odd, HEAD_DIM, 1)))
    att = jnp.concatenate(pairs, axis=1)
    y = _dot((att * gate_ref[...]).astype(BF16), wout_ref[...])
    o_ref[...] = x_ref[...] + _rms(y, npost_ref[...])


def _mixer_b_sample_out(x, att, gate, wout, npost):
    return pl.pallas_call(
        _mixer_b_sample_out_kernel,
        out_shape=jax.ShapeDtypeStruct(x.shape, F32),
        compiler_params=pltpu.CompilerParams(vmem_limit_bytes=VMEM_LIMIT_BYTES),
        name="mixer_b_sample_out",
    )(x, att, gate, wout, npost)


def _rglru_gates(xc, wa_ref, ba_ref, wx_ref, bx_ref, lam_ref):
    xcb = xc.astype(BF16)
    ra, rx = [], []
    for blk in range(C_BLOCKS):
        cols = slice(blk * C_BLOCK_W, (blk + 1) * C_BLOCK_W)
        ra.append(_dot(xcb[:, cols], wa_ref[blk]))
        rx.append(_dot(xcb[:, cols], wx_ref[blk]))
    r = _sigmoid(jnp.concatenate(ra, axis=-1) + ba_ref[...])
    i_gate = _sigmoid(jnp.concatenate(rx, axis=-1) + bx_ref[...])
    log_a = r * (-LRU_C * _softplus(-lam_ref[...]))
    a = jnp.exp(log_a)
    mult = jnp.sqrt(-jnp.tanh(log_a) * (a * a + 1.0))
    return a, mult, i_gate * xc


def _mixer_c_prompt_kernel(x_ref, npre_ref, npost_ref, win_ref, cw_ref, cb_ref, wa_ref, ba_ref,
                           wx_ref, bx_ref, lam_ref, wout_ref, o_ref, conv_ref, hlast_ref,
                           xpad_ref, h_ref, win_bf_ref, wa_bf_ref, wx_bf_ref, wout_bf_ref,
                           *, tm, tiles_per_seq):
    @pl.when(pl.program_id(0) == 0)
    def _():
        for dst, src in ((win_bf_ref, win_ref), (wa_bf_ref, wa_ref), (wx_bf_ref, wx_ref),
                         (wout_bf_ref, wout_ref)):
            _cast_weights(dst, src)

    win_ref, wa_ref, wx_ref, wout_ref = win_bf_ref, wa_bf_ref, wx_bf_ref, wout_bf_ref
    first = (pl.program_id(0) % tiles_per_seq) == 0

    @pl.when(first)
    def _():
        xpad_ref[0:V7X_SUBLANES, :] = jnp.zeros((V7X_SUBLANES, D_RNN), F32)
        h_ref[...] = jnp.zeros((V7X_SUBLANES, D_RNN), F32)

    x = x_ref[...]
    h = _rms(x, npre_ref[...]).astype(BF16)
    xr = _dot(h, win_ref[:, :D_RNN])
    xpad_ref[V7X_SUBLANES:, :] = xr
    conv_ref[...] = xr[tm - V7X_SUBLANES:, :]
    cw = cw_ref[...]
    xc = cb_ref[...] + xr * cw[CONV_W - 1:CONV_W, :]
    for tap in range(CONV_W - 1):
        back = CONV_W - 1 - tap
        xc = xc + xpad_ref[V7X_SUBLANES - back:V7X_SUBLANES - back + tm, :] * cw[tap:tap + 1, :]
    xpad_ref[0:V7X_SUBLANES, :] = xr[tm - V7X_SUBLANES:, :]

    a, mult, gx = _rglru_gates(xc, wa_ref, ba_ref, wx_ref, bx_ref, lam_ref)
    row = lax.broadcasted_iota(jnp.int32, (tm, 1), 0)
    mult = jnp.where(first & (row == 0), 1.0, mult)
    b = mult * gx

    n_groups = tm // V7X_SUBLANES
    a = a.reshape(n_groups, V7X_SUBLANES, D_RNN)
    b = b.reshape(n_groups, V7X_SUBLANES, D_RNN)
    sub = lax.broadcasted_iota(jnp.int32, (n_groups, V7X_SUBLANES, D_RNN), 1)
    for dist in (1, 2, 4):
        keep = sub >= dist
        a_prev = pltpu.roll(a, dist, 1)
        b_prev = pltpu.roll(b, dist, 1)
        b = jnp.where(keep, a * b_prev + b, b)
        a = jnp.where(keep, a * a_prev, a)
    carry = h_ref[V7X_SUBLANES - 1:V7X_SUBLANES, :]
    groups = []
    for gi in range(n_groups):
        hs_g = a[gi] * carry + b[gi]
        carry = hs_g[V7X_SUBLANES - 1:V7X_SUBLANES, :]
        groups.append(hs_g)
    h_ref[...] = groups[-1]
    hlast_ref[...] = groups[-1]
    hs = jnp.concatenate(groups, axis=0)

    gate = _silu(_dot(h, win_ref[:, D_RNN:]))
    y = _dot((hs * gate).astype(BF16), wout_ref[...])
    o_ref[...] = x + _rms(y, npost_ref[...])


def _mixer_c_prompt(x, npre, npost, win, cw, cb, wa, ba, wx, bx, lam, wout, *, tm, seq):
    n = x.shape[0]
    tiles_per_seq = seq // tm
    n_seq = n // seq
    row_spec = pl.BlockSpec((tm, D_MODEL), lambda i: (i, 0))
    tail_spec = pl.BlockSpec((V7X_SUBLANES, D_RNN), lambda i: (i // tiles_per_seq, 0))
    vec = _const_spec((1, D_RNN))
    blk = _const_spec((C_BLOCKS, C_BLOCK_W, C_BLOCK_W))
    return pl.pallas_call(
        functools.partial(_mixer_c_prompt_kernel, tm=tm, tiles_per_seq=tiles_per_seq),
        grid=(n // tm,),
        in_specs=[row_spec, vec, vec, _const_spec((D_MODEL, 2 * D_RNN)),
                  _const_spec((CONV_W, D_RNN)), vec, blk, vec, blk, vec, vec,
                  _const_spec((D_RNN, D_MODEL))],
        out_specs=[row_spec, tail_spec, tail_spec],
        out_shape=[jax.ShapeDtypeStruct((n, D_MODEL), F32),
                   jax.ShapeDtypeStruct((n_seq * V7X_SUBLANES, D_RNN), F32),
                   jax.ShapeDtypeStruct((n_seq * V7X_SUBLANES, D_RNN), F32)],
        scratch_shapes=[pltpu.VMEM((tm + V7X_SUBLANES, D_RNN), F32),
                        pltpu.VMEM((V7X_SUBLANES, D_RNN), F32)]
        + [pltpu.VMEM(w.shape, BF16) for w in (win, wa, wx, wout)],
        compiler_params=_params(),
        name="mixer_c_prompt",
    )(x, npre, npost, win, cw, cb, wa, ba, wx, bx, lam, wout)


def _mixer_c_sample_kernel(x_ref, npre_ref, npost_ref, win_ref, cw_ref, cb_ref, wa_ref, ba_ref,
                           wx_ref, bx_ref, lam_ref, wout_ref, conv_ref, h0_ref,
                           o_ref, conv_out_ref, h_out_ref):
    x = x_ref[...]
    h = _rms(x, npre_ref[...]).astype(BF16)
    xr = _dot(h, win_ref[:, :D_RNN])
    cw = cw_ref[...]
    xc = cb_ref[...] + xr * cw[CONV_W - 1:CONV_W, :]
    for tap in range(CONV_W - 1):
        xc = xc + conv_ref[tap] * cw[tap:tap + 1, :]
        if tap > 0:
            conv_out_ref[tap - 1] = conv_ref[tap]
    conv_out_ref[CONV_W - 2] = xr
    a, mult, gx = _rglru_gates(xc, wa_ref, ba_ref, wx_ref, bx_ref, lam_ref)
    hs = a * h0_ref[...] + mult * gx
    h_out_ref[...] = hs
    gate = _silu(_dot(h, win_ref[:, D_RNN:]))
    y = _dot((hs * gate).astype(BF16), wout_ref[...])
    o_ref[...] = x + _rms(y, npost_ref[...])


def _mixer_c_sample(x, npre, npost, win, cw, cb, wa, ba, wx, bx, lam, wout, conv_t, h0):
    n = x.shape[0]
    return pl.pallas_call(
        _mixer_c_sample_kernel,
        out_shape=[jax.ShapeDtypeStruct((n, D_MODEL), F32),
                   jax.ShapeDtypeStruct((CONV_W - 1, n, D_RNN), F32),
                   jax.ShapeDtypeStruct((n, D_RNN), F32)],
        compiler_params=pltpu.CompilerParams(vmem_limit_bytes=VMEM_LIMIT_BYTES),
        name="mixer_c_sample",
    )(x, npre, npost, win, cw, cb, wa, ba, wx, bx, lam, wout, conv_t, h0)


def kernel(x_prompt, x_sample, cache_b_k, cache_b_v, state_c_conv, state_c_h, norm_pre, norm_post,
           a_w_in, a_ln_g, a_ln_b, a_w_s, a_b_s, a_w_out, b_w_in, b_sinks, b_w_out, c_w_in,
           c_conv_w, c_conv_b, c_w_a, c_b_a, c_w_x, c_b_x, c_lam, c_w_out):
    batch, seq, _ = x_prompt.shape
    dec_batch, dec_seq, _ = x_sample.shape
    past_len = PAST_LEN
    w_buf = cache_b_k.shape[2]
    assert dec_seq == 1 and w_buf == WINDOW and seq % ROW_TILE == 0
    assert ROW_TILE % B_PROJ_ROWS == 0 and B_PROJ_ROWS % WINDOW == 0
    assert ROW_TILE % A_SUB_ROWS == 0 and A_SUB_ROWS % CHUNK == 0

    xp = x_prompt.reshape(batch * seq, D_MODEL)
    xs = x_sample.reshape(dec_batch, D_MODEL)
    row = lambda a: a.reshape(1, -1)

    cos_p, sin_p = _rope_tables(jnp.arange(seq, dtype=jnp.int32))
    cos_s, sin_s = _rope_tables(past_len + jnp.arange(dec_seq, dtype=jnp.int32))

    a_v_s = []
    b_kp, b_vp, b_ks, b_vs = [], [], [], []
    c_cp, c_hp, c_cs, c_hs = [], [], [], []
    for i in range(DEPTH):
        kind, j = i % N_MIXERS, i // N_MIXERS
        npre, npost = row(norm_pre[i]), row(norm_post[i])
        if kind == 0:
            xp, xs, vs = _mixer_a(xp, xs, npre, npost, a_w_in, row(a_ln_g[j]), row(a_ln_b[j]),
                                  a_w_s[j], a_b_s[j].T, a_w_out, layer=j, tm=ROW_TILE)
            a_v_s.append(vs.reshape(dec_batch, dec_seq, D_A))
        elif kind == 1:
            win, wout = b_w_in[j], b_w_out[j]
            xp, kp, vp = _mixer_b_prompt(xp, npre, npost, win, row(b_sinks[j]), cos_p, sin_p, wout,
                                         tm=ROW_TILE, seq=seq)
            b_kp.append(kp.reshape(batch, WINDOW, N_KV_HEADS, HEAD_DIM))
            b_vp.append(vp.reshape(batch, WINDOW, N_KV_HEADS, HEAD_DIM))

            q2, kn, vn, knt, vnt, gate = _mixer_b_sample_proj(xs, npre, win, cos_s, sin_s)
            to_stored = lambda c: jnp.transpose(c, (0, 2, 3, 1))
            om, kt_new, vt_new = _mixer_b_sample_attn(
                q2, to_stored(cache_b_k[j]), to_stored(cache_b_v[j]), kn, vn, knt, vnt,
                b_sinks[j].reshape(N_HEADS, 1), bt=SAMPLE_BATCH_TILE)
            xs = _mixer_b_sample_out(xs, om, gate, wout, npost)
            b_ks.append(jnp.transpose(kt_new, (0, 3, 1, 2)))
            b_vs.append(jnp.transpose(vt_new, (0, 3, 1, 2)))
        else:
            args = (npre, npost, c_w_in[j], c_conv_w[j], row(c_conv_b[j]),
                    c_w_a[j], row(c_b_a[j]), c_w_x[j], row(c_b_x[j]),
                    row(c_lam[j]), c_w_out[j])
            xp, conv_tail, h_tail = _mixer_c_prompt(xp, *args, tm=ROW_TILE, seq=seq)
            c_cp.append(conv_tail.reshape(batch, V7X_SUBLANES, D_RNN)[:, V7X_SUBLANES - (CONV_W - 1):])
            c_hp.append(h_tail.reshape(batch, V7X_SUBLANES, D_RNN)[:, V7X_SUBLANES - 1])
            xs, conv_new, h_new = _mixer_c_sample(
                xs, *args, jnp.transpose(state_c_conv[j], (1, 0, 2)), state_c_h[j])
            c_cs.append(jnp.transpose(conv_new, (1, 0, 2)))
            c_hs.append(h_new)

    return (xp.reshape(batch, seq, D_MODEL), xs.reshape(dec_batch, dec_seq, D_MODEL),
            jnp.stack(a_v_s), jnp.stack(b_kp), jnp.stack(b_vp), jnp.stack(b_ks), jnp.stack(b_vs),
            jnp.stack(c_cp), jnp.stack(c_hp), jnp.stack(c_cs), jnp.stack(c_hs))
```

```python
import functools

import jax
import jax.numpy as jnp
import numpy as np
from jax import lax
from jax.experimental import pallas as pl
from jax.experimental.pallas import tpu as pltpu

D_MODEL = 1024
DEPTH = 4
N_MIXERS = 3
NORM_EPS = 1e-6
LN_EPS = 1e-5

D_A = 2 * D_MODEL
CHUNK = 128
A_GROUPS = 4
A_GROUP_W = D_A // A_GROUPS

HEAD_DIM = 64
N_HEADS = D_MODEL // HEAD_DIM
N_KV_HEADS = N_HEADS // 8
Q_PER_KV = N_HEADS // N_KV_HEADS
WINDOW = 128
ROPE_THETA = 10000.0
QD = N_HEADS * HEAD_DIM
KD = N_KV_HEADS * HEAD_DIM

D_RNN = D_MODEL
C_BLOCKS = 4
C_BLOCK_W = D_RNN // C_BLOCKS
CONV_W = 4
LRU_C = 8.0

PAST_LEN = 8192

V7X_LANES = 128
V7X_SUBLANES = 8
V7X_VMEM_BYTES = 64 * 1024 * 1024
VMEM_LIMIT_BYTES = V7X_VMEM_BYTES - 8 * 1024 * 1024

ROW_TILE = 1024
A_SUB_ROWS = 512
A_STAGE_SLOTS, A_STAGE_ROWS, A_STAGE_COLS = 4, 1024, 256
B_PROJ_ROWS = 256
SAMPLE_BATCH_TILE = 64

BF16 = jnp.bfloat16
F32 = jnp.float32
SQRT_2_OVER_PI = np.float32(np.sqrt(2.0 / np.pi))
GELU_CUBIC = np.float32(np.sqrt(2.0 / np.pi) * 0.044715)


def _dot(a, b):
    return jnp.dot(a, b.astype(BF16), preferred_element_type=F32)


def _cast_weights(dst_ref, src_ref):
    if len(src_ref.shape) == 3:
        for blk in range(src_ref.shape[0]):
            dst_ref[blk] = src_ref[blk].astype(BF16)
    else:
        for c0 in range(0, src_ref.shape[1], 2 * V7X_LANES):
            dst_ref[:, c0:c0 + 2 * V7X_LANES] = src_ref[:, c0:c0 + 2 * V7X_LANES].astype(BF16)


def _rms(x, g):
    return x * lax.rsqrt(jnp.mean(x * x, axis=-1, keepdims=True) + NORM_EPS) * g


def _gelu(x):
    inner = x * (SQRT_2_OVER_PI + GELU_CUBIC * (x * x))
    return x * (0.5 + 0.5 * jnp.tanh(inner))


def _sigmoid(x):
    return 1.0 / (1.0 + jnp.exp(-x))


def _silu(x):
    return x * _sigmoid(x)


def _softplus(x):
    return jnp.maximum(x, 0.0) + jnp.log1p(jnp.exp(-jnp.abs(x)))


def _const_spec(shape):
    zeros = (0,) * len(shape)
    return pl.BlockSpec(shape, lambda i: zeros, pipeline_mode=pl.Buffered(1))


def _params():
    return pltpu.CompilerParams(dimension_semantics=("arbitrary",),
                                vmem_limit_bytes=VMEM_LIMIT_BYTES)


def _layer_norm_a(v, lng_ref, lnb_ref):
    d = v - jnp.mean(v, axis=-1, keepdims=True)
    var = jnp.mean(d * d, axis=-1, keepdims=True)
    return d * lax.rsqrt(var + LN_EPS) * lng_ref[...] + lnb_ref[...]


def _mixer_a_sample_kernel(x_ref, npre_ref, npost_ref, win_ref, lng_ref, lnb_ref, ws_ref, bst_ref,
                           wout_ref, o_ref, v_ref):
    x = x_ref[...]
    h = _rms(x, npre_ref[...]).astype(BF16)
    vn = _layer_norm_a(_gelu(_dot(h, win_ref[:, 2 * D_A:])), lng_ref, lnb_ref)
    v_ref[...] = vn
    acc = jnp.zeros(x.shape, F32)
    for g in range(A_GROUPS):
        c0 = g * A_GROUP_W
        u = _gelu(_dot(h, win_ref[:, 2 * c0:2 * c0 + A_GROUP_W]))
        gate = _silu(_dot(h, win_ref[:, 2 * c0 + A_GROUP_W:2 * c0 + 2 * A_GROUP_W]))
        mixed = ws_ref[g][0:1, 0:1] * vn[:, c0:c0 + A_GROUP_W] + bst_ref[0:1, g:g + 1]
        acc = acc + _dot((u * mixed * gate).astype(BF16), wout_ref[c0:c0 + A_GROUP_W, :])
    o_ref[...] = x + _rms(acc, npost_ref[...])


def _mixer_a_prompt_kernel(x_ref, npre_ref, npost_ref, win_ref, lng_ref, lnb_ref, ws_ref, bst_ref,
                           wout_ref, o_ref, *, tm, sub):
    x = x_ref[...]
    h = _rms(x, npre_ref[...]).astype(BF16)
    row = lax.broadcasted_iota(jnp.int32, (CHUNK, CHUNK), 0)
    col = lax.broadcasted_iota(jnp.int32, (CHUNK, CHUNK), 1)
    causal = row >= col
    n_sub = tm // sub
    items = [(r, g) for r in range(n_sub) for g in range(A_GROUPS)]
    v_parts = {r: [] for r in range(n_sub)}
    vnb, acc, ws_masked = {}, {}, {}

    def sub_rows(r):
        return slice(r * sub, (r + 1) * sub)

    def v_part(r, c):
        c0 = 2 * D_A + c * A_GROUP_W
        v_parts[r].append(_gelu(_dot(h[sub_rows(r)], win_ref[:, c0:c0 + A_GROUP_W])))

    def v_finish(r):
        v = jnp.concatenate(v_parts.pop(r), axis=1)
        vnb[r] = _layer_norm_a(v, lng_ref, lnb_ref).astype(BF16)

    def front(r, g):
        c0 = g * A_GROUP_W
        hr = h[sub_rows(r)]
        ug = _dot(hr, win_ref[:, 2 * c0:2 * c0 + 2 * A_GROUP_W])
        u, gate = ug[:, :A_GROUP_W], ug[:, A_GROUP_W:]
        if g not in ws_masked:
            ws_masked[g] = jnp.where(causal, ws_ref[g], 0.0).astype(BF16)
        mixed = jnp.concatenate(
            [_dot(ws_masked[g], vnb[r][c * CHUNK:(c + 1) * CHUNK, c0:c0 + A_GROUP_W])
             for c in range(sub // CHUNK)], axis=0)
        return u, gate, mixed

    def back(r, g, u, gate, mixed):
        c0 = g * A_GROUP_W
        bias = jnp.concatenate([bst_ref[:, g:g + 1]] * (sub // CHUNK), axis=0)
        z = (_gelu(u) * (mixed + bias) * _silu(gate)).astype(BF16)
        y = _dot(z, wout_ref[c0:c0 + A_GROUP_W, :])
        acc[r] = y if g == 0 else acc[r] + y
        if g == A_GROUPS - 1:
            rows = sub_rows(r)
            o_ref[rows, :] = x[rows] + _rms(acc.pop(r), npost_ref[...])

    for c in range(A_GROUPS):
        v_part(0, c)
    v_finish(0)
    nxt = front(*items[0])
    for k, (r, g) in enumerate(items):
        cur = nxt
        if r + 1 < n_sub:
            v_part(r + 1, g)
            if g == A_GROUPS - 1:
                v_finish(r + 1)
        if k + 1 < len(items):
            nxt = front(*items[k + 1])
        back(r, g, *cur)


def _a_win_col(c):
    part, off = divmod(c, D_A)
    g, within = divmod(off, A_GROUP_W)
    if part == 1:
        return 2 * D_A + off
    return g * 2 * A_GROUP_W + (A_GROUP_W if part == 2 else 0) + within


def _stage_weights(hbm_ref, layer, dst_ref, stage_ref, sem_ref, dst_col=lambda c: c):
    rows, cols = dst_ref.shape
    n_slots, piece_rows, piece_cols = stage_ref.shape
    ahead = n_slots - 1
    pieces = [(r0, c0) for r0 in range(0, rows, piece_rows) for c0 in range(0, cols, piece_cols)]

    def copy(k):
        r0, c0 = pieces[k]
        src = hbm_ref.at[layer, pl.ds(r0, piece_rows), pl.ds(c0, piece_cols)]
        return pltpu.make_async_copy(src, stage_ref.at[k % n_slots], sem_ref.at[k % n_slots])

    for k in range(min(ahead, len(pieces))):
        copy(k).start()
    for k, (r0, c0) in enumerate(pieces):
        if k + ahead < len(pieces):
            copy(k + ahead).start()
        copy(k).wait()
        d0 = dst_col(c0)
        dst_ref[r0:r0 + piece_rows, d0:d0 + piece_cols] = stage_ref[k % n_slots].astype(BF16)


def _mixer_a_kernel(x_ref, xs_ref, npre_ref, npost_ref, win_hbm, lng_ref, lnb_ref, ws_ref, bst_ref,
                    wout_hbm, o_ref, os_ref, vs_ref, win_ref, wout_ref, stage_ref, sem_ref,
                    *, layer, tm, sub, n_tiles):
    step = pl.program_id(0)

    @pl.when(step == 0)
    def _():
        _stage_weights(win_hbm, layer, win_ref, stage_ref, sem_ref, dst_col=_a_win_col)
        _stage_weights(wout_hbm, layer, wout_ref, stage_ref, sem_ref)

    @pl.when(step < n_tiles)
    def _():
        _mixer_a_prompt_kernel(x_ref, npre_ref, npost_ref, win_ref, lng_ref, lnb_ref, ws_ref,
                               bst_ref, wout_ref, o_ref, tm=tm, sub=sub)

    @pl.when(step == n_tiles)
    def _():
        _mixer_a_sample_kernel(xs_ref, npre_ref, npost_ref, win_ref, lng_ref, lnb_ref, ws_ref,
                               bst_ref, wout_ref, os_ref, vs_ref)


def _mixer_a(x, xs, npre, npost, win, lng, lnb, ws, bst, wout, *, layer, tm):
    n, ns = x.shape[0], xs.shape[0]
    n_tiles = n // tm
    row_spec = pl.BlockSpec((tm, D_MODEL), lambda i: (jnp.minimum(i, n_tiles - 1), 0))
    hbm_spec = pl.BlockSpec(memory_space=pl.ANY)
    return pl.pallas_call(
        functools.partial(_mixer_a_kernel, layer=layer, tm=tm, sub=A_SUB_ROWS, n_tiles=n_tiles),
        grid=(n_tiles + 1,),
        in_specs=[row_spec, _const_spec((ns, D_MODEL)), _const_spec((1, D_MODEL)),
                  _const_spec((1, D_MODEL)), hbm_spec, _const_spec((1, D_A)),
                  _const_spec((1, D_A)), _const_spec((A_GROUPS, CHUNK, CHUNK)),
                  _const_spec((CHUNK, A_GROUPS)), hbm_spec],
        out_specs=[row_spec, pl.BlockSpec((ns, D_MODEL), lambda i: (0, 0)),
                   pl.BlockSpec((ns, D_A), lambda i: (0, 0))],
        out_shape=[jax.ShapeDtypeStruct((n, D_MODEL), F32), jax.ShapeDtypeStruct((ns, D_MODEL), F32),
                   jax.ShapeDtypeStruct((ns, D_A), F32)],
        scratch_shapes=[pltpu.VMEM((D_MODEL, 3 * D_A), BF16), pltpu.VMEM((D_A, D_MODEL), BF16),
                        pltpu.VMEM((A_STAGE_SLOTS, A_STAGE_ROWS, A_STAGE_COLS), F32),
                        pltpu.SemaphoreType.DMA((A_STAGE_SLOTS,))],
        compiler_params=_params(),
        name="mixer_a",
    )(x, xs, npre, npost, win, lng, lnb, ws, bst, wout)


def _rope_tables(positions):
    half = HEAD_DIM // 2
    inv_freq = ROPE_THETA ** (-jnp.arange(half, dtype=F32) / half)
    ang = positions.astype(F32)[:, None] * inv_freq[None, :]
    cos, sin = jnp.cos(ang), jnp.sin(ang)
    reps = V7X_LANES // HEAD_DIM
    cos_t = jnp.tile(jnp.concatenate([cos, cos], axis=-1), (1, reps))
    sin_t = jnp.tile(jnp.concatenate([-sin, sin], axis=-1), (1, reps))
    return cos_t, sin_t


def _rope(x, cos, sin_signed):
    lane = lax.broadcasted_iota(jnp.int32, x.shape, 1)
    first_half = (lane % HEAD_DIM) < (HEAD_DIM // 2)
    rot = jnp.where(first_half, pltpu.roll(x, V7X_LANES - HEAD_DIM // 2, 1),
                    pltpu.roll(x, HEAD_DIM // 2, 1))
    return x * cos + rot * sin_signed


def _dup_kv_halves(x, lane):
    swapped = pltpu.roll(x, HEAD_DIM, 1)
    low = lane < HEAD_DIM
    return jnp.where(low, x, swapped), jnp.where(low, swapped, x)


def _mixer_b_prompt_kernel(x_ref, npre_ref, npost_ref, win_ref, sinks_ref, cos_ref, sin_ref,
                           wout_ref, o_ref, kout_ref, vout_ref,
                           k0_ref, k1_ref, vt0_ref, vt1_ref, att_ref, win_bf_ref, wout_bf_ref,
                           *, tm, tiles_per_seq):
    @pl.when(pl.program_id(0) == 0)
    def _():
        _cast_weights(win_bf_ref, win_ref)
        _cast_weights(wout_bf_ref, wout_ref)

    win_ref, wout_ref = win_bf_ref, wout_bf_ref
    first = (pl.program_id(0) % tiles_per_seq) == 0

    @pl.when(first)
    def _():
        for ref in (k0_ref, k1_ref):
            ref[0:WINDOW, :] = jnp.zeros((WINDOW, V7X_LANES), BF16)
        for ref in (vt0_ref, vt1_ref):
            ref[:, 0:WINDOW] = jnp.zeros((V7X_LANES, WINDOW), BF16)

    x = x_ref[...]
    h = _rms(x, npre_ref[...]).astype(BF16)
    lane = lax.broadcasted_iota(jnp.int32, (tm, V7X_LANES), 1)
    cos, sin = cos_ref[...], sin_ref[...]

    k = _rope(_dot(h, win_ref[:, QD:QD + KD]), cos, sin)
    v = _dot(h, win_ref[:, QD + KD:QD + 2 * KD])
    kout_ref[...] = k[tm - WINDOW:, :]
    vout_ref[...] = v[tm - WINDOW:, :]
    kd0, kd1 = _dup_kv_halves(k, lane)
    k0_ref[WINDOW:, :] = kd0.astype(BF16)
    k1_ref[WINDOW:, :] = kd1.astype(BF16)
    vt = v.T
    vt0_ref[:, WINDOW:] = jnp.concatenate([vt[:HEAD_DIM], vt[:HEAD_DIM]], axis=0).astype(BF16)
    vt1_ref[:, WINDOW:] = jnp.concatenate([vt[HEAD_DIM:], vt[HEAD_DIM:]], axis=0).astype(BF16)

    ci = lax.broadcasted_iota(jnp.int32, (2 * WINDOW, 2 * WINDOW), 0)
    qi = lax.broadcasted_iota(jnp.int32, (2 * WINDOW, 2 * WINDOW), 1) % WINDOW
    band = (ci >= qi) & (ci <= qi + WINDOW)
    bias = jnp.where(band, 0.0, -jnp.inf)
    bias_first = jnp.where(band & (ci >= jnp.where(first, WINDOW, 0)), 0.0, -jnp.inf)
    head_lane = lax.broadcasted_iota(jnp.int32, (1, 2 * WINDOW), 1) < WINDOW
    low_lanes = lax.broadcasted_iota(jnp.int32, (WINDOW, V7X_LANES), 1) < HEAD_DIM
    top_rows = lax.broadcasted_iota(jnp.int32, (V7X_LANES, WINDOW), 0) < HEAD_DIM
    scale = HEAD_DIM ** -0.5

    n_blocks = tm // WINDOW
    n_pairs = N_HEADS // 2
    chunk = 2 * V7X_LANES
    n_chunks = QD // chunk
    bpg = B_PROJ_ROWS // WINDOW
    n_groups = tm // B_PROJ_ROWS
    items = [(j, p) for j in range(n_blocks) for p in range(n_pairs)]
    q_chunks, gate_chunks, gated, y_chunks = {}, {}, {}, {}

    def block_rows(j):
        return slice(j * WINDOW, (j + 1) * WINDOW)

    def group_rows(g):
        return slice(g * B_PROJ_ROWS, (g + 1) * B_PROJ_ROWS)

    def project_q(g, c):
        q_chunks[g, c] = _dot(h[group_rows(g)], win_ref[:, c * chunk:(c + 1) * chunk])

    def project_gate(g, c):
        c0 = QD + 2 * KD + c * chunk
        gate_chunks[g, c] = _silu(_dot(h[group_rows(g)], win_ref[:, c0:c0 + chunk]))

    def project_out(g, c):
        if c == 0:
            gated[g] = jnp.concatenate(
                [att_ref[group_rows(g), cc * chunk:(cc + 1) * chunk] * gate_chunks.pop((g, cc))
                 for cc in range(n_chunks)], axis=1).astype(BF16)
        y_chunks[g, c] = _dot(gated[g], wout_ref[:, c * chunk:(c + 1) * chunk])
        if c == n_chunks - 1:
            y = jnp.concatenate([y_chunks.pop((g, cc)) for cc in range(n_chunks)], axis=1)
            rows = group_rows(g)
            o_ref[rows, :] = x[rows] + _rms(y, npost_ref[...])

    def scores(j, p):
        rows = block_rows(j)
        kref = k0_ref if (2 * p) // Q_PER_KV == 0 else k1_ref
        half = (p % 2) * V7X_LANES
        r0 = (j % bpg) * WINDOW
        qp = q_chunks[j // bpg, p // 2][r0:r0 + WINDOW, half:half + V7X_LANES]
        qp = _rope(qp, cos[rows], sin[rows]) * scale
        qs = jnp.concatenate([jnp.where(low_lanes, qp, 0.0), jnp.where(low_lanes, 0.0, qp)],
                             axis=0).astype(BF16)
        st = lax.dot_general(kref[j * WINDOW:(j + 2) * WINDOW, :], qs, (((1,), (1,)), ((), ())),
                             preferred_element_type=F32)
        return st + (bias_first if j == 0 else bias)

    def attend(j, p, st):
        vtref = vt0_ref if (2 * p) // Q_PER_KV == 0 else vt1_ref
        sink = jnp.where(head_lane, sinks_ref[0, 2 * p], sinks_ref[0, 2 * p + 1])
        m = jnp.maximum(jnp.max(st, axis=0, keepdims=True), sink)
        e = jnp.exp(st - m)
        inv = 1.0 / (jnp.sum(e, axis=0, keepdims=True) + jnp.exp(sink - m))
        return _dot(vtref[:, j * WINDOW:(j + 2) * WINDOW], e.astype(BF16)), inv

    def finish(j, p, ot, inv):
        ot = ot * inv
        pair = jnp.where(top_rows, ot[:, :WINDOW], ot[:, WINDOW:])
        att_ref[block_rows(j), p * V7X_LANES:(p + 1) * V7X_LANES] = pair.T

    items_per_group = bpg * n_pairs
    side = {}
    for g in range(n_groups):
        tasks = []
        for c in range(n_chunks):
            if g + 1 < n_groups:
                tasks.append((project_q, g + 1, c))
            tasks.append((project_gate, g, c))
            if g > 0:
                tasks.append((project_out, g - 1, c))
        for t, task in enumerate(tasks):
            at = g * items_per_group + (t * items_per_group) // len(tasks)
            side.setdefault(at, []).append(task)

    for c in range(n_chunks):
        project_q(0, c)
    st_next = scores(*items[0])
    pending = None
    for i, (j, p) in enumerate(items):
        st = st_next
        for fn, g, c in side.get(i, []):
            if fn is not project_out:
                fn(g, c)
        if i + 1 < len(items):
            st_next = scores(*items[i + 1])
        ot, inv = attend(j, p, st)
        if pending is not None:
            finish(*pending)
        pending = (j, p, ot, inv)
        for fn, g, c in side.get(i, []):
            if fn is project_out:
                fn(g, c)
    finish(*pending)
    for c in range(n_chunks):
        project_out(n_groups - 1, c)

    for ref in (k0_ref, k1_ref):
        ref[0:WINDOW, :] = ref[tm:tm + WINDOW, :]
    for ref in (vt0_ref, vt1_ref):
        ref[:, 0:WINDOW] = ref[:, tm:tm + WINDOW]


def _mixer_b_prompt(x, npre, npost, win, sinks, cos, sin, wout, *, tm, seq):
    n = x.shape[0]
    tiles_per_seq = seq // tm
    n_seq = n // seq
    row_spec = pl.BlockSpec((tm, D_MODEL), lambda i: (i, 0))
    rope_spec = pl.BlockSpec((tm, V7X_LANES), lambda i: (i % tiles_per_seq, 0))
    kv_spec = pl.BlockSpec((WINDOW, KD), lambda i: (i // tiles_per_seq, 0))
    return pl.pallas_call(
        functools.partial(_mixer_b_prompt_kernel, tm=tm, tiles_per_seq=tiles_per_seq),
        grid=(n // tm,),
        in_specs=[row_spec, _const_spec((1, D_MODEL)), _const_spec((1, D_MODEL)),
                  _const_spec((D_MODEL, 2 * QD + 2 * KD)),
                  pl.BlockSpec(memory_space=pltpu.SMEM),
                  rope_spec, rope_spec, _const_spec((QD, D_MODEL))],
        out_specs=[row_spec, kv_spec, kv_spec],
        out_shape=[jax.ShapeDtypeStruct((n, D_MODEL), F32),
                   jax.ShapeDtypeStruct((n_seq * WINDOW, KD), F32),
                   jax.ShapeDtypeStruct((n_seq * WINDOW, KD), F32)],
        scratch_shapes=[pltpu.VMEM((tm + WINDOW, V7X_LANES), BF16)] * 2
        + [pltpu.VMEM((V7X_LANES, tm + WINDOW), BF16)] * 2 + [pltpu.VMEM((tm, QD), F32)]
        + [pltpu.VMEM(win.shape, BF16), pltpu.VMEM(wout.shape, BF16)],
        compiler_params=_params(),
        name="mixer_b_prompt",
    )(x, npre, npost, win, sinks, cos, sin, wout)


def _mixer_b_sample_proj_kernel(x_ref, npre_ref, win_ref, cos_ref, sin_ref,
                                q_ref, k_ref, v_ref, kt_ref, vt_ref, gate_ref):
    n = x_ref.shape[0]
    h = _rms(x_ref[...], npre_ref[...]).astype(BF16)
    cos, sin = cos_ref[...], sin_ref[...]
    scale = HEAD_DIM ** -0.5
    low_lanes = lax.broadcasted_iota(jnp.int32, (n, V7X_LANES), 1) < HEAD_DIM
    for p in range(N_HEADS // 2):
        qp = _rope(_dot(h, win_ref[:, p * V7X_LANES:(p + 1) * V7X_LANES]), cos, sin) * scale
        q_ref[pl.ds(2 * p, n, stride=N_HEADS), :] = jnp.where(low_lanes, qp, 0.0)
        q_ref[pl.ds(2 * p + 1, n, stride=N_HEADS), :] = jnp.where(
            low_lanes, pltpu.roll(qp, HEAD_DIM, 1), 0.0)
    k = _rope(_dot(h, win_ref[:, QD:QD + KD]), cos, sin)
    v = _dot(h, win_ref[:, QD + KD:QD + 2 * KD])
    k_ref[...] = k
    v_ref[...] = v
    kt_ref[...] = k.T
    vt_ref[...] = v.T
    gate_ref[...] = _silu(_dot(h, win_ref[:, QD + 2 * KD:]))


def _mixer_b_sample_proj(x, npre, win, cos_row, sin_row):
    n = x.shape[0]
    return pl.pallas_call(
        _mixer_b_sample_proj_kernel,
        out_shape=[jax.ShapeDtypeStruct((n * N_HEADS, V7X_LANES), F32),
                   jax.ShapeDtypeStruct((n, KD), F32), jax.ShapeDtypeStruct((n, KD), F32),
                   jax.ShapeDtypeStruct((KD, n), F32), jax.ShapeDtypeStruct((KD, n), F32),
                   jax.ShapeDtypeStruct((n, QD), F32)],
        compiler_params=pltpu.CompilerParams(vmem_limit_bytes=VMEM_LIMIT_BYTES),
        name="mixer_b_sample_proj",
    )(x, npre, win, cos_row, sin_row)


def _mixer_b_sample_attn_kernel(q_ref, kt_ref, vt_ref, kn_ref, vn_ref, knt_ref, vnt_ref, sinks_ref,
                                o_ref, kto_ref, vto_ref, *, bt):
    w = kt_ref.shape[3]
    n = bt * N_KV_HEADS
    kt = kt_ref[...].reshape(n, HEAD_DIM, w)
    vt = vt_ref[...].reshape(n, HEAD_DIM, w)
    q = q_ref[...].reshape(bt, N_HEADS, V7X_LANES)
    kn, vn = kn_ref[...], vn_ref[...]

    qb = q.astype(BF16)
    q_kv = qb.reshape(n, Q_PER_KV, V7X_LANES)[:, :, :HEAD_DIM]
    s_c = jnp.einsum("nhd,nds->nhs", q_kv, kt.astype(BF16),
                     preferred_element_type=F32).reshape(bt, N_HEADS, w)
    first_kv = lax.broadcasted_iota(jnp.int32, (bt, N_HEADS, V7X_LANES), 1) < Q_PER_KV
    kn_b = kn.astype(BF16).astype(F32)
    vn_b = vn.astype(BF16).astype(F32)
    k_sel = jnp.where(first_kv, kn_b[:, None, :], pltpu.roll(kn_b, HEAD_DIM, 1)[:, None, :])
    v_sel = jnp.where(first_kv, vn_b[:, None, :], pltpu.roll(vn_b, HEAD_DIM, 1)[:, None, :])
    s_n = jnp.sum(qb.astype(F32) * k_sel, axis=-1, keepdims=True)
    sink = sinks_ref[...][None]
    m = jnp.maximum(jnp.maximum(jnp.max(s_c, axis=-1, keepdims=True), s_n), sink)
    e_c = jnp.exp(s_c - m)
    e_n = jnp.exp(s_n - m)
    inv = 1.0 / (jnp.sum(e_c, axis=-1, keepdims=True) + e_n + jnp.exp(sink - m))
    p_c = (e_c * inv).astype(BF16).reshape(n, Q_PER_KV, w)
    o = jnp.einsum("nhs,nds->nhd", p_c, vt.astype(BF16),
                   preferred_element_type=F32).reshape(bt, N_HEADS, HEAD_DIM)
    p_n = (e_n * inv).astype(BF16).astype(F32)
    o = o + p_n * v_sel[:, :, :HEAD_DIM]
    o_ref[...] = jnp.zeros(o_ref.shape, F32)
    o_ref[:, 0:HEAD_DIM] = o.reshape(bt * N_HEADS, HEAD_DIM)

    is_first = pl.program_id(0) == 0
    knt = jnp.where(is_first, knt_ref[:, 0:bt], knt_ref[:, bt:])
    vnt = jnp.where(is_first, vnt_ref[:, 0:bt], vnt_ref[:, bt:])
    last = lax.broadcasted_iota(jnp.int32, (KD, w), 1) == w - 1
    for b in range(bt):
        for src, cols, dst in ((kt_ref, knt, kto_ref), (vt_ref, vnt, vto_ref)):
            old = src[b].reshape(KD, w)
            new = jnp.where(last, jnp.broadcast_to(cols[:, b:b + 1], (KD, w)),
                            pltpu.roll(old, w - 1, 1))
            dst[b] = new.reshape(N_KV_HEADS, HEAD_DIM, w)


def _mixer_b_sample_attn(q2, kt, vt, kn, vn, knt, vnt, sinks_col, *, bt):
    b, _, _, w = kt.shape
    assert b == 2 * bt
    q_spec = pl.BlockSpec((bt * N_HEADS, V7X_LANES), lambda i: (i, 0))
    c_spec = pl.BlockSpec((bt, N_KV_HEADS, HEAD_DIM, w), lambda i: (i, 0, 0, 0))
    n_spec = pl.BlockSpec((bt, KD), lambda i: (i, 0))
    t_spec = pl.BlockSpec((KD, b), lambda i: (0, 0))
    return pl.pallas_call(
        functools.partial(_mixer_b_sample_attn_kernel, bt=bt),
        grid=(b // bt,),
        in_specs=[q_spec, c_spec, c_spec, n_spec, n_spec, t_spec, t_spec,
                  pl.BlockSpec((N_HEADS, 1), lambda i: (0, 0))],
        out_specs=[q_spec, c_spec, c_spec],
        out_shape=[jax.ShapeDtypeStruct((b * N_HEADS, V7X_LANES), F32),
                   jax.ShapeDtypeStruct(kt.shape, F32), jax.ShapeDtypeStruct(vt.shape, F32)],
        compiler_params=_params(),
        name="mixer_b_sample_attn",
    )(q2, kt, vt, kn, vn, knt, vnt, sinks_col)


def _mixer_b_sample_out_kernel(x_ref, att_ref, gate_ref, wout_ref, npost_ref, o_ref):
    n = x_ref.shape[0]
    low_lanes = lax.broadcasted_iota(jnp.int32, (n, V7X_LANES), 1) < HEAD_DIM
    pairs = []
    for p in range(N_HEADS // 2):
        even = att_ref[pl.ds(2 * p, n, stride=N_HEADS), :]
        odd = att_ref[pl.ds(2 * p + 1, n, stride=N_HEADS), :]
        pairs.append(jnp.where(low_lanes, even, pltpu.roll(odd, HEAD_DIM, 1)))
    att = jnp.concatenate(pairs, axis=1)
    y = _dot((att * gate_ref[...]).astype(BF16), wout_ref[...])
    o_ref[...] = x_ref[...] + _rms(y, npost_ref[...])


def _mixer_b_sample_out(x, att, gate, wout, npost):
    return pl.pallas_call(
        _mixer_b_sample_out_kernel,
        out_shape=jax.ShapeDtypeStruct(x.shape, F32),
        compiler_params=pltpu.CompilerParams(vmem_limit_bytes=VMEM_LIMIT_BYTES),
        name="mixer_b_sample_out",
    )(x, att, gate, wout, npost)


def _rglru_gates(xc, wa_ref, ba_ref, wx_ref, bx_ref, lam_ref):
    xcb = xc.astype(BF16)
    ra, rx = [], []
    for blk in range(C_BLOCKS):
        cols = slice(blk * C_BLOCK_W, (blk + 1) * C_BLOCK_W)
        ra.append(_dot(xcb[:, cols], wa_ref[blk]))
        rx.append(_dot(xcb[:, cols], wx_ref[blk]))
    r = _sigmoid(jnp.concatenate(ra, axis=-1) + ba_ref[...])
    i_gate = _sigmoid(jnp.concatenate(rx, axis=-1) + bx_ref[...])
    log_a = r * (-LRU_C * _softplus(-lam_ref[...]))
    a = jnp.exp(log_a)
    mult = jnp.sqrt(-jnp.tanh(log_a) * (a * a + 1.0))
    return a, mult, i_gate * xc


def _mixer_c_prompt_kernel(x_ref, npre_ref, npost_ref, win_ref, cw_ref, cb_ref, wa_ref, ba_ref,
                           wx_ref, bx_ref, lam_ref, wout_ref, o_ref, conv_ref, hlast_ref,
                           xpad_ref, h_ref, win_bf_ref, wa_bf_ref, wx_bf_ref, wout_bf_ref,
                           *, tm, tiles_per_seq):
    @pl.when(pl.program_id(0) == 0)
    def _():
        for dst, src in ((win_bf_ref, win_ref), (wa_bf_ref, wa_ref), (wx_bf_ref, wx_ref),
                         (wout_bf_ref, wout_ref)):
            _cast_weights(dst, src)

    win_ref, wa_ref, wx_ref, wout_ref = win_bf_ref, wa_bf_ref, wx_bf_ref, wout_bf_ref
    first = (pl.program_id(0) % tiles_per_seq) == 0

    @pl.when(first)
    def _():
        xpad_ref[0:V7X_SUBLANES, :] = jnp.zeros((V7X_SUBLANES, D_RNN), F32)
        h_ref[...] = jnp.zeros((V7X_SUBLANES, D_RNN), F32)

    x = x_ref[...]
    h = _rms(x, npre_ref[...]).astype(BF16)
    xr = _dot(h, win_ref[:, :D_RNN])
    xpad_ref[V7X_SUBLANES:, :] = xr
    conv_ref[...] = xr[tm - V7X_SUBLANES:, :]
    cw = cw_ref[...]
    xc = cb_ref[...] + xr * cw[CONV_W - 1:CONV_W, :]
    for tap in range(CONV_W - 1):
        back = CONV_W - 1 - tap
        xc = xc + xpad_ref[V7X_SUBLANES - back:V7X_SUBLANES - back + tm, :] * cw[tap:tap + 1, :]
    xpad_ref[0:V7X_SUBLANES, :] = xr[tm - V7X_SUBLANES:, :]

    a, mult, gx = _rglru_gates(xc, wa_ref, ba_ref, wx_ref, bx_ref, lam_ref)
    row = lax.broadcasted_iota(jnp.int32, (tm, 1), 0)
    mult = jnp.where(first & (row == 0), 1.0, mult)
    b = mult * gx

    n_groups = tm // V7X_SUBLANES
    a = a.reshape(n_groups, V7X_SUBLANES, D_RNN)
    b = b.reshape(n_groups, V7X_SUBLANES, D_RNN)
    sub = lax.broadcasted_iota(jnp.int32, (n_groups, V7X_SUBLANES, D_RNN), 1)
    for dist in (1, 2, 4):
        keep = sub >= dist
        a_prev = pltpu.roll(a, dist, 1)
        b_prev = pltpu.roll(b, dist, 1)
        b = jnp.where(keep, a * b_prev + b, b)
        a = jnp.where(keep, a * a_prev, a)
    carry = h_ref[V7X_SUBLANES - 1:V7X_SUBLANES, :]
    groups = []
    for gi in range(n_groups):
        hs_g = a[gi] * carry + b[gi]
        carry = hs_g[V7X_SUBLANES - 1:V7X_SUBLANES, :]
        groups.append(hs_g)
    h_ref[...] = groups[-1]
    hlast_ref[...] = groups[-1]
    hs = jnp.concatenate(groups, axis=0)

    gate = _silu(_dot(h, win_ref[:, D_RNN:]))
    y = _dot((hs * gate).astype(BF16), wout_ref[...])
    o_ref[...] = x + _rms(y, npost_ref[...])


def _mixer_c_prompt(x, npre, npost, win, cw, cb, wa, ba, wx, bx, lam, wout, *, tm, seq):
    n = x.shape[0]
    tiles_per_seq = seq // tm
    n_seq = n // seq
    row_spec = pl.BlockSpec((tm, D_MODEL), lambda i: (i, 0))
    tail_spec = pl.BlockSpec((V7X_SUBLANES, D_RNN), lambda i: (i // tiles_per_seq, 0))
    vec = _const_spec((1, D_RNN))
    blk = _const_spec((C_BLOCKS, C_BLOCK_W, C_BLOCK_W))
    return pl.pallas_call(
        functools.partial(_mixer_c_prompt_kernel, tm=tm, tiles_per_seq=tiles_per_seq),
        grid=(n // tm,),
        in_specs=[row_spec, vec, vec, _const_spec((D_MODEL, 2 * D_RNN)),
                  _const_spec((CONV_W, D_RNN)), vec, blk, vec, blk, vec, vec,
                  _const_spec((D_RNN, D_MODEL))],
        out_specs=[row_spec, tail_spec, tail_spec],
        out_shape=[jax.ShapeDtypeStruct((n, D_MODEL), F32),
                   jax.ShapeDtypeStruct((n_seq * V7X_SUBLANES, D_RNN), F32),
                   jax.ShapeDtypeStruct((n_seq * V7X_SUBLANES, D_RNN), F32)],
        scratch_shapes=[pltpu.VMEM((tm + V7X_SUBLANES, D_RNN), F32),
                        pltpu.VMEM((V7X_SUBLANES, D_RNN), F32)]
        + [pltpu.VMEM(w.shape, BF16) for w in (win, wa, wx, wout)],
        compiler_params=_params(),
        name="mixer_c_prompt",
    )(x, npre, npost, win, cw, cb, wa, ba, wx, bx, lam, wout)


def _mixer_c_sample_kernel(x_ref, npre_ref, npost_ref, win_ref, cw_ref, cb_ref, wa_ref, ba_ref,
                           wx_ref, bx_ref, lam_ref, wout_ref, conv_ref, h0_ref,
                           o_ref, conv_out_ref, h_out_ref):
    x = x_ref[...]
    h = _rms(x, npre_ref[...]).astype(BF16)
    xr = _dot(h, win_ref[:, :D_RNN])
    cw = cw_ref[...]
    xc = cb_ref[...] + xr * cw[CONV_W - 1:CONV_W, :]
    for tap in range(CONV_W - 1):
        xc = xc + conv_ref[tap] * cw[tap:tap + 1, :]
        if tap > 0:
            conv_out_ref[tap - 1] = conv_ref[tap]
    conv_out_ref[CONV_W - 2] = xr
    a, mult, gx = _rglru_gates(xc, wa_ref, ba_ref, wx_ref, bx_ref, lam_ref)
    hs = a * h0_ref[...] + mult * gx
    h_out_ref[...] = hs
    gate = _silu(_dot(h, win_ref[:, D_RNN:]))
    y = _dot((hs * gate).astype(BF16), wout_ref[...])
    o_ref[...] = x + _rms(y, npost_ref[...])


def _mixer_c_sample(x, npre, npost, win, cw, cb, wa, ba, wx, bx, lam, wout, conv_t, h0):
    n = x.shape[0]
    return pl.pallas_call(
        _mixer_c_sample_kernel,
        out_shape=[jax.ShapeDtypeStruct((n, D_MODEL), F32),
                   jax.ShapeDtypeStruct((CONV_W - 1, n, D_RNN), F32),
                   jax.ShapeDtypeStruct((n, D_RNN), F32)],
        compiler_params=pltpu.CompilerParams(vmem_limit_bytes=VMEM_LIMIT_BYTES),
        name="mixer_c_sample",
    )(x, npre, npost, win, cw, cb, wa, ba, wx, bx, lam, wout, conv_t, h0)


def kernel(x_prompt, x_sample, cache_b_k, cache_b_v, state_c_conv, state_c_h, norm_pre, norm_post,
           a_w_in, a_ln_g, a_ln_b, a_w_s, a_b_s, a_w_out, b_w_in, b_sinks, b_w_out, c_w_in,
           c_conv_w, c_conv_b, c_w_a, c_b_a, c_w_x, c_b_x, c_lam, c_w_out):
    batch, seq, _ = x_prompt.shape
    dec_batch, dec_seq, _ = x_sample.shape
    past_len = PAST_LEN
    w_buf = cache_b_k.shape[2]
    assert dec_seq == 1 and w_buf == WINDOW and seq % ROW_TILE == 0
    assert ROW_TILE % B_PROJ_ROWS == 0 and B_PROJ_ROWS % WINDOW == 0
    assert ROW_TILE % A_SUB_ROWS == 0 and A_SUB_ROWS % CHUNK == 0

    xp = x_prompt.reshape(batch * seq, D_MODEL)
    xs = x_sample.reshape(dec_batch, D_MODEL)
    row = lambda a: a.reshape(1, -1)

    cos_p, sin_p = _rope_tables(jnp.arange(seq, dtype=jnp.int32))
    cos_s, sin_s = _rope_tables(past_len + jnp.arange(dec_seq, dtype=jnp.int32))

    a_v_s = []
    b_kp, b_vp, b_ks, b_vs = [], [], [], []
    c_cp, c_hp, c_cs, c_hs = [], [], [], []
    for i in range(DEPTH):
        kind, j = i % N_MIXERS, i // N_MIXERS
        npre, npost = row(norm_pre[i]), row(norm_post[i])
        if kind == 0:
            xp, xs, vs = _mixer_a(xp, xs, npre, npost, a_w_in, row(a_ln_g[j]), row(a_ln_b[j]),
                                  a_w_s[j], a_b_s[j].T, a_w_out, layer=j, tm=ROW_TILE)
            a_v_s.append(vs.reshape(dec_batch, dec_seq, D_A))
        elif kind == 1:
            win, wout = b_w_in[j], b_w_out[j]
            xp, kp, vp = _mixer_b_prompt(xp, npre, npost, win, row(b_sinks[j]), cos_p, sin_p, wout,
                                         tm=ROW_TILE, seq=seq)
            b_kp.append(kp.reshape(batch, WINDOW, N_KV_HEADS, HEAD_DIM))
            b_vp.append(vp.reshape(batch, WINDOW, N_KV_HEADS, HEAD_DIM))

            q2, kn, vn, knt, vnt, gate = _mixer_b_sample_proj(xs, npre, win, cos_s, sin_s)
            to_stored = lambda c: jnp.transpose(c, (0, 2, 3, 1))
            om, kt_new, vt_new = _mixer_b_sample_attn(
                q2, to_stored(cache_b_k[j]), to_stored(cache_b_v[j]), kn, vn, knt, vnt,
                b_sinks[j].reshape(N_HEADS, 1), bt=SAMPLE_BATCH_TILE)
            xs = _mixer_b_sample_out(xs, om, gate, wout, npost)
            b_ks.append(jnp.transpose(kt_new, (0, 3, 1, 2)))
            b_vs.append(jnp.transpose(vt_new, (0, 3, 1, 2)))
        else:
            args = (npre, npost, c_w_in[j], c_conv_w[j], row(c_conv_b[j]),
                    c_w_a[j], row(c_b_a[j]), c_w_x[j], row(c_b_x[j]),
                    row(c_lam[j]), c_w_out[j])
            xp, conv_tail, h_tail = _mixer_c_prompt(xp, *args, tm=ROW_TILE, seq=seq)
            c_cp.append(conv_tail.reshape(batch, V7X_SUBLANES, D_RNN)[:, V7X_SUBLANES - (CONV_W - 1):])
            c_hp.append(h_tail.reshape(batch, V7X_SUBLANES, D_RNN)[:, V7X_SUBLANES - 1])
            xs, conv_new, h_new = _mixer_c_sample(
                xs, *args, jnp.transpose(state_c_conv[j], (1, 0, 2)), state_c_h[j])
            c_cs.append(jnp.transpose(conv_new, (1, 0, 2)))
            c_hs.append(h_new)

    return (xp.reshape(batch, seq, D_MODEL), xs.reshape(dec_batch, dec_seq, D_MODEL),
            jnp.stack(a_v_s), jnp.stack(b_kp), jnp.stack(b_vp), jnp.stack(b_ks), jnp.stack(b_vs),
            jnp.stack(c_cp), jnp.stack(c_hp), jnp.stack(c_cs), jnp.stack(c_hs))
```

```python
import functools

import jax
import jax.numpy as jnp
import numpy as np
from jax import lax
from jax.experimental import pallas as pl
from jax.experimental.pallas import tpu as pltpu

D_MODEL = 1024
DEPTH = 4
N_MIXERS = 3
NORM_EPS = 1e-6
LN_EPS = 1e-5

D_A = 2 * D_MODEL
CHUNK = 128
A_GROUPS = 4
A_GROUP_W = D_A // A_GROUPS

HEAD_DIM = 64
N_HEADS = D_MODEL // HEAD_DIM
N_KV_HEADS = N_HEADS // 8
Q_PER_KV = N_HEADS // N_KV_HEADS
WINDOW = 128
ROPE_THETA = 10000.0
QD = N_HEADS * HEAD_DIM
KD = N_KV_HEADS * HEAD_DIM

D_RNN = D_MODEL
C_BLOCKS = 4
C_BLOCK_W = D_RNN // C_BLOCKS
CONV_W = 4
LRU_C = 8.0

PAST_LEN = 8192

V7X_LANES = 128
V7X_SUBLANES = 8
V7X_VMEM_BYTES = 64 * 1024 * 1024
VMEM_LIMIT_BYTES = V7X_VMEM_BYTES - 8 * 1024 * 1024

ROW_TILE = 1024
A_SUB_ROWS = 512
A_STAGE_SLOTS, A_STAGE_ROWS, A_STAGE_COLS = 4, 1024, 256
B_PROJ_ROWS = 256
SAMPLE_BATCH_TILE = 64

BF16 = jnp.bfloat16
F32 = jnp.float32
SQRT_2_OVER_PI = np.float32(np.sqrt(2.0 / np.pi))
GELU_CUBIC = np.float32(np.sqrt(2.0 / np.pi) * 0.044715)


def _dot(a, b):
    return jnp.dot(a, b.astype(BF16), preferred_element_type=F32)


def _cast_weights(dst_ref, src_ref):
    if len(src_ref.shape) == 3:
        for blk in range(src_ref.shape[0]):
            dst_ref[blk] = src_ref[blk].astype(BF16)
    else:
        for c0 in range(0, src_ref.shape[1], 2 * V7X_LANES):
            dst_ref[:, c0:c0 + 2 * V7X_LANES] = src_ref[:, c0:c0 + 2 * V7X_LANES].astype(BF16)


def _rms(x, g):
    return x * lax.rsqrt(jnp.mean(x * x, axis=-1, keepdims=True) + NORM_EPS) * g


def _gelu(x):
    inner = x * (SQRT_2_OVER_PI + GELU_CUBIC * (x * x))
    return x * (0.5 + 0.5 * jnp.tanh(inner))


def _sigmoid(x):
    return 1.0 / (1.0 + jnp.exp(-x))


def _silu(x):
    return x * _sigmoid(x)


def _softplus(x):
    return jnp.maximum(x, 0.0) + jnp.log1p(jnp.exp(-jnp.abs(x)))


def _const_spec(shape):
    zeros = (0,) * len(shape)
    return pl.BlockSpec(shape, lambda i: zeros, pipeline_mode=pl.Buffered(1))


def _params():
    return pltpu.CompilerParams(dimension_semantics=("arbitrary",),
                                vmem_limit_bytes=VMEM_LIMIT_BYTES)


def _layer_norm_a(v, lng_ref, lnb_ref):
    d = v - jnp.mean(v, axis=-1, keepdims=True)
    var = jnp.mean(d * d, axis=-1, keepdims=True)
    return d * lax.rsqrt(var + LN_EPS) * lng_ref[...] + lnb_ref[...]


def _mixer_a_sample_kernel(x_ref, npre_ref, npost_ref, win_ref, lng_ref, lnb_ref, ws_ref, bst_ref,
                           wout_ref, o_ref, v_ref):
    x = x_ref[...]
    h = _rms(x, npre_ref[...]).astype(BF16)
    vn = _layer_norm_a(_gelu(_dot(h, win_ref[:, 2 * D_A:])), lng_ref, lnb_ref)
    v_ref[...] = vn
    acc = jnp.zeros(x.shape, F32)
    for g in range(A_GROUPS):
        c0 = g * A_GROUP_W
        u = _gelu(_dot(h, win_ref[:, 2 * c0:2 * c0 + A_GROUP_W]))
        gate = _silu(_dot(h, win_ref[:, 2 * c0 + A_GROUP_W:2 * c0 + 2 * A_GROUP_W]))
        mixed = ws_ref[g][0:1, 0:1] * vn[:, c0:c0 + A_GROUP_W] + bst_ref[0:1, g:g + 1]
        acc = acc + _dot((u * mixed * gate).astype(BF16), wout_ref[c0:c0 + A_GROUP_W, :])
    o_ref[...] = x + _rms(acc, npost_ref[...])


def _mixer_a_prompt_kernel(x_ref, npre_ref, npost_ref, win_ref, lng_ref, lnb_ref, ws_ref, bst_ref,
                           wout_ref, o_ref, *, tm, sub):
    x = x_ref[...]
    h = _rms(x, npre_ref[...]).astype(BF16)
    row = lax.broadcasted_iota(jnp.int32, (CHUNK, CHUNK), 0)
    col = lax.broadcasted_iota(jnp.int32, (CHUNK, CHUNK), 1)
    causal = row >= col
    n_sub = tm // sub
    items = [(r, g) for r in range(n_sub) for g in range(A_GROUPS)]
    v_parts = {r: [] for r in range(n_sub)}
    vnb, acc, ws_masked = {}, {}, {}

    def sub_rows(r):
        return slice(r * sub, (r + 1) * sub)

    def v_part(r, c):
        c0 = 2 * D_A + c * A_GROUP_W
        v_parts[r].append(_gelu(_dot(h[sub_rows(r)], win_ref[:, c0:c0 + A_GROUP_W])))

    def v_finish(r):
        v = jnp.concatenate(v_parts.pop(r), axis=1)
        vnb[r] = _layer_norm_a(v, lng_ref, lnb_ref).astype(BF16)

    def front(r, g):
        c0 = g * A_GROUP_W
        hr = h[sub_rows(r)]
        ug = _dot(hr, win_ref[:, 2 * c0:2 * c0 + 2 * A_GROUP_W])
        u, gate = ug[:, :A_GROUP_W], ug[:, A_GROUP_W:]
        if g not in ws_masked:
            ws_masked[g] = jnp.where(causal, ws_ref[g], 0.0).astype(BF16)
        mixed = jnp.concatenate(
            [_dot(ws_masked[g], vnb[r][c * CHUNK:(c + 1) * CHUNK, c0:c0 + A_GROUP_W])
             for c in range(sub // CHUNK)], axis=0)
        return u, gate, mixed

    def back(r, g, u, gate, mixed):
        c0 = g * A_GROUP_W
        bias = jnp.concatenate([bst_ref[:, g:g + 1]] * (sub // CHUNK), axis=0)
        z = (_gelu(u) * (mixed + bias) * _silu(gate)).astype(BF16)
        y = _dot(z, wout_ref[c0:c0 + A_GROUP_W, :])
        acc[r] = y if g == 0 else acc[r] + y
        if g == A_GROUPS - 1:
            rows = sub_rows(r)
            o_ref[rows, :] = x[rows] + _rms(acc.pop(r), npost_ref[...])

    for c in range(A_GROUPS):
        v_part(0, c)
    v_finish(0)
    nxt = front(*items[0])
    for k, (r, g) in enumerate(items):
        cur = nxt
        if r + 1 < n_sub:
            v_part(r + 1, g)
            if g == A_GROUPS - 1:
                v_finish(r + 1)
        if k + 1 < len(items):
            nxt = front(*items[k + 1])
        back(r, g, *cur)


def _a_win_col(c):
    part, off = divmod(c, D_A)
    g, within = divmod(off, A_GROUP_W)
    if part == 1:
        return 2 * D_A + off
    return g * 2 * A_GROUP_W + (A_GROUP_W if part == 2 else 0) + within


def _stage_weights(hbm_ref, layer, dst_ref, stage_ref, sem_ref, dst_col=lambda c: c):
    rows, cols = dst_ref.shape
    n_slots, piece_rows, piece_cols = stage_ref.shape
    ahead = n_slots - 1
    pieces = [(r0, c0) for r0 in range(0, rows, piece_rows) for c0 in range(0, cols, piece_cols)]

    def copy(k):
        r0, c0 = pieces[k]
        src = hbm_ref.at[layer, pl.ds(r0, piece_rows), pl.ds(c0, piece_cols)]
        return pltpu.make_async_copy(src, stage_ref.at[k % n_slots], sem_ref.at[k % n_slots])

    for k in range(min(ahead, len(pieces))):
        copy(k).start()
    for k, (r0, c0) in enumerate(pieces):
        if k + ahead < len(pieces):
            copy(k + ahead).start()
        copy(k).wait()
        d0 = dst_col(c0)
        dst_ref[r0:r0 + piece_rows, d0:d0 + piece_cols] = stage_ref[k % n_slots].astype(BF16)


def _mixer_a_kernel(x_ref, xs_ref, npre_ref, npost_ref, win_hbm, lng_ref, lnb_ref, ws_ref, bst_ref,
                    wout_hbm, o_ref, os_ref, vs_ref, win_ref, wout_ref, stage_ref, sem_ref,
                    *, layer, tm, sub, n_tiles):
    step = pl.program_id(0)

    @pl.when(step == 0)
    def _():
        _stage_weights(win_hbm, layer, win_ref, stage_ref, sem_ref, dst_col=_a_win_col)
        _stage_weights(wout_hbm, layer, wout_ref, stage_ref, sem_ref)

    @pl.when(step < n_tiles)
    def _():
        _mixer_a_prompt_kernel(x_ref, npre_ref, npost_ref, win_ref, lng_ref, lnb_ref, ws_ref,
                               bst_ref, wout_ref, o_ref, tm=tm, sub=sub)

    @pl.when(step == n_tiles)
    def _():
        _mixer_a_sample_kernel(xs_ref, npre_ref, npost_ref, win_ref, lng_ref, lnb_ref, ws_ref,
                               bst_ref, wout_ref, os_ref, vs_ref)


def _mixer_a(x, xs, npre, npost, win, lng, lnb, ws, bst, wout, *, layer, tm):
    n, ns = x.shape[0], xs.shape[0]
    n_tiles = n // tm
    row_spec = pl.BlockSpec((tm, D_MODEL), lambda i: (jnp.minimum(i, n_tiles - 1), 0))
    hbm_spec = pl.BlockSpec(memory_space=pl.ANY)
    return pl.pallas_call(
        functools.partial(_mixer_a_kernel, layer=layer, tm=tm, sub=A_SUB_ROWS, n_tiles=n_tiles),
        grid=(n_tiles + 1,),
        in_specs=[row_spec, _const_spec((ns, D_MODEL)), _const_spec((1, D_MODEL)),
                  _const_spec((1, D_MODEL)), hbm_spec, _const_spec((1, D_A)),
                  _const_spec((1, D_A)), _const_spec((A_GROUPS, CHUNK, CHUNK)),
                  _const_spec((CHUNK, A_GROUPS)), hbm_spec],
        out_specs=[row_spec, pl.BlockSpec((ns, D_MODEL), lambda i: (0, 0)),
                   pl.BlockSpec((ns, D_A), lambda i: (0, 0))],
        out_shape=[jax.ShapeDtypeStruct((n, D_MODEL), F32), jax.ShapeDtypeStruct((ns, D_MODEL), F32),
                   jax.ShapeDtypeStruct((ns, D_A), F32)],
        scratch_shapes=[pltpu.VMEM((D_MODEL, 3 * D_A), BF16), pltpu.VMEM((D_A, D_MODEL), BF16),
                        pltpu.VMEM((A_STAGE_SLOTS, A_STAGE_ROWS, A_STAGE_COLS), F32),
                        pltpu.SemaphoreType.DMA((A_STAGE_SLOTS,))],
        compiler_params=_params(),
        name="mixer_a",
    )(x, xs, npre, npost, win, lng, lnb, ws, bst, wout)


def _rope_tables(positions):
    half = HEAD_DIM // 2
    inv_freq = ROPE_THETA ** (-jnp.arange(half, dtype=F32) / half)
    ang = positions.astype(F32)[:, None] * inv_freq[None, :]
    return jnp.cos(ang), jnp.sin(ang)


def _expand_rope(cos, sin):
    reps = V7X_LANES // HEAD_DIM
    return (jnp.concatenate([cos, cos] * reps, axis=-1),
            jnp.concatenate([-sin, sin] * reps, axis=-1))


def _rope(x, cos, sin_signed):
    lane = lax.broadcasted_iota(jnp.int32, x.shape, 1)
    first_half = (lane % HEAD_DIM) < (HEAD_DIM // 2)
    rot = jnp.where(first_half, pltpu.roll(x, V7X_LANES - HEAD_DIM // 2, 1),
                    pltpu.roll(x, HEAD_DIM // 2, 1))
    return x * cos + rot * sin_signed


def _dup_kv_halves(x, lane):
    swapped = pltpu.roll(x, HEAD_DIM, 1)
    low = lane < HEAD_DIM
    return jnp.where(low, x, swapped), jnp.where(low, swapped, x)


def _mixer_b_prompt_kernel(x_ref, npre_ref, npost_ref, win_ref, sinks_ref, cos_ref, sin_ref,
                           wout_ref, o_ref, kout_ref, vout_ref,
                           k0_ref, k1_ref, vt0_ref, vt1_ref, att_ref, win_bf_ref, wout_bf_ref,
                           *, tm, tiles_per_seq):
    @pl.when(pl.program_id(0) == 0)
    def _():
        _cast_weights(win_bf_ref, win_ref)
        _cast_weights(wout_bf_ref, wout_ref)

    win_ref, wout_ref = win_bf_ref, wout_bf_ref
    first = (pl.program_id(0) % tiles_per_seq) == 0

    @pl.when(first)
    def _():
        for ref in (k0_ref, k1_ref):
            ref[0:WINDOW, :] = jnp.zeros((WINDOW, V7X_LANES), BF16)
        for ref in (vt0_ref, vt1_ref):
            ref[:, 0:WINDOW] = jnp.zeros((V7X_LANES, WINDOW), BF16)

    x = x_ref[...]
    h = _rms(x, npre_ref[...]).astype(BF16)
    lane = lax.broadcasted_iota(jnp.int32, (tm, V7X_LANES), 1)
    cos, sin = _expand_rope(cos_ref[...], sin_ref[...])

    k = _rope(_dot(h, win_ref[:, QD:QD + KD]), cos, sin)
    v = _dot(h, win_ref[:, QD + KD:QD + 2 * KD])
    kout_ref[...] = k[tm - WINDOW:, :]
    vout_ref[...] = v[tm - WINDOW:, :]
    kd0, kd1 = _dup_kv_halves(k, lane)
    k0_ref[WINDOW:, :] = kd0.astype(BF16)
    k1_ref[WINDOW:, :] = kd1.astype(BF16)
    vt = v.T
    vt0_ref[:, WINDOW:] = jnp.concatenate([vt[:HEAD_DIM], vt[:HEAD_DIM]], axis=0).astype(BF16)
    vt1_ref[:, WINDOW:] = jnp.concatenate([vt[HEAD_DIM:], vt[HEAD_DIM:]], axis=0).astype(BF16)

    ci = lax.broadcasted_iota(jnp.int32, (2 * WINDOW, 2 * WINDOW), 0)
    qi = lax.broadcasted_iota(jnp.int32, (2 * WINDOW, 2 * WINDOW), 1) % WINDOW
    band = (ci >= qi) & (ci <= qi + WINDOW)
    bias = jnp.where(band, 0.0, -jnp.inf)
    bias_first = jnp.where(band & (ci >= jnp.where(first, WINDOW, 0)), 0.0, -jnp.inf)
    head_lane = lax.broadcasted_iota(jnp.int32, (1, 2 * WINDOW), 1) < WINDOW
    low_lanes = lax.broadcasted_iota(jnp.int32, (WINDOW, V7X_LANES), 1) < HEAD_DIM
    top_rows = lax.broadcasted_iota(jnp.int32, (V7X_LANES, WINDOW), 0) < HEAD_DIM
    scale = HEAD_DIM ** -0.5

    n_blocks = tm // WINDOW
    n_pairs = N_HEADS // 2
    chunk = 2 * V7X_LANES
    n_chunks = QD // chunk
    bpg = B_PROJ_ROWS // WINDOW
    n_groups = tm // B_PROJ_ROWS
    items = [(j, p) for j in range(n_blocks) for p in range(n_pairs)]
    q_chunks, gate_chunks, gated, y_chunks = {}, {}, {}, {}

    def block_rows(j):
        return slice(j * WINDOW, (j + 1) * WINDOW)

    def group_rows(g):
        return slice(g * B_PROJ_ROWS, (g + 1) * B_PROJ_ROWS)

    def project_q(g, c):
        q_chunks[g, c] = _dot(h[group_rows(g)], win_ref[:, c * chunk:(c + 1) * chunk])

    def project_gate(g, c):
        c0 = QD + 2 * KD + c * chunk
        gate_chunks[g, c] = _silu(_dot(h[group_rows(g)], win_ref[:, c0:c0 + chunk]))

    def project_out(g, c):
        if c == 0:
            gated[g] = jnp.concatenate(
                [att_ref[group_rows(g), cc * chunk:(cc + 1) * chunk] * gate_chunks.pop((g, cc))
                 for cc in range(n_chunks)], axis=1).astype(BF16)
        y_chunks[g, c] = _dot(gated[g], wout_ref[:, c * chunk:(c + 1) * chunk])
        if c == n_chunks - 1:
            y = jnp.concatenate([y_chunks.pop((g, cc)) for cc in range(n_chunks)], axis=1)
            rows = group_rows(g)
            o_ref[rows, :] = x[rows] + _rms(y, npost_ref[...])

    def scores(j, p):
        rows = block_rows(j)
        kref = k0_ref if (2 * p) // Q_PER_KV == 0 else k1_ref
        half = (p % 2) * V7X_LANES
        r0 = (j % bpg) * WINDOW
        qp = q_chunks[j // bpg, p // 2][r0:r0 + WINDOW, half:half + V7X_LANES]
        qp = _rope(qp, cos[rows], sin[rows]) * scale
        qs = jnp.concatenate([jnp.where(low_lanes, qp, 0.0), jnp.where(low_lanes, 0.0, qp)],
                             axis=0).astype(BF16)
        st = lax.dot_general(kref[j * WINDOW:(j + 2) * WINDOW, :], qs, (((1,), (1,)), ((), ())),
                             preferred_element_type=F32)
        return st + (bias_first if j == 0 else bias)

    def attend(j, p, st):
        vtref = vt0_ref if (2 * p) // Q_PER_KV == 0 else vt1_ref
        sink = jnp.where(head_lane, sinks_ref[0, 2 * p], sinks_ref[0, 2 * p + 1])
        m = jnp.maximum(jnp.max(st, axis=0, keepdims=True), sink)
        e = jnp.exp(st - m)
        inv = 1.0 / (jnp.sum(e, axis=0, keepdims=True) + jnp.exp(sink - m))
        return _dot(vtref[:, j * WINDOW:(j + 2) * WINDOW], e.astype(BF16)), inv

    def finish(j, p, ot, inv):
        ot = ot * inv
        pair = jnp.where(top_rows, ot[:, :WINDOW], ot[:, WINDOW:])
        att_ref[block_rows(j), p * V7X_LANES:(p + 1) * V7X_LANES] = pair.T

    items_per_group = bpg * n_pairs
    side = {}
    for g in range(n_groups):
        tasks = []
        for c in range(n_chunks):
            if g + 1 < n_groups:
                tasks.append((project_q, g + 1, c))
            tasks.append((project_gate, g, c))
            if g > 0:
                tasks.append((project_out, g - 1, c))
        for t, task in enumerate(tasks):
            at = g * items_per_group + (t * items_per_group) // len(tasks)
            side.setdefault(at, []).append(task)

    for c in range(n_chunks):
        project_q(0, c)
    st_next = scores(*items[0])
    pending = None
    for i, (j, p) in enumerate(items):
        st = st_next
        for fn, g, c in side.get(i, []):
            if fn is not project_out:
                fn(g, c)
        if i + 1 < len(items):
            st_next = scores(*items[i + 1])
        ot, inv = attend(j, p, st)
        if pending is not None:
            finish(*pending)
        pending = (j, p, ot, inv)
        for fn, g, c in side.get(i, []):
            if fn is project_out:
                fn(g, c)
    finish(*pending)
    for c in range(n_chunks):
        project_out(n_groups - 1, c)

    for ref in (k0_ref, k1_ref):
        ref[0:WINDOW, :] = ref[tm:tm + WINDOW, :]
    for ref in (vt0_ref, vt1_ref):
        ref[:, 0:WINDOW] = ref[:, tm:tm + WINDOW]


def _mixer_b_prompt(x, npre, npost, win, sinks, cos, sin, wout, *, tm, seq):
    n = x.shape[0]
    tiles_per_seq = seq // tm
    n_seq = n // seq
    row_spec = pl.BlockSpec((tm, D_MODEL), lambda i: (i, 0))
    rope_spec = pl.BlockSpec((tm, HEAD_DIM // 2), lambda i: (i % tiles_per_seq, 0))
    kv_spec = pl.BlockSpec((WINDOW, KD), lambda i: (i // tiles_per_seq, 0))
    return pl.pallas_call(
        functools.partial(_mixer_b_prompt_kernel, tm=tm, tiles_per_seq=tiles_per_seq),
        grid=(n // tm,),
        in_specs=[row_spec, _const_spec((1, D_MODEL)), _const_spec((1, D_MODEL)),
                  _const_spec((D_MODEL, 2 * QD + 2 * KD)),
                  pl.BlockSpec(memory_space=pltpu.SMEM),
                  rope_spec, rope_spec, _const_spec((QD, D_MODEL))],
        out_specs=[row_spec, kv_spec, kv_spec],
        out_shape=[jax.ShapeDtypeStruct((n, D_MODEL), F32),
                   jax.ShapeDtypeStruct((n_seq * WINDOW, KD), F32),
                   jax.ShapeDtypeStruct((n_seq * WINDOW, KD), F32)],
        scratch_shapes=[pltpu.VMEM((tm + WINDOW, V7X_LANES), BF16)] * 2
        + [pltpu.VMEM((V7X_LANES, tm + WINDOW), BF16)] * 2 + [pltpu.VMEM((tm, QD), F32)]
        + [pltpu.VMEM(win.shape, BF16), pltpu.VMEM(wout.shape, BF16)],
        compiler_params=_params(),
        name="mixer_b_prompt",
    )(x, npre, npost, win, sinks, cos, sin, wout)


def _mixer_b_sample_proj_kernel(x_ref, npre_ref, win_ref, cos_ref, sin_ref,
                                q_ref, k_ref, v_ref, kt_ref, vt_ref, gate_ref):
    n = x_ref.shape[0]
    h = _rms(x_ref[...], npre_ref[...]).astype(BF16)
    cos, sin = _expand_rope(cos_ref[...], sin_ref[...])
    scale = HEAD_DIM ** -0.5
    low_lanes = lax.broadcasted_iota(jnp.int32, (n, V7X_LANES), 1) < HEAD_DIM
    for p in range(N_HEADS // 2):
        qp = _rope(_dot(h, win_ref[:, p * V7X_LANES:(p + 1) * V7X_LANES]), cos, sin) * scale
        q_ref[pl.ds(2 * p, n, stride=N_HEADS), :] = jnp.where(low_lanes, qp, 0.0)
        q_ref[pl.ds(2 * p + 1, n, stride=N_HEADS), :] = jnp.where(
            low_lanes, pltpu.roll(qp, HEAD_DIM, 1), 0.0)
    k = _rope(_dot(h, win_ref[:, QD:QD + KD]), cos, sin)
    v = _dot(h, win_ref[:, QD + KD:QD + 2 * KD])
    k_ref[...] = k
    v_ref[...] = v
    kt_ref[...] = k.T
    vt_ref[...] = v.T
    gate_ref[...] = _silu(_dot(h, win_ref[:, QD + 2 * KD:]))


def _mixer_b_sample_proj(x, npre, win, cos_row, sin_row):
    n = x.shape[0]
    return pl.pallas_call(
        _mixer_b_sample_proj_kernel,
        out_shape=[jax.ShapeDtypeStruct((n * N_HEADS, V7X_LANES), F32),
                   jax.ShapeDtypeStruct((n, KD), F32), jax.ShapeDtypeStruct((n, KD), F32),
                   jax.ShapeDtypeStruct((KD, n), F32), jax.ShapeDtypeStruct((KD, n), F32),
                   jax.ShapeDtypeStruct((n, QD), F32)],
        compiler_params=pltpu.CompilerParams(vmem_limit_bytes=VMEM_LIMIT_BYTES),
        name="mixer_b_sample_proj",
    )(x, npre, win, cos_row, sin_row)


def _mixer_b_sample_attn_kernel(q_ref, kt_ref, vt_ref, kn_ref, vn_ref, knt_ref, vnt_ref, sinks_ref,
                                o_ref, kto_ref, vto_ref, *, bt):
    w = kt_ref.shape[3]
    n = bt * N_KV_HEADS
    kt = kt_ref[...].reshape(n, HEAD_DIM, w)
    vt = vt_ref[...].reshape(n, HEAD_DIM, w)
    q = q_ref[...].reshape(bt, N_HEADS, V7X_LANES)
    kn, vn = kn_ref[...], vn_ref[...]

    qb = q.astype(BF16)
    q_kv = qb.reshape(n, Q_PER_KV, V7X_LANES)[:, :, :HEAD_DIM]
    s_c = jnp.einsum("nhd,nds->nhs", q_kv, kt.astype(BF16),
                     preferred_element_type=F32).reshape(bt, N_HEADS, w)
    first_kv = lax.broadcasted_iota(jnp.int32, (bt, N_HEADS, V7X_LANES), 1) < Q_PER_KV
    kn_b = kn.astype(BF16).astype(F32)
    vn_b = vn.astype(BF16).astype(F32)
    k_sel = jnp.where(first_kv, kn_b[:, None, :], pltpu.roll(kn_b, HEAD_DIM, 1)[:, None, :])
    v_sel = jnp.where(first_kv, vn_b[:, None, :], pltpu.roll(vn_b, HEAD_DIM, 1)[:, None, :])
    s_n = jnp.sum(qb.astype(F32) * k_sel, axis=-1, keepdims=True)
    sink = sinks_ref[...][None]
    m = jnp.maximum(jnp.maximum(jnp.max(s_c, axis=-1, keepdims=True), s_n), sink)
    e_c = jnp.exp(s_c - m)
    e_n = jnp.exp(s_n - m)
    inv = 1.0 / (jnp.sum(e_c, axis=-1, keepdims=True) + e_n + jnp.exp(sink - m))
    p_c = (e_c * inv).astype(BF16).reshape(n, Q_PER_KV, w)
    o = jnp.einsum("nhs,nds->nhd", p_c, vt.astype(BF16),
                   preferred_element_type=F32).reshape(bt, N_HEADS, HEAD_DIM)
    p_n = (e_n * inv).astype(BF16).astype(F32)
    o = o + p_n * v_sel[:, :, :HEAD_DIM]
    o_ref[...] = jnp.zeros(o_ref.shape, F32)
    o_ref[:, 0:HEAD_DIM] = o.reshape(bt * N_HEADS, HEAD_DIM)

    is_first = pl.program_id(0) == 0
    knt = jnp.where(is_first, knt_ref[:, 0:bt], knt_ref[:, bt:])
    vnt = jnp.where(is_first, vnt_ref[:, 0:bt], vnt_ref[:, bt:])
    last = lax.broadcasted_iota(jnp.int32, (KD, w), 1) == w - 1
    for b in range(bt):
        for src, cols, dst in ((kt_ref, knt, kto_ref), (vt_ref, vnt, vto_ref)):
            old = src[b].reshape(KD, w)
            new = jnp.where(last, jnp.broadcast_to(cols[:, b:b + 1], (KD, w)),
                            pltpu.roll(old, w - 1, 1))
            dst[b] = new.reshape(N_KV_HEADS, HEAD_DIM, w)


def _mixer_b_sample_attn(q2, kt, vt, kn, vn, knt, vnt, sinks_col, *, bt):
    b, _, _, w = kt.shape
    assert b == 2 * bt
    q_spec = pl.BlockSpec((bt * N_HEADS, V7X_LANES), lambda i: (i, 0))
    c_spec = pl.BlockSpec((bt, N_KV_HEADS, HEAD_DIM, w), lambda i: (i, 0, 0, 0))
    n_spec = pl.BlockSpec((bt, KD), lambda i: (i, 0))
    t_spec = pl.BlockSpec((KD, b), lambda i: (0, 0))
    return pl.pallas_call(
        functools.partial(_mixer_b_sample_attn_kernel, bt=bt),
        grid=(b // bt,),
        in_specs=[q_spec, c_spec, c_spec, n_spec, n_spec, t_spec, t_spec,
                  pl.BlockSpec((N_HEADS, 1), lambda i: (0, 0))],
        out_specs=[q_spec, c_spec, c_spec],
        out_shape=[jax.ShapeDtypeStruct((b * N_HEADS, V7X_LANES), F32),
                   jax.ShapeDtypeStruct(kt.shape, F32), jax.ShapeDtypeStruct(vt.shape, F32)],
        compiler_params=_params(),
        name="mixer_b_sample_attn",
    )(q2, kt, vt, kn, vn, knt, vnt, sinks_col)


def _mixer_b_sample_out_kernel(x_ref, att_ref, gate_ref, wout_ref, npost_ref, o_ref):
    n = x_ref.shape[0]
    low_lanes = lax.broadcasted_iota(jnp.int32, (n, V7X_LANES), 1) < HEAD_DIM
    pairs = []
    for p in range(N_HEADS // 2):
        even = att_ref[pl.ds(2 * p, n, stride=N_HEADS), :]
        odd = att_ref[pl.ds(2 * p + 1, n, stride=N_HEADS), :]
        pairs.append(jnp.where(low_lanes, even, pltpu.roll(odd, HEAD_DIM, 1)))
    att = jnp.concatenate(pairs, axis=1)
    y = _dot((att * gate_ref[...]).astype(BF16), wout_ref[...])
    o_ref[...] = x_ref[...] + _rms(y, npost_ref[...])


def _mixer_b_sample_out(x, att, gate, wout, npost):
    return pl.pallas_call(
        _mixer_b_sample_out_kernel,
        out_shape=jax.ShapeDtypeStruct(x.shape, F32),
        compiler_params=pltpu.CompilerParams(vmem_limit_bytes=VMEM_LIMIT_BYTES),
        name="mixer_b_sample_out",
    )(x, att, gate, wout, npost)


def _rglru_gates(xc, wa_ref, ba_ref, wx_ref, bx_ref, lam_ref):
    xcb = xc.astype(BF16)
    ra, rx = [], []
    for blk in range(C_BLOCKS):
        cols = slice(blk * C_BLOCK_W, (blk + 1) * C_BLOCK_W)
        ra.append(_dot(xcb[:, cols], wa_ref[blk]))
        rx.append(_dot(xcb[:, cols], wx_ref[blk]))
    r = _sigmoid(jnp.concatenate(ra, axis=-1) + ba_ref[...])
    i_gate = _sigmoid(jnp.concatenate(rx, axis=-1) + bx_ref[...])
    log_a = r * (-LRU_C * _softplus(-lam_ref[...]))
    a = jnp.exp(log_a)
    mult = jnp.sqrt(-jnp.tanh(log_a) * (a * a + 1.0))
    return a, mult, i_gate * xc


def _mixer_c_prompt_kernel(x_ref, npre_ref, npost_ref, win_ref, cw_ref, cb_ref, wa_ref, ba_ref,
                           wx_ref, bx_ref, lam_ref, wout_ref, o_ref, conv_ref, hlast_ref,
                           xpad_ref, h_ref, win_bf_ref, wa_bf_ref, wx_bf_ref, wout_bf_ref,
                           *, tm, tiles_per_seq):
    @pl.when(pl.program_id(0) == 0)
    def _():
        for dst, src in ((win_bf_ref, win_ref), (wa_bf_ref, wa_ref), (wx_bf_ref, wx_ref),
                         (wout_bf_ref, wout_ref)):
            _cast_weights(dst, src)

    win_ref, wa_ref, wx_ref, wout_ref = win_bf_ref, wa_bf_ref, wx_bf_ref, wout_bf_ref
    first = (pl.program_id(0) % tiles_per_seq) == 0

    @pl.when(first)
    def _():
        xpad_ref[0:V7X_SUBLANES, :] = jnp.zeros((V7X_SUBLANES, D_RNN), F32)
        h_ref[...] = jnp.zeros((V7X_SUBLANES, D_RNN), F32)

    x = x_ref[...]
    h = _rms(x, npre_ref[...]).astype(BF16)
    xr = _dot(h, win_ref[:, :D_RNN])
    xpad_ref[V7X_SUBLANES:, :] = xr
    conv_ref[...] = xr[tm - V7X_SUBLANES:, :]
    cw = cw_ref[...]
    xc = cb_ref[...] + xr * cw[CONV_W - 1:CONV_W, :]
    for tap in range(CONV_W - 1):
        back = CONV_W - 1 - tap
        xc = xc + xpad_ref[V7X_SUBLANES - back:V7X_SUBLANES - back + tm, :] * cw[tap:tap + 1, :]
    xpad_ref[0:V7X_SUBLANES, :] = xr[tm - V7X_SUBLANES:, :]

    a, mult, gx = _rglru_gates(xc, wa_ref, ba_ref, wx_ref, bx_ref, lam_ref)
    row = lax.broadcasted_iota(jnp.int32, (tm, 1), 0)
    mult = jnp.where(first & (row == 0), 1.0, mult)
    b = mult * gx

    n_groups = tm // V7X_SUBLANES
    a = a.reshape(n_groups, V7X_SUBLANES, D_RNN)
    b = b.reshape(n_groups, V7X_SUBLANES, D_RNN)
    sub = lax.broadcasted_iota(jnp.int32, (n_groups, V7X_SUBLANES, D_RNN), 1)
    for dist in (1, 2, 4):
        keep = sub >= dist
        a_prev = pltpu.roll(a, dist, 1)
        b_prev = pltpu.roll(b, dist, 1)
        b = jnp.where(keep, a * b_prev + b, b)
        a = jnp.where(keep, a * a_prev, a)
    carry = h_ref[V7X_SUBLANES - 1:V7X_SUBLANES, :]
    groups = []
    for gi in range(n_groups):
        hs_g = a[gi] * carry + b[gi]
        carry = hs_g[V7X_SUBLANES - 1:V7X_SUBLANES, :]
        groups.append(hs_g)
    h_ref[...] = groups[-1]
    hlast_ref[...] = groups[-1]
    hs = jnp.concatenate(groups, axis=0)

    gate = _silu(_dot(h, win_ref[:, D_RNN:]))
    y = _dot((hs * gate).astype(BF16), wout_ref[...])
    o_ref[...] = x + _rms(y, npost_ref[...])


def _mixer_c_prompt(x, npre, npost, win, cw, cb, wa, ba, wx, bx, lam, wout, *, tm, seq):
    n = x.shape[0]
    tiles_per_seq = seq // tm
    n_seq = n // seq
    row_spec = pl.BlockSpec((tm, D_MODEL), lambda i: (i, 0))
    tail_spec = pl.BlockSpec((V7X_SUBLANES, D_RNN), lambda i: (i // tiles_per_seq, 0))
    vec = _const_spec((1, D_RNN))
    blk = _const_spec((C_BLOCKS, C_BLOCK_W, C_BLOCK_W))
    return pl.pallas_call(
        functools.partial(_mixer_c_prompt_kernel, tm=tm, tiles_per_seq=tiles_per_seq),
        grid=(n // tm,),
        in_specs=[row_spec, vec, vec, _const_spec((D_MODEL, 2 * D_RNN)),
                  _const_spec((CONV_W, D_RNN)), vec, blk, vec, blk, vec, vec,
                  _const_spec((D_RNN, D_MODEL))],
        out_specs=[row_spec, tail_spec, tail_spec],
        out_shape=[jax.ShapeDtypeStruct((n, D_MODEL), F32),
                   jax.ShapeDtypeStruct((n_seq * V7X_SUBLANES, D_RNN), F32),
                   jax.ShapeDtypeStruct((n_seq * V7X_SUBLANES, D_RNN), F32)],
        scratch_shapes=[pltpu.VMEM((tm + V7X_SUBLANES, D_RNN), F32),
                        pltpu.VMEM((V7X_SUBLANES, D_RNN), F32)]
        + [pltpu.VMEM(w.shape, BF16) for w in (win, wa, wx, wout)],
        compiler_params=_params(),
        name="mixer_c_prompt",
    )(x, npre, npost, win, cw, cb, wa, ba, wx, bx, lam, wout)


def _mixer_c_sample_kernel(x_ref, npre_ref, npost_ref, win_ref, cw_ref, cb_ref, wa_ref, ba_ref,
                           wx_ref, bx_ref, lam_ref, wout_ref, conv_ref, h0_ref,
                           o_ref, conv_out_ref, h_out_ref):
    x = x_ref[...]
    h = _rms(x, npre_ref[...]).astype(BF16)
    xr = _dot(h, win_ref[:, :D_RNN])
    cw = cw_ref[...]
    xc = cb_ref[...] + xr * cw[CONV_W - 1:CONV_W, :]
    for tap in range(CONV_W - 1):
        xc = xc + conv_ref[tap] * cw[tap:tap + 1, :]
        if tap > 0:
            conv_out_ref[tap - 1] = conv_ref[tap]
    conv_out_ref[CONV_W - 2] = xr
    a, mult, gx = _rglru_gates(xc, wa_ref, ba_ref, wx_ref, bx_ref, lam_ref)
    hs = a * h0_ref[...] + mult * gx
    h_out_ref[...] = hs
    gate = _silu(_dot(h, win_ref[:, D_RNN:]))
    y = _dot((hs * gate).astype(BF16), wout_ref[...])
    o_ref[...] = x + _rms(y, npost_ref[...])


def _mixer_c_sample(x, npre, npost, win, cw, cb, wa, ba, wx, bx, lam, wout, conv_t, h0):
    n = x.shape[0]
    return pl.pallas_call(
        _mixer_c_sample_kernel,
        out_shape=[jax.ShapeDtypeStruct((n, D_MODEL), F32),
                   jax.ShapeDtypeStruct((CONV_W - 1, n, D_RNN), F32),
                   jax.ShapeDtypeStruct((n, D_RNN), F32)],
        compiler_params=pltpu.CompilerParams(vmem_limit_bytes=VMEM_LIMIT_BYTES),
        name="mixer_c_sample",
    )(x, npre, npost, win, cw, cb, wa, ba, wx, bx, lam, wout, conv_t, h0)


def kernel(x_prompt, x_sample, cache_b_k, cache_b_v, state_c_conv, state_c_h, norm_pre, norm_post,
           a_w_in, a_ln_g, a_ln_b, a_w_s, a_b_s, a_w_out, b_w_in, b_sinks, b_w_out, c_w_in,
           c_conv_w, c_conv_b, c_w_a, c_b_a, c_w_x, c_b_x, c_lam, c_w_out):
    batch, seq, _ = x_prompt.shape
    dec_batch, dec_seq, _ = x_sample.shape
    past_len = PAST_LEN
    w_buf = cache_b_k.shape[2]
    assert dec_seq == 1 and w_buf == WINDOW and seq % ROW_TILE == 0
    assert ROW_TILE % B_PROJ_ROWS == 0 and B_PROJ_ROWS % WINDOW == 0
    assert ROW_TILE % A_SUB_ROWS == 0 and A_SUB_ROWS % CHUNK == 0

    xp = x_prompt.reshape(batch * seq, D_MODEL)
    xs = x_sample.reshape(dec_batch, D_MODEL)
    row = lambda a: a.reshape(1, -1)

    cos_p, sin_p = _rope_tables(jnp.arange(seq, dtype=jnp.int32))
    cos_s, sin_s = _rope_tables(past_len + jnp.arange(dec_seq, dtype=jnp.int32))

    a_v_s = []
    b_kp, b_vp, b_ks, b_vs = [], [], [], []
    c_cp, c_hp, c_cs, c_hs = [], [], [], []
    for i in range(DEPTH):
        kind, j = i % N_MIXERS, i // N_MIXERS
        npre, npost = row(norm_pre[i]), row(norm_post[i])
        if kind == 0:
            xp, xs, vs = _mixer_a(xp, xs, npre, npost, a_w_in, row(a_ln_g[j]), row(a_ln_b[j]),
                                  a_w_s[j], a_b_s[j].T, a_w_out, layer=j, tm=ROW_TILE)
            a_v_s.append(vs.reshape(dec_batch, dec_seq, D_A))
        elif kind == 1:
            win, wout = b_w_in[j], b_w_out[j]
            xp, kp, vp = _mixer_b_prompt(xp, npre, npost, win, row(b_sinks[j]), cos_p, sin_p, wout,
                                         tm=ROW_TILE, seq=seq)
            b_kp.append(kp.reshape(batch, WINDOW, N_KV_HEADS, HEAD_DIM))
            b_vp.append(vp.reshape(batch, WINDOW, N_KV_HEADS, HEAD_DIM))

            q2, kn, vn, knt, vnt, gate = _mixer_b_sample_proj(xs, npre, win, cos_s, sin_s)
            to_stored = lambda c: jnp.transpose(c, (0, 2, 3, 1))
            om, kt_new, vt_new = _mixer_b_sample_attn(
                q2, to_stored(cache_b_k[j]), to_stored(cache_b_v[j]), kn, vn, knt, vnt,
                b_sinks[j].reshape(N_HEADS, 1), bt=SAMPLE_BATCH_TILE)
            xs = _mixer_b_sample_out(xs, om, gate, wout, npost)
            b_ks.append(jnp.transpose(kt_new, (0, 3, 1, 2)))
            b_vs.append(jnp.transpose(vt_new, (0, 3, 1, 2)))
        else:
            args = (npre, npost, c_w_in[j], c_conv_w[j], row(c_conv_b[j]),
                    c_w_a[j], row(c_b_a[j]), c_w_x[j], row(c_b_x[j]),
                    row(c_lam[j]), c_w_out[j])
            xp, conv_tail, h_tail = _mixer_c_prompt(xp, *args, tm=ROW_TILE, seq=seq)
            c_cp.append(conv_tail.reshape(batch, V7X_SUBLANES, D_RNN)[:, V7X_SUBLANES - (CONV_W - 1):])
            c_hp.append(h_tail.reshape(batch, V7X_SUBLANES, D_RNN)[:, V7X_SUBLANES - 1])
            xs, conv_new, h_new = _mixer_c_sample(
                xs, *args, jnp.transpose(state_c_conv[j], (1, 0, 2)), state_c_h[j])
            c_cs.append(jnp.transpose(conv_new, (1, 0, 2)))
            c_hs.append(h_new)

    return (xp.reshape(batch, seq, D_MODEL), xs.reshape(dec_batch, dec_seq, D_MODEL),
            jnp.stack(a_v_s), jnp.stack(b_kp), jnp.stack(b_vp), jnp.stack(b_ks), jnp.stack(b_vs),
            jnp.stack(c_cp), jnp.stack(c_hp), jnp.stack(c_cs), jnp.stack(c_hs))
```

```python
import functools

import jax
import jax.numpy as jnp
import numpy as np
from jax import lax
from jax.experimental import pallas as pl
from jax.experimental.pallas import tpu as pltpu

D_MODEL = 1024
DEPTH = 4
N_MIXERS = 3
NORM_EPS = 1e-6
LN_EPS = 1e-5

D_A = 2 * D_MODEL
CHUNK = 128
A_GROUPS = 4
A_GROUP_W = D_A // A_GROUPS

HEAD_DIM = 64
N_HEADS = D_MODEL // HEAD_DIM
N_KV_HEADS = N_HEADS // 8
Q_PER_KV = N_HEADS // N_KV_HEADS
WINDOW = 128
ROPE_THETA = 10000.0
QD = N_HEADS * HEAD_DIM
KD = N_KV_HEADS * HEAD_DIM

D_RNN = D_MODEL
C_BLOCKS = 4
C_BLOCK_W = D_RNN // C_BLOCKS
CONV_W = 4
LRU_C = 8.0

PAST_LEN = 8192

V7X_LANES = 128
V7X_SUBLANES = 8
V7X_VMEM_BYTES = 64 * 1024 * 1024
VMEM_LIMIT_BYTES = V7X_VMEM_BYTES - 8 * 1024 * 1024

ROW_TILE = 1024
A_SUB_ROWS = 512
A_STAGE_SLOTS, A_STAGE_ROWS, A_STAGE_COLS = 4, 1024, 256
B_PROJ_ROWS = 256
SAMPLE_BATCH_TILE = 64

BF16 = jnp.bfloat16
F32 = jnp.float32
SQRT_2_OVER_PI = np.float32(np.sqrt(2.0 / np.pi))
GELU_CUBIC = np.float32(np.sqrt(2.0 / np.pi) * 0.044715)


def _dot(a, b):
    return jnp.dot(a, b.astype(BF16), preferred_element_type=F32)


def _cast_weights(dst_ref, src_ref):
    if len(src_ref.shape) == 3:
        for blk in range(src_ref.shape[0]):
            dst_ref[blk] = src_ref[blk].astype(BF16)
    else:
        for c0 in range(0, src_ref.shape[1], 2 * V7X_LANES):
            dst_ref[:, c0:c0 + 2 * V7X_LANES] = src_ref[:, c0:c0 + 2 * V7X_LANES].astype(BF16)


def _rms(x, g):
    return x * lax.rsqrt(jnp.mean(x * x, axis=-1, keepdims=True) + NORM_EPS) * g


def _gelu(x):
    inner = x * (SQRT_2_OVER_PI + GELU_CUBIC * (x * x))
    return x * (0.5 + 0.5 * jnp.tanh(inner))


def _sigmoid(x):
    return 1.0 / (1.0 + jnp.exp(-x))


def _silu(x):
    return x * _sigmoid(x)


def _softplus(x):
    return jnp.maximum(x, 0.0) + jnp.log1p(jnp.exp(-jnp.abs(x)))


def _const_spec(shape):
    zeros = (0,) * len(shape)
    return pl.BlockSpec(shape, lambda i: zeros, pipeline_mode=pl.Buffered(1))


def _params():
    return pltpu.CompilerParams(dimension_semantics=("arbitrary",),
                                vmem_limit_bytes=VMEM_LIMIT_BYTES)


def _layer_norm_a(v, lng_ref, lnb_ref):
    d = v - jnp.mean(v, axis=-1, keepdims=True)
    var = jnp.mean(d * d, axis=-1, keepdims=True)
    return d * lax.rsqrt(var + LN_EPS) * lng_ref[...] + lnb_ref[...]


def _mixer_a_sample_kernel(x_ref, npre_ref, npost_ref, win_ref, lng_ref, lnb_ref, ws_ref, bst_ref,
                           wout_ref, o_ref, v_ref):
    x = x_ref[...]
    h = _rms(x, npre_ref[...]).astype(BF16)
    vn = _layer_norm_a(_gelu(_dot(h, win_ref[:, 2 * D_A:])), lng_ref, lnb_ref)
    v_ref[...] = vn
    acc = jnp.zeros(x.shape, F32)
    for g in range(A_GROUPS):
        c0 = g * A_GROUP_W
        u = _gelu(_dot(h, win_ref[:, 2 * c0:2 * c0 + A_GROUP_W]))
        gate = _silu(_dot(h, win_ref[:, 2 * c0 + A_GROUP_W:2 * c0 + 2 * A_GROUP_W]))
        mixed = ws_ref[g][0:1, 0:1] * vn[:, c0:c0 + A_GROUP_W] + bst_ref[0:1, g:g + 1]
        acc = acc + _dot((u * mixed * gate).astype(BF16), wout_ref[c0:c0 + A_GROUP_W, :])
    o_ref[...] = x + _rms(acc, npost_ref[...])


def _mixer_a_prompt_kernel(x_ref, npre_ref, npost_ref, win_ref, lng_ref, lnb_ref, ws_ref, bst_ref,
                           wout_ref, o_ref, *, tm, sub):
    x = x_ref[...]
    h = _rms(x, npre_ref[...]).astype(BF16)
    row = lax.broadcasted_iota(jnp.int32, (CHUNK, CHUNK), 0)
    col = lax.broadcasted_iota(jnp.int32, (CHUNK, CHUNK), 1)
    causal = row >= col
    n_sub = tm // sub
    items = [(r, g) for r in range(n_sub) for g in range(A_GROUPS)]
    v_parts = {r: [] for r in range(n_sub)}
    vnb, acc, ws_masked = {}, {}, {}

    def sub_rows(r):
        return slice(r * sub, (r + 1) * sub)

    def v_part(r, c):
        c0 = 2 * D_A + c * A_GROUP_W
        v_parts[r].append(_gelu(_dot(h[sub_rows(r)], win_ref[:, c0:c0 + A_GROUP_W])))

    def v_finish(r):
        v = jnp.concatenate(v_parts.pop(r), axis=1)
        vnb[r] = _layer_norm_a(v, lng_ref, lnb_ref).astype(BF16)

    def front(r, g):
        c0 = g * A_GROUP_W
        hr = h[sub_rows(r)]
        ug = _dot(hr, win_ref[:, 2 * c0:2 * c0 + 2 * A_GROUP_W])
        u, gate = ug[:, :A_GROUP_W], ug[:, A_GROUP_W:]
        if g not in ws_masked:
            ws_masked[g] = jnp.where(causal, ws_ref[g], 0.0).astype(BF16)
        mixed = jnp.concatenate(
            [_dot(ws_masked[g], vnb[r][c * CHUNK:(c + 1) * CHUNK, c0:c0 + A_GROUP_W])
             for c in range(sub // CHUNK)], axis=0)
        return u, gate, mixed

    def back(r, g, u, gate, mixed):
        c0 = g * A_GROUP_W
        bias = jnp.concatenate([bst_ref[:, g:g + 1]] * (sub // CHUNK), axis=0)
        z = (_gelu(u) * (mixed + bias) * _silu(gate)).astype(BF16)
        y = _dot(z, wout_ref[c0:c0 + A_GROUP_W, :])
        acc[r] = y if g == 0 else acc[r] + y
        if g == A_GROUPS - 1:
            rows = sub_rows(r)
            o_ref[rows, :] = x[rows] + _rms(acc.pop(r), npost_ref[...])

    for c in range(A_GROUPS):
        v_part(0, c)
    v_finish(0)
    nxt = front(*items[0])
    for k, (r, g) in enumerate(items):
        cur = nxt
        if r + 1 < n_sub:
            v_part(r + 1, g)
            if g == A_GROUPS - 1:
                v_finish(r + 1)
        if k + 1 < len(items):
            nxt = front(*items[k + 1])
        back(r, g, *cur)


def _a_win_col(c):
    part, off = divmod(c, D_A)
    g, within = divmod(off, A_GROUP_W)
    if part == 1:
        return 2 * D_A + off
    return g * 2 * A_GROUP_W + (A_GROUP_W if part == 2 else 0) + within


def _stage_weights(hbm_ref, layer, dst_ref, stage_ref, sem_ref, dst_col=lambda c: c):
    rows, cols = dst_ref.shape
    n_slots, piece_rows, piece_cols = stage_ref.shape
    ahead = n_slots - 1
    pieces = [(r0, c0) for r0 in range(0, rows, piece_rows) for c0 in range(0, cols, piece_cols)]

    def copy(k):
        r0, c0 = pieces[k]
        src = hbm_ref.at[layer, pl.ds(r0, piece_rows), pl.ds(c0, piece_cols)]
        return pltpu.make_async_copy(src, stage_ref.at[k % n_slots], sem_ref.at[k % n_slots])

    for k in range(min(ahead, len(pieces))):
        copy(k).start()
    for k, (r0, c0) in enumerate(pieces):
        if k + ahead < len(pieces):
            copy(k + ahead).start()
        copy(k).wait()
        d0 = dst_col(c0)
        dst_ref[r0:r0 + piece_rows, d0:d0 + piece_cols] = stage_ref[k % n_slots].astype(BF16)


def _mixer_a_kernel(x_ref, xs_ref, npre_ref, npost_ref, win_hbm, lng_ref, lnb_ref, ws_ref, bst_ref,
                    wout_hbm, o_ref, os_ref, vs_ref, win_ref, wout_ref, stage_ref, sem_ref,
                    *, layer, tm, sub, n_tiles):
    step = pl.program_id(0)

    @pl.when(step == 0)
    def _():
        _stage_weights(win_hbm, layer, win_ref, stage_ref, sem_ref, dst_col=_a_win_col)
        _stage_weights(wout_hbm, layer, wout_ref, stage_ref, sem_ref)

    @pl.when(step < n_tiles)
    def _():
        _mixer_a_prompt_kernel(x_ref, npre_ref, npost_ref, win_ref, lng_ref, lnb_ref, ws_ref,
                               bst_ref, wout_ref, o_ref, tm=tm, sub=sub)

    @pl.when(step == n_tiles)
    def _():
        _mixer_a_sample_kernel(xs_ref, npre_ref, npost_ref, win_ref, lng_ref, lnb_ref, ws_ref,
                               bst_ref, wout_ref, os_ref, vs_ref)


def _mixer_a(x, xs, npre, npost, win, lng, lnb, ws, bst, wout, *, layer, tm):
    n, ns = x.shape[0], xs.shape[0]
    n_tiles = n // tm
    row_spec = pl.BlockSpec((tm, D_MODEL), lambda i: (jnp.minimum(i, n_tiles - 1), 0))
    hbm_spec = pl.BlockSpec(memory_space=pl.ANY)
    return pl.pallas_call(
        functools.partial(_mixer_a_kernel, layer=layer, tm=tm, sub=A_SUB_ROWS, n_tiles=n_tiles),
        grid=(n_tiles + 1,),
        in_specs=[row_spec, _const_spec((ns, D_MODEL)), _const_spec((1, D_MODEL)),
                  _const_spec((1, D_MODEL)), hbm_spec, _const_spec((1, D_A)),
                  _const_spec((1, D_A)), _const_spec((A_GROUPS, CHUNK, CHUNK)),
                  _const_spec((CHUNK, A_GROUPS)), hbm_spec],
        out_specs=[row_spec, pl.BlockSpec((ns, D_MODEL), lambda i: (0, 0)),
                   pl.BlockSpec((ns, D_A), lambda i: (0, 0))],
        out_shape=[jax.ShapeDtypeStruct((n, D_MODEL), F32), jax.ShapeDtypeStruct((ns, D_MODEL), F32),
                   jax.ShapeDtypeStruct((ns, D_A), F32)],
        scratch_shapes=[pltpu.VMEM((D_MODEL, 3 * D_A), BF16), pltpu.VMEM((D_A, D_MODEL), BF16),
                        pltpu.VMEM((A_STAGE_SLOTS, A_STAGE_ROWS, A_STAGE_COLS), F32),
                        pltpu.SemaphoreType.DMA((A_STAGE_SLOTS,))],
        compiler_params=_params(),
        name="mixer_a",
    )(x, xs, npre, npost, win, lng, lnb, ws, bst, wout)


def _rope_tables(positions):
    half = HEAD_DIM // 2
    n = positions.shape[0]
    inv_freq = ROPE_THETA ** (-jnp.arange(half, dtype=F32) / half)
    ang = positions.astype(F32)[:, None] * inv_freq[None, :]
    per_row = V7X_LANES // half
    if n % per_row:
        return jnp.cos(ang), jnp.sin(ang)
    dense = ang.reshape(n // per_row, V7X_LANES)
    cos, sin = lax.optimization_barrier((jnp.cos(dense), jnp.sin(dense)))
    return cos.reshape(n, half), sin.reshape(n, half)


def _expand_rope(cos, sin):
    reps = V7X_LANES // HEAD_DIM
    return (jnp.concatenate([cos, cos] * reps, axis=-1),
            jnp.concatenate([-sin, sin] * reps, axis=-1))


def _rope(x, cos, sin_signed):
    lane = lax.broadcasted_iota(jnp.int32, x.shape, 1)
    first_half = (lane % HEAD_DIM) < (HEAD_DIM // 2)
    rot = jnp.where(first_half, pltpu.roll(x, V7X_LANES - HEAD_DIM // 2, 1),
                    pltpu.roll(x, HEAD_DIM // 2, 1))
    return x * cos + rot * sin_signed


def _dup_kv_halves(x, lane):
    swapped = pltpu.roll(x, HEAD_DIM, 1)
    low = lane < HEAD_DIM
    return jnp.where(low, x, swapped), jnp.where(low, swapped, x)


def _mixer_b_prompt_kernel(x_ref, npre_ref, npost_ref, win_ref, sinks_ref, cos_ref, sin_ref,
                           wout_ref, o_ref, kout_ref, vout_ref,
                           k0_ref, k1_ref, vt0_ref, vt1_ref, att_ref, win_bf_ref, wout_bf_ref,
                           *, tm, tiles_per_seq):
    @pl.when(pl.program_id(0) == 0)
    def _():
        _cast_weights(win_bf_ref, win_ref)
        _cast_weights(wout_bf_ref, wout_ref)

    win_ref, wout_ref = win_bf_ref, wout_bf_ref
    first = (pl.program_id(0) % tiles_per_seq) == 0

    @pl.when(first)
    def _():
        for ref in (k0_ref, k1_ref):
            ref[0:WINDOW, :] = jnp.zeros((WINDOW, V7X_LANES), BF16)
        for ref in (vt0_ref, vt1_ref):
            ref[:, 0:WINDOW] = jnp.zeros((V7X_LANES, WINDOW), BF16)

    x = x_ref[...]
    h = _rms(x, npre_ref[...]).astype(BF16)
    lane = lax.broadcasted_iota(jnp.int32, (tm, V7X_LANES), 1)
    cos, sin = _expand_rope(cos_ref[...], sin_ref[...])

    k = _rope(_dot(h, win_ref[:, QD:QD + KD]), cos, sin)
    v = _dot(h, win_ref[:, QD + KD:QD + 2 * KD])
    kout_ref[...] = k[tm - WINDOW:, :]
    vout_ref[...] = v[tm - WINDOW:, :]
    kd0, kd1 = _dup_kv_halves(k, lane)
    k0_ref[WINDOW:, :] = kd0.astype(BF16)
    k1_ref[WINDOW:, :] = kd1.astype(BF16)
    vt = v.T
    vt0_ref[:, WINDOW:] = jnp.concatenate([vt[:HEAD_DIM], vt[:HEAD_DIM]], axis=0).astype(BF16)
    vt1_ref[:, WINDOW:] = jnp.concatenate([vt[HEAD_DIM:], vt[HEAD_DIM:]], axis=0).astype(BF16)

    ci = lax.broadcasted_iota(jnp.int32, (2 * WINDOW, 2 * WINDOW), 0)
    qi = lax.broadcasted_iota(jnp.int32, (2 * WINDOW, 2 * WINDOW), 1) % WINDOW
    band = (ci >= qi) & (ci <= qi + WINDOW)
    bias = jnp.where(band, 0.0, -jnp.inf)
    bias_first = jnp.where(band & (ci >= jnp.where(first, WINDOW, 0)), 0.0, -jnp.inf)
    head_lane = lax.broadcasted_iota(jnp.int32, (1, 2 * WINDOW), 1) < WINDOW
    low_lanes = lax.broadcasted_iota(jnp.int32, (WINDOW, V7X_LANES), 1) < HEAD_DIM
    top_rows = lax.broadcasted_iota(jnp.int32, (V7X_LANES, WINDOW), 0) < HEAD_DIM
    scale = HEAD_DIM ** -0.5

    n_blocks = tm // WINDOW
    n_pairs = N_HEADS // 2
    chunk = 2 * V7X_LANES
    n_chunks = QD // chunk
    bpg = B_PROJ_ROWS // WINDOW
    n_groups = tm // B_PROJ_ROWS
    items = [(j, p) for j in range(n_blocks) for p in range(n_pairs)]
    q_chunks, gate_chunks, gated, y_chunks = {}, {}, {}, {}

    def block_rows(j):
        return slice(j * WINDOW, (j + 1) * WINDOW)

    def group_rows(g):
        return slice(g * B_PROJ_ROWS, (g + 1) * B_PROJ_ROWS)

    def project_q(g, c):
        q_chunks[g, c] = _dot(h[group_rows(g)], win_ref[:, c * chunk:(c + 1) * chunk])

    def project_gate(g, c):
        c0 = QD + 2 * KD + c * chunk
        gate_chunks[g, c] = _silu(_dot(h[group_rows(g)], win_ref[:, c0:c0 + chunk]))

    def project_out(g, c):
        if c == 0:
            gated[g] = jnp.concatenate(
                [att_ref[group_rows(g), cc * chunk:(cc + 1) * chunk] * gate_chunks.pop((g, cc))
                 for cc in range(n_chunks)], axis=1).astype(BF16)
        y_chunks[g, c] = _dot(gated[g], wout_ref[:, c * chunk:(c + 1) * chunk])
        if c == n_chunks - 1:
            y = jnp.concatenate([y_chunks.pop((g, cc)) for cc in range(n_chunks)], axis=1)
            rows = group_rows(g)
            o_ref[rows, :] = x[rows] + _rms(y, npost_ref[...])

    def scores(j, p):
        rows = block_rows(j)
        kref = k0_ref if (2 * p) // Q_PER_KV == 0 else k1_ref
        half = (p % 2) * V7X_LANES
        r0 = (j % bpg) * WINDOW
        qp = q_chunks[j // bpg, p // 2][r0:r0 + WINDOW, half:half + V7X_LANES]
        qp = _rope(qp, cos[rows], sin[rows]) * scale
        qs = jnp.concatenate([jnp.where(low_lanes, qp, 0.0), jnp.where(low_lanes, 0.0, qp)],
                             axis=0).astype(BF16)
        st = lax.dot_general(kref[j * WINDOW:(j + 2) * WINDOW, :], qs, (((1,), (1,)), ((), ())),
                             preferred_element_type=F32)
        return st + (bias_first if j == 0 else bias)

    def attend(j, p, st):
        vtref = vt0_ref if (2 * p) // Q_PER_KV == 0 else vt1_ref
        sink = jnp.where(head_lane, sinks_ref[0, 2 * p], sinks_ref[0, 2 * p + 1])
        m = jnp.maximum(jnp.max(st, axis=0, keepdims=True), sink)
        e = jnp.exp(st - m)
        inv = 1.0 / (jnp.sum(e, axis=0, keepdims=True) + jnp.exp(sink - m))
        return _dot(vtref[:, j * WINDOW:(j + 2) * WINDOW], e.astype(BF16)), inv

    def finish(j, p, ot, inv):
        ot = ot * inv
        pair = jnp.where(top_rows, ot[:, :WINDOW], ot[:, WINDOW:])
        att_ref[block_rows(j), p * V7X_LANES:(p + 1) * V7X_LANES] = pair.T

    items_per_group = bpg * n_pairs
    side = {}
    for g in range(n_groups):
        tasks = []
        for c in range(n_chunks):
            if g + 1 < n_groups:
                tasks.append((project_q, g + 1, c))
            tasks.append((project_gate, g, c))
            if g > 0:
                tasks.append((project_out, g - 1, c))
        for t, task in enumerate(tasks):
            at = g * items_per_group + (t * items_per_group) // len(tasks)
            side.setdefault(at, []).append(task)

    for c in range(n_chunks):
        project_q(0, c)
    st_next = scores(*items[0])
    pending = None
    for i, (j, p) in enumerate(items):
        st = st_next
        for fn, g, c in side.get(i, []):
            if fn is not project_out:
                fn(g, c)
        if i + 1 < len(items):
            st_next = scores(*items[i + 1])
        ot, inv = attend(j, p, st)
        if pending is not None:
            finish(*pending)
        pending = (j, p, ot, inv)
        for fn, g, c in side.get(i, []):
            if fn is project_out:
                fn(g, c)
    finish(*pending)
    for c in range(n_chunks):
        project_out(n_groups - 1, c)

    for ref in (k0_ref, k1_ref):
        ref[0:WINDOW, :] = ref[tm:tm + WINDOW, :]
    for ref in (vt0_ref, vt1_ref):
        ref[:, 0:WINDOW] = ref[:, tm:tm + WINDOW]


def _mixer_b_prompt(x, npre, npost, win, sinks, cos, sin, wout, *, tm, seq):
    n = x.shape[0]
    tiles_per_seq = seq // tm
    n_seq = n // seq
    row_spec = pl.BlockSpec((tm, D_MODEL), lambda i: (i, 0))
    rope_spec = pl.BlockSpec((tm, HEAD_DIM // 2), lambda i: (i % tiles_per_seq, 0))
    kv_spec = pl.BlockSpec((WINDOW, KD), lambda i: (i // tiles_per_seq, 0))
    return pl.pallas_call(
        functools.partial(_mixer_b_prompt_kernel, tm=tm, tiles_per_seq=tiles_per_seq),
        grid=(n // tm,),
        in_specs=[row_spec, _const_spec((1, D_MODEL)), _const_spec((1, D_MODEL)),
                  _const_spec((D_MODEL, 2 * QD + 2 * KD)),
                  pl.BlockSpec(memory_space=pltpu.SMEM),
                  rope_spec, rope_spec, _const_spec((QD, D_MODEL))],
        out_specs=[row_spec, kv_spec, kv_spec],
        out_shape=[jax.ShapeDtypeStruct((n, D_MODEL), F32),
                   jax.ShapeDtypeStruct((n_seq * WINDOW, KD), F32),
                   jax.ShapeDtypeStruct((n_seq * WINDOW, KD), F32)],
        scratch_shapes=[pltpu.VMEM((tm + WINDOW, V7X_LANES), BF16)] * 2
        + [pltpu.VMEM((V7X_LANES, tm + WINDOW), BF16)] * 2 + [pltpu.VMEM((tm, QD), F32)]
        + [pltpu.VMEM(win.shape, BF16), pltpu.VMEM(wout.shape, BF16)],
        compiler_params=_params(),
        name="mixer_b_prompt",
    )(x, npre, npost, win, sinks, cos, sin, wout)


def _mixer_b_sample_proj_kernel(x_ref, npre_ref, win_ref, cos_ref, sin_ref,
                                q_ref, k_ref, v_ref, kt_ref, vt_ref, gate_ref):
    n = x_ref.shape[0]
    h = _rms(x_ref[...], npre_ref[...]).astype(BF16)
    cos, sin = _expand_rope(cos_ref[...], sin_ref[...])
    scale = HEAD_DIM ** -0.5
    low_lanes = lax.broadcasted_iota(jnp.int32, (n, V7X_LANES), 1) < HEAD_DIM
    for p in range(N_HEADS // 2):
        qp = _rope(_dot(h, win_ref[:, p * V7X_LANES:(p + 1) * V7X_LANES]), cos, sin) * scale
        q_ref[pl.ds(2 * p, n, stride=N_HEADS), :] = jnp.where(low_lanes, qp, 0.0)
        q_ref[pl.ds(2 * p + 1, n, stride=N_HEADS), :] = jnp.where(
            low_lanes, pltpu.roll(qp, HEAD_DIM, 1), 0.0)
    k = _rope(_dot(h, win_ref[:, QD:QD + KD]), cos, sin)
    v = _dot(h, win_ref[:, QD + KD:QD + 2 * KD])
    k_ref[...] = k
    v_ref[...] = v
    kt_ref[...] = k.T
    vt_ref[...] = v.T
    gate_ref[...] = _silu(_dot(h, win_ref[:, QD + 2 * KD:]))


def _mixer_b_sample_proj(x, npre, win, cos_row, sin_row):
    n = x.shape[0]
    return pl.pallas_call(
        _mixer_b_sample_proj_kernel,
        out_shape=[jax.ShapeDtypeStruct((n * N_HEADS, V7X_LANES), F32),
                   jax.ShapeDtypeStruct((n, KD), F32), jax.ShapeDtypeStruct((n, KD), F32),
                   jax.ShapeDtypeStruct((KD, n), F32), jax.ShapeDtypeStruct((KD, n), F32),
                   jax.ShapeDtypeStruct((n, QD), F32)],
        compiler_params=pltpu.CompilerParams(vmem_limit_bytes=VMEM_LIMIT_BYTES),
        name="mixer_b_sample_proj",
    )(x, npre, win, cos_row, sin_row)


def _mixer_b_sample_attn_kernel(q_ref, kt_ref, vt_ref, kn_ref, vn_ref, knt_ref, vnt_ref, sinks_ref,
                                o_ref, kto_ref, vto_ref, *, bt):
    w = kt_ref.shape[3]
    n = bt * N_KV_HEADS
    kt = kt_ref[...].reshape(n, HEAD_DIM, w)
    vt = vt_ref[...].reshape(n, HEAD_DIM, w)
    q = q_ref[...].reshape(bt, N_HEADS, V7X_LANES)
    kn, vn = kn_ref[...], vn_ref[...]

    qb = q.astype(BF16)
    q_kv = qb.reshape(n, Q_PER_KV, V7X_LANES)[:, :, :HEAD_DIM]
    s_c = jnp.einsum("nhd,nds->nhs", q_kv, kt.astype(BF16),
                     preferred_element_type=F32).reshape(bt, N_HEADS, w)
    first_kv = lax.broadcasted_iota(jnp.int32, (bt, N_HEADS, V7X_LANES), 1) < Q_PER_KV
    kn_b = kn.astype(BF16).astype(F32)
    vn_b = vn.astype(BF16).astype(F32)
    k_sel = jnp.where(first_kv, kn_b[:, None, :], pltpu.roll(kn_b, HEAD_DIM, 1)[:, None, :])
    v_sel = jnp.where(first_kv, vn_b[:, None, :], pltpu.roll(vn_b, HEAD_DIM, 1)[:, None, :])
    s_n = jnp.sum(qb.astype(F32) * k_sel, axis=-1, keepdims=True)
    sink = sinks_ref[...][None]
    m = jnp.maximum(jnp.maximum(jnp.max(s_c, axis=-1, keepdims=True), s_n), sink)
    e_c = jnp.exp(s_c - m)
    e_n = jnp.exp(s_n - m)
    inv = 1.0 / (jnp.sum(e_c, axis=-1, keepdims=True) + e_n + jnp.exp(sink - m))
    p_c = (e_c * inv).astype(BF16).reshape(n, Q_PER_KV, w)
    o = jnp.einsum("nhs,nds->nhd", p_c, vt.astype(BF16),
                   preferred_element_type=F32).reshape(bt, N_HEADS, HEAD_DIM)
    p_n = (e_n * inv).astype(BF16).astype(F32)
    o = o + p_n * v_sel[:, :, :HEAD_DIM]
    o_ref[...] = jnp.zeros(o_ref.shape, F32)
    o_ref[:, 0:HEAD_DIM] = o.reshape(bt * N_HEADS, HEAD_DIM)

    is_first = pl.program_id(0) == 0
    knt = jnp.where(is_first, knt_ref[:, 0:bt], knt_ref[:, bt:])
    vnt = jnp.where(is_first, vnt_ref[:, 0:bt], vnt_ref[:, bt:])
    last = lax.broadcasted_iota(jnp.int32, (KD, w), 1) == w - 1
    for b in range(bt):
        for src, cols, dst in ((kt_ref, knt, kto_ref), (vt_ref, vnt, vto_ref)):
            old = src[b].reshape(KD, w)
            new = jnp.where(last, jnp.broadcast_to(cols[:, b:b + 1], (KD, w)),
                            pltpu.roll(old, w - 1, 1))
            dst[b] = new.reshape(N_KV_HEADS, HEAD_DIM, w)


def _mixer_b_sample_attn(q2, kt, vt, kn, vn, knt, vnt, sinks_col, *, bt):
    b, _, _, w = kt.shape
    assert b == 2 * bt
    q_spec = pl.BlockSpec((bt * N_HEADS, V7X_LANES), lambda i: (i, 0))
    c_spec = pl.BlockSpec((bt, N_KV_HEADS, HEAD_DIM, w), lambda i: (i, 0, 0, 0))
    n_spec = pl.BlockSpec((bt, KD), lambda i: (i, 0))
    t_spec = pl.BlockSpec((KD, b), lambda i: (0, 0))
    return pl.pallas_call(
        functools.partial(_mixer_b_sample_attn_kernel, bt=bt),
        grid=(b // bt,),
        in_specs=[q_spec, c_spec, c_spec, n_spec, n_spec, t_spec, t_spec,
                  pl.BlockSpec((N_HEADS, 1), lambda i: (0, 0))],
        out_specs=[q_spec, c_spec, c_spec],
        out_shape=[jax.ShapeDtypeStruct((b * N_HEADS, V7X_LANES), F32),
                   jax.ShapeDtypeStruct(kt.shape, F32), jax.ShapeDtypeStruct(vt.shape, F32)],
        compiler_params=_params(),
        name="mixer_b_sample_attn",
    )(q2, kt, vt, kn, vn, knt, vnt, sinks_col)


def _mixer_b_sample_out_kernel(x_ref, att_ref, gate_ref, wout_ref, npost_ref, o_ref):
    n = x_ref.shape[0]
    low_lanes = lax.broadcasted_iota(jnp.int32, (n, V7X_LANES), 1) < HEAD_DIM
    pairs = []
    for p in range(N_HEADS // 2):
        even = att_ref[pl.ds(2 * p, n, stride=N_HEADS), :]
        odd = att_ref[pl.ds(2 * p + 1, n, stride=N_HEADS), :]
        pairs.append(jnp.where(low_lanes, even, pltpu.roll(odd, HEAD_DIM, 1)))
    att = jnp.concatenate(pairs, axis=1)
    y = _dot((att * gate_ref[...]).astype(BF16), wout_ref[...])
    o_ref[...] = x_ref[...] + _rms(y, npost_ref[...])


def _mixer_b_sample_out(x, att, gate, wout, npost):
    return pl.pallas_call(
        _mixer_b_sample_out_kernel,
        out_shape=jax.ShapeDtypeStruct(x.shape, F32),
        compiler_params=pltpu.CompilerParams(vmem_limit_bytes=VMEM_LIMIT_BYTES),
        name="mixer_b_sample_out",
    )(x, att, gate, wout, npost)


def _rglru_gates(xc, wa_ref, ba_ref, wx_ref, bx_ref, lam_ref):
    xcb = xc.astype(BF16)
    ra, rx = [], []
    for blk in range(C_BLOCKS):
        cols = slice(blk * C_BLOCK_W, (blk + 1) * C_BLOCK_W)
        ra.append(_dot(xcb[:, cols], wa_ref[blk]))
        rx.append(_dot(xcb[:, cols], wx_ref[blk]))
    r = _sigmoid(jnp.concatenate(ra, axis=-1) + ba_ref[...])
    i_gate = _sigmoid(jnp.concatenate(rx, axis=-1) + bx_ref[...])
    log_a = r * (-LRU_C * _softplus(-lam_ref[...]))
    a = jnp.exp(log_a)
    mult = jnp.sqrt(-jnp.tanh(log_a) * (a * a + 1.0))
    return a, mult, i_gate * xc


def _mixer_c_prompt_kernel(x_ref, npre_ref, npost_ref, win_ref, cw_ref, cb_ref, wa_ref, ba_ref,
                           wx_ref, bx_ref, lam_ref, wout_ref, o_ref, conv_ref, hlast_ref,
                           xpad_ref, h_ref, win_bf_ref, wa_bf_ref, wx_bf_ref, wout_bf_ref,
                           *, tm, tiles_per_seq):
    @pl.when(pl.program_id(0) == 0)
    def _():
        for dst, src in ((win_bf_ref, win_ref), (wa_bf_ref, wa_ref), (wx_bf_ref, wx_ref),
                         (wout_bf_ref, wout_ref)):
            _cast_weights(dst, src)

    win_ref, wa_ref, wx_ref, wout_ref = win_bf_ref, wa_bf_ref, wx_bf_ref, wout_bf_ref
    first = (pl.program_id(0) % tiles_per_seq) == 0

    @pl.when(first)
    def _():
        xpad_ref[0:V7X_SUBLANES, :] = jnp.zeros((V7X_SUBLANES, D_RNN), F32)
        h_ref[...] = jnp.zeros((V7X_SUBLANES, D_RNN), F32)

    x = x_ref[...]
    h = _rms(x, npre_ref[...]).astype(BF16)
    xr = _dot(h, win_ref[:, :D_RNN])
    xpad_ref[V7X_SUBLANES:, :] = xr
    conv_ref[...] = xr[tm - V7X_SUBLANES:, :]
    cw = cw_ref[...]
    xc = cb_ref[...] + xr * cw[CONV_W - 1:CONV_W, :]
    for tap in range(CONV_W - 1):
        back = CONV_W - 1 - tap
        xc = xc + xpad_ref[V7X_SUBLANES - back:V7X_SUBLANES - back + tm, :] * cw[tap:tap + 1, :]
    xpad_ref[0:V7X_SUBLANES, :] = xr[tm - V7X_SUBLANES:, :]

    a, mult, gx = _rglru_gates(xc, wa_ref, ba_ref, wx_ref, bx_ref, lam_ref)
    row = lax.broadcasted_iota(jnp.int32, (tm, 1), 0)
    mult = jnp.where(first & (row == 0), 1.0, mult)
    b = mult * gx

    n_groups = tm // V7X_SUBLANES
    a = a.reshape(n_groups, V7X_SUBLANES, D_RNN)
    b = b.reshape(n_groups, V7X_SUBLANES, D_RNN)
    sub = lax.broadcasted_iota(jnp.int32, (n_groups, V7X_SUBLANES, D_RNN), 1)
    for dist in (1, 2, 4):
        keep = sub >= dist
        a_prev = pltpu.roll(a, dist, 1)
        b_prev = pltpu.roll(b, dist, 1)
        b = jnp.where(keep, a * b_prev + b, b)
        a = jnp.where(keep, a * a_prev, a)
    carry = h_ref[V7X_SUBLANES - 1:V7X_SUBLANES, :]
    groups = []
    for gi in range(n_groups):
        hs_g = a[gi] * carry + b[gi]
        carry = hs_g[V7X_SUBLANES - 1:V7X_SUBLANES, :]
        groups.append(hs_g)
    h_ref[...] = groups[-1]
    hlast_ref[...] = groups[-1]
    hs = jnp.concatenate(groups, axis=0)

    gate = _silu(_dot(h, win_ref[:, D_RNN:]))
    y = _dot((hs * gate).astype(BF16), wout_ref[...])
    o_ref[...] = x + _rms(y, npost_ref[...])


def _mixer_c_prompt(x, npre, npost, win, cw, cb, wa, ba, wx, bx, lam, wout, *, tm, seq):
    n = x.shape[0]
    tiles_per_seq = seq // tm
    n_seq = n // seq
    row_spec = pl.BlockSpec((tm, D_MODEL), lambda i: (i, 0))
    tail_spec = pl.BlockSpec((V7X_SUBLANES, D_RNN), lambda i: (i // tiles_per_seq, 0))
    vec = _const_spec((1, D_RNN))
    blk = _const_spec((C_BLOCKS, C_BLOCK_W, C_BLOCK_W))
    return pl.pallas_call(
        functools.partial(_mixer_c_prompt_kernel, tm=tm, tiles_per_seq=tiles_per_seq),
        grid=(n // tm,),
        in_specs=[row_spec, vec, vec, _const_spec((D_MODEL, 2 * D_RNN)),
                  _const_spec((CONV_W, D_RNN)), vec, blk, vec, blk, vec, vec,
                  _const_spec((D_RNN, D_MODEL))],
        out_specs=[row_spec, tail_spec, tail_spec],
        out_shape=[jax.ShapeDtypeStruct((n, D_MODEL), F32),
                   jax.ShapeDtypeStruct((n_seq * V7X_SUBLANES, D_RNN), F32),
                   jax.ShapeDtypeStruct((n_seq * V7X_SUBLANES, D_RNN), F32)],
        scratch_shapes=[pltpu.VMEM((tm + V7X_SUBLANES, D_RNN), F32),
                        pltpu.VMEM((V7X_SUBLANES, D_RNN), F32)]
        + [pltpu.VMEM(w.shape, BF16) for w in (win, wa, wx, wout)],
        compiler_params=_params(),
        name="mixer_c_prompt",
    )(x, npre, npost, win, cw, cb, wa, ba, wx, bx, lam, wout)


def _mixer_c_sample_kernel(x_ref, npre_ref, npost_ref, win_ref, cw_ref, cb_ref, wa_ref, ba_ref,
                           wx_ref, bx_ref, lam_ref, wout_ref, conv_ref, h0_ref,
                           o_ref, conv_out_ref, h_out_ref):
    x = x_ref[...]
    h = _rms(x, npre_ref[...]).astype(BF16)
    xr = _dot(h, win_ref[:, :D_RNN])
    cw = cw_ref[...]
    xc = cb_ref[...] + xr * cw[CONV_W - 1:CONV_W, :]
    for tap in range(CONV_W - 1):
        xc = xc + conv_ref[tap] * cw[tap:tap + 1, :]
        if tap > 0:
            conv_out_ref[tap - 1] = conv_ref[tap]
    conv_out_ref[CONV_W - 2] = xr
    a, mult, gx = _rglru_gates(xc, wa_ref, ba_ref, wx_ref, bx_ref, lam_ref)
    hs = a * h0_ref[...] + mult * gx
    h_out_ref[...] = hs
    gate = _silu(_dot(h, win_ref[:, D_RNN:]))
    y = _dot((hs * gate).astype(BF16), wout_ref[...])
    o_ref[...] = x + _rms(y, npost_ref[...])


def _mixer_c_sample(x, npre, npost, win, cw, cb, wa, ba, wx, bx, lam, wout, conv_t, h0):
    n = x.shape[0]
    return pl.pallas_call(
        _mixer_c_sample_kernel,
        out_shape=[jax.ShapeDtypeStruct((n, D_MODEL), F32),
                   jax.ShapeDtypeStruct((CONV_W - 1, n, D_RNN), F32),
                   jax.ShapeDtypeStruct((n, D_RNN), F32)],
        compiler_params=pltpu.CompilerParams(vmem_limit_bytes=VMEM_LIMIT_BYTES),
        name="mixer_c_sample",
    )(x, npre, npost, win, cw, cb, wa, ba, wx, bx, lam, wout, conv_t, h0)


def kernel(x_prompt, x_sample, cache_b_k, cache_b_v, state_c_conv, state_c_h, norm_pre, norm_post,
           a_w_in, a_ln_g, a_ln_b, a_w_s, a_b_s, a_w_out, b_w_in, b_sinks, b_w_out, c_w_in,
           c_conv_w, c_conv_b, c_w_a, c_b_a, c_w_x, c_b_x, c_lam, c_w_out):
    batch, seq, _ = x_prompt.shape
    dec_batch, dec_seq, _ = x_sample.shape
    past_len = PAST_LEN
    w_buf = cache_b_k.shape[2]
    assert dec_seq == 1 and w_buf == WINDOW and seq % ROW_TILE == 0
    assert ROW_TILE % B_PROJ_ROWS == 0 and B_PROJ_ROWS % WINDOW == 0
    assert ROW_TILE % A_SUB_ROWS == 0 and A_SUB_ROWS % CHUNK == 0

    xp = x_prompt.reshape(batch * seq, D_MODEL)
    xs = x_sample.reshape(dec_batch, D_MODEL)
    row = lambda a: a.reshape(1, -1)

    cos_p, sin_p = _rope_tables(jnp.arange(seq, dtype=jnp.int32))
    cos_s, sin_s = _rope_tables(past_len + jnp.arange(dec_seq, dtype=jnp.int32))

    a_v_s = []
    b_kp, b_vp, b_ks, b_vs = [], [], [], []
    c_cp, c_hp, c_cs, c_hs = [], [], [], []
    for i in range(DEPTH):
        kind, j = i % N_MIXERS, i // N_MIXERS
        npre, npost = row(norm_pre[i]), row(norm_post[i])
        if kind == 0:
            xp, xs, vs = _mixer_a(xp, xs, npre, npost, a_w_in, row(a_ln_g[j]), row(a_ln_b[j]),
                                  a_w_s[j], a_b_s[j].T, a_w_out, layer=j, tm=ROW_TILE)
            a_v_s.append(vs.reshape(dec_batch, dec_seq, D_A))
        elif kind == 1:
            win, wout = b_w_in[j], b_w_out[j]
            xp, kp, vp = _mixer_b_prompt(xp, npre, npost, win, row(b_sinks[j]), cos_p, sin_p, wout,
                                         tm=ROW_TILE, seq=seq)
            b_kp.append(kp.reshape(batch, WINDOW, N_KV_HEADS, HEAD_DIM))
            b_vp.append(vp.reshape(batch, WINDOW, N_KV_HEADS, HEAD_DIM))

            q2, kn, vn, knt, vnt, gate = _mixer_b_sample_proj(xs, npre, win, cos_s, sin_s)
            to_stored = lambda c: jnp.transpose(c, (0, 2, 3, 1))
            om, kt_new, vt_new = _mixer_b_sample_attn(
                q2, to_stored(cache_b_k[j]), to_stored(cache_b_v[j]), kn, vn, knt, vnt,
                b_sinks[j].reshape(N_HEADS, 1), bt=SAMPLE_BATCH_TILE)
            xs = _mixer_b_sample_out(xs, om, gate, wout, npost)
            b_ks.append(jnp.transpose(kt_new, (0, 3, 1, 2)))
            b_vs.append(jnp.transpose(vt_new, (0, 3, 1, 2)))
        else:
            args = (npre, npost, c_w_in[j], c_conv_w[j], row(c_conv_b[j]),
                    c_w_a[j], row(c_b_a[j]), c_w_x[j], row(c_b_x[j]),
                    row(c_lam[j]), c_w_out[j])
            xp, conv_tail, h_tail = _mixer_c_prompt(xp, *args, tm=ROW_TILE, seq=seq)
            c_cp.append(conv_tail.reshape(batch, V7X_SUBLANES, D_RNN)[:, V7X_SUBLANES - (CONV_W - 1):])
            c_hp.append(h_tail.reshape(batch, V7X_SUBLANES, D_RNN)[:, V7X_SUBLANES - 1])
            xs, conv_new, h_new = _mixer_c_sample(
                xs, *args, jnp.transpose(state_c_conv[j], (1, 0, 2)), state_c_h[j])
            c_cs.append(jnp.transpose(conv_new, (1, 0, 2)))
            c_hs.append(h_new)

    return (xp.reshape(batch, seq, D_MODEL), xs.reshape(dec_batch, dec_seq, D_MODEL),
            jnp.stack(a_v_s), jnp.stack(b_kp), jnp.stack(b_vp), jnp.stack(b_ks), jnp.stack(b_vs),
            jnp.stack(c_cp), jnp.stack(c_hp), jnp.stack(c_cs), jnp.stack(c_hs))
```

```python
import functools

import jax
import jax.numpy as jnp
import numpy as np
from jax import lax
from jax.experimental import pallas as pl
from jax.experimental.pallas import tpu as pltpu

D_MODEL = 1024
DEPTH = 4
N_MIXERS = 3
NORM_EPS = 1e-6
LN_EPS = 1e-5

D_A = 2 * D_MODEL
CHUNK = 128
A_GROUPS = 4
A_GROUP_W = D_A // A_GROUPS

HEAD_DIM = 64
N_HEADS = D_MODEL // HEAD_DIM
N_KV_HEADS = N_HEADS // 8
Q_PER_KV = N_HEADS // N_KV_HEADS
WINDOW = 128
ROPE_THETA = 10000.0
QD = N_HEADS * HEAD_DIM
KD = N_KV_HEADS * HEAD_DIM

D_RNN = D_MODEL
C_BLOCKS = 4
C_BLOCK_W = D_RNN // C_BLOCKS
CONV_W = 4
LRU_C = 8.0

PAST_LEN = 8192

V7X_LANES = 128
V7X_SUBLANES = 8
V7X_VMEM_BYTES = 64 * 1024 * 1024
VMEM_LIMIT_BYTES = V7X_VMEM_BYTES - 8 * 1024 * 1024

ROW_TILE = 1024
A_SUB_ROWS = 512
A_STAGE_SLOTS, A_STAGE_ROWS, A_STAGE_COLS = 4, 1024, 256
B_PROJ_ROWS = 256
SAMPLE_BATCH_TILE = 64

BF16 = jnp.bfloat16
F32 = jnp.float32
SQRT_2_OVER_PI = np.float32(np.sqrt(2.0 / np.pi))
GELU_CUBIC = np.float32(np.sqrt(2.0 / np.pi) * 0.044715)


def _dot(a, b):
    return jnp.dot(a, b.astype(BF16), preferred_element_type=F32)


def _cast_weights(dst_ref, src_ref):
    if len(src_ref.shape) == 3:
        for blk in range(src_ref.shape[0]):
            dst_ref[blk] = src_ref[blk].astype(BF16)
    else:
        for c0 in range(0, src_ref.shape[1], 2 * V7X_LANES):
            dst_ref[:, c0:c0 + 2 * V7X_LANES] = src_ref[:, c0:c0 + 2 * V7X_LANES].astype(BF16)


def _rms(x, g):
    return x * lax.rsqrt(jnp.mean(x * x, axis=-1, keepdims=True) + NORM_EPS) * g


def _gelu(x):
    inner = x * (SQRT_2_OVER_PI + GELU_CUBIC * (x * x))
    return x * (0.5 + 0.5 * jnp.tanh(inner))


def _sigmoid(x):
    return 1.0 / (1.0 + jnp.exp(-x))


def _silu(x):
    return x * _sigmoid(x)


def _softplus(x):
    return jnp.maximum(x, 0.0) + jnp.log1p(jnp.exp(-jnp.abs(x)))


def _const_spec(shape):
    zeros = (0,) * len(shape)
    return pl.BlockSpec(shape, lambda i: zeros, pipeline_mode=pl.Buffered(1))


def _params():
    return pltpu.CompilerParams(dimension_semantics=("arbitrary",),
                                vmem_limit_bytes=VMEM_LIMIT_BYTES)


def _layer_norm_a(v, lng_ref, lnb_ref):
    d = v - jnp.mean(v, axis=-1, keepdims=True)
    var = jnp.mean(d * d, axis=-1, keepdims=True)
    return d * lax.rsqrt(var + LN_EPS) * lng_ref[...] + lnb_ref[...]


def _mixer_a_sample_kernel(x_ref, npre_ref, npost_ref, win_ref, lng_ref, lnb_ref, ws_ref, bst_ref,
                           wout_ref, o_ref, v_ref):
    x = x_ref[...]
    h = _rms(x, npre_ref[...]).astype(BF16)
    vn = _layer_norm_a(_gelu(_dot(h, win_ref[:, 2 * D_A:])), lng_ref, lnb_ref)
    v_ref[...] = vn
    acc = jnp.zeros(x.shape, F32)
    for g in range(A_GROUPS):
        c0 = g * A_GROUP_W
        u = _gelu(_dot(h, win_ref[:, 2 * c0:2 * c0 + A_GROUP_W]))
        gate = _silu(_dot(h, win_ref[:, 2 * c0 + A_GROUP_W:2 * c0 + 2 * A_GROUP_W]))
        mixed = ws_ref[g][0:1, 0:1] * vn[:, c0:c0 + A_GROUP_W] + bst_ref[0:1, g:g + 1]
        acc = acc + _dot((u * mixed * gate).astype(BF16), wout_ref[c0:c0 + A_GROUP_W, :])
    o_ref[...] = x + _rms(acc, npost_ref[...])


def _mixer_a_prompt_kernel(x_ref, npre_ref, npost_ref, win_ref, lng_ref, lnb_ref, ws_ref, bst_ref,
                           wout_ref, o_ref, *, tm, sub):
    x = x_ref[...]
    h = _rms(x, npre_ref[...]).astype(BF16)
    row = lax.broadcasted_iota(jnp.int32, (CHUNK, CHUNK), 0)
    col = lax.broadcasted_iota(jnp.int32, (CHUNK, CHUNK), 1)
    causal = row >= col
    n_sub = tm // sub
    items = [(r, g) for r in range(n_sub) for g in range(A_GROUPS)]
    v_parts = {r: [] for r in range(n_sub)}
    vnb, acc, ws_masked = {}, {}, {}

    def sub_rows(r):
        return slice(r * sub, (r + 1) * sub)

    def v_part(r, c):
        c0 = 2 * D_A + c * A_GROUP_W
        v_parts[r].append(_gelu(_dot(h[sub_rows(r)], win_ref[:, c0:c0 + A_GROUP_W])))

    def v_finish(r):
        v = jnp.concatenate(v_parts.pop(r), axis=1)
        vnb[r] = _layer_norm_a(v, lng_ref, lnb_ref).astype(BF16)

    def front(r, g):
        c0 = g * A_GROUP_W
        hr = h[sub_rows(r)]
        ug = _dot(hr, win_ref[:, 2 * c0:2 * c0 + 2 * A_GROUP_W])
        u, gate = ug[:, :A_GROUP_W], ug[:, A_GROUP_W:]
        if g not in ws_masked:
            ws_masked[g] = jnp.where(causal, ws_ref[g], 0.0).astype(BF16)
        mixed = jnp.concatenate(
            [_dot(ws_masked[g], vnb[r][c * CHUNK:(c + 1) * CHUNK, c0:c0 + A_GROUP_W])
             for c in range(sub // CHUNK)], axis=0)
        return u, gate, mixed

    def back(r, g, u, gate, mixed):
        c0 = g * A_GROUP_W
        bias = jnp.concatenate([bst_ref[:, g:g + 1]] * (sub // CHUNK), axis=0)
        z = (_gelu(u) * (mixed + bias) * _silu(gate)).astype(BF16)
        y = _dot(z, wout_ref[c0:c0 + A_GROUP_W, :])
        acc[r] = y if g == 0 else acc[r] + y
        if g == A_GROUPS - 1:
            rows = sub_rows(r)
            o_ref[rows, :] = x[rows] + _rms(acc.pop(r), npost_ref[...])

    for c in range(A_GROUPS):
        v_part(0, c)
    v_finish(0)
    nxt = front(*items[0])
    for k, (r, g) in enumerate(items):
        cur = nxt
        if r + 1 < n_sub:
            v_part(r + 1, g)
            if g == A_GROUPS - 1:
                v_finish(r + 1)
        if k + 1 < len(items):
            nxt = front(*items[k + 1])
        back(r, g, *cur)


def _a_win_col(c):
    part, off = divmod(c, D_A)
    g, within = divmod(off, A_GROUP_W)
    if part == 1:
        return 2 * D_A + off
    return g * 2 * A_GROUP_W + (A_GROUP_W if part == 2 else 0) + within


def _stage_weights(hbm_ref, layer, dst_ref, stage_ref, sem_ref, dst_col=lambda c: c):
    rows, cols = dst_ref.shape
    n_slots, piece_rows, piece_cols = stage_ref.shape
    ahead = n_slots - 1
    pieces = [(r0, c0) for r0 in range(0, rows, piece_rows) for c0 in range(0, cols, piece_cols)]

    def copy(k):
        r0, c0 = pieces[k]
        src = hbm_ref.at[layer, pl.ds(r0, piece_rows), pl.ds(c0, piece_cols)]
        return pltpu.make_async_copy(src, stage_ref.at[k % n_slots], sem_ref.at[k % n_slots])

    for k in range(min(ahead, len(pieces))):
        copy(k).start()
    for k, (r0, c0) in enumerate(pieces):
        if k + ahead < len(pieces):
            copy(k + ahead).start()
        copy(k).wait()
        d0 = dst_col(c0)
        dst_ref[r0:r0 + piece_rows, d0:d0 + piece_cols] = stage_ref[k % n_slots].astype(BF16)


def _mixer_a_kernel(x_ref, xs_ref, npre_ref, npost_ref, win_hbm, lng_ref, lnb_ref, ws_ref, bst_ref,
                    wout_hbm, o_ref, os_ref, vs_ref, win_ref, wout_ref, stage_ref, sem_ref,
                    *, layer, tm, sub, n_tiles):
    step = pl.program_id(0)

    @pl.when(step == 0)
    def _():
        _stage_weights(win_hbm, layer, win_ref, stage_ref, sem_ref, dst_col=_a_win_col)
        _stage_weights(wout_hbm, layer, wout_ref, stage_ref, sem_ref)

    @pl.when(step < n_tiles)
    def _():
        _mixer_a_prompt_kernel(x_ref, npre_ref, npost_ref, win_ref, lng_ref, lnb_ref, ws_ref,
                               bst_ref, wout_ref, o_ref, tm=tm, sub=sub)

    @pl.when(step == n_tiles)
    def _():
        _mixer_a_sample_kernel(xs_ref, npre_ref, npost_ref, win_ref, lng_ref, lnb_ref, ws_ref,
                               bst_ref, wout_ref, os_ref, vs_ref)


def _mixer_a(x, xs, npre, npost, win, lng, lnb, ws, bst, wout, *, layer, tm):
    n, ns = x.shape[0], xs.shape[0]
    n_tiles = n // tm
    row_spec = pl.BlockSpec((tm, D_MODEL), lambda i: (jnp.minimum(i, n_tiles - 1), 0))
    hbm_spec = pl.BlockSpec(memory_space=pl.ANY)
    return pl.pallas_call(
        functools.partial(_mixer_a_kernel, layer=layer, tm=tm, sub=A_SUB_ROWS, n_tiles=n_tiles),
        grid=(n_tiles + 1,),
        in_specs=[row_spec, _const_spec((ns, D_MODEL)), _const_spec((1, D_MODEL)),
                  _const_spec((1, D_MODEL)), hbm_spec, _const_spec((1, D_A)),
                  _const_spec((1, D_A)), _const_spec((A_GROUPS, CHUNK, CHUNK)),
                  _const_spec((CHUNK, A_GROUPS)), hbm_spec],
        out_specs=[row_spec, pl.BlockSpec((ns, D_MODEL), lambda i: (0, 0)),
                   pl.BlockSpec((ns, D_A), lambda i: (0, 0))],
        out_shape=[jax.ShapeDtypeStruct((n, D_MODEL), F32), jax.ShapeDtypeStruct((ns, D_MODEL), F32),
                   jax.ShapeDtypeStruct((ns, D_A), F32)],
        scratch_shapes=[pltpu.VMEM((D_MODEL, 3 * D_A), BF16), pltpu.VMEM((D_A, D_MODEL), BF16),
                        pltpu.VMEM((A_STAGE_SLOTS, A_STAGE_ROWS, A_STAGE_COLS), F32),
                        pltpu.SemaphoreType.DMA((A_STAGE_SLOTS,))],
        compiler_params=_params(),
        name="mixer_a",
    )(x, xs, npre, npost, win, lng, lnb, ws, bst, wout)


def _rope_tables(positions):
    half = HEAD_DIM // 2
    n = positions.shape[0]
    inv_freq = ROPE_THETA ** (-jnp.arange(half, dtype=F32) / half)
    ang = positions.astype(F32)[:, None] * inv_freq[None, :]
    per_row = V7X_LANES // half
    if n % per_row:
        return jnp.cos(ang), jnp.sin(ang)
    dense = ang.reshape(n // per_row, V7X_LANES)
    cos, sin = lax.optimization_barrier((jnp.cos(dense), jnp.sin(dense)))
    return cos.reshape(n, half), sin.reshape(n, half)


def _expand_rope(cos, sin):
    reps = V7X_LANES // HEAD_DIM
    return (jnp.concatenate([cos, cos] * reps, axis=-1),
            jnp.concatenate([-sin, sin] * reps, axis=-1))


def _rope(x, cos, sin_signed):
    lane = lax.broadcasted_iota(jnp.int32, x.shape, 1)
    first_half = (lane % HEAD_DIM) < (HEAD_DIM // 2)
    rot = jnp.where(first_half, pltpu.roll(x, V7X_LANES - HEAD_DIM // 2, 1),
                    pltpu.roll(x, HEAD_DIM // 2, 1))
    return x * cos + rot * sin_signed


def _dup_kv_halves(x, lane):
    swapped = pltpu.roll(x, HEAD_DIM, 1)
    low = lane < HEAD_DIM
    return jnp.where(low, x, swapped), jnp.where(low, swapped, x)


def _mixer_b_prompt_kernel(x_ref, npre_ref, npost_ref, win_ref, sinks_ref, cos_ref, sin_ref,
                           wout_ref, o_ref, kout_ref, vout_ref,
                           k0_ref, k1_ref, vt0_ref, vt1_ref, att_ref, win_bf_ref, wout_bf_ref,
                           *, tm, tiles_per_seq):
    @pl.when(pl.program_id(0) == 0)
    def _():
        _cast_weights(win_bf_ref, win_ref)
        _cast_weights(wout_bf_ref, wout_ref)

    win_ref, wout_ref = win_bf_ref, wout_bf_ref
    first = (pl.program_id(0) % tiles_per_seq) == 0

    @pl.when(first)
    def _():
        for ref in (k0_ref, k1_ref):
            ref[0:WINDOW, :] = jnp.zeros((WINDOW, V7X_LANES), BF16)
        for ref in (vt0_ref, vt1_ref):
            ref[:, 0:WINDOW] = jnp.zeros((V7X_LANES, WINDOW), BF16)

    x = x_ref[...]
    h = _rms(x, npre_ref[...]).astype(BF16)
    lane = lax.broadcasted_iota(jnp.int32, (tm, V7X_LANES), 1)
    cos, sin = _expand_rope(cos_ref[...], sin_ref[...])

    k = _rope(_dot(h, win_ref[:, QD:QD + KD]), cos, sin)
    v = _dot(h, win_ref[:, QD + KD:QD + 2 * KD])
    kout_ref[...] = k[tm - WINDOW:, :]
    vout_ref[...] = v[tm - WINDOW:, :]
    kd0, kd1 = _dup_kv_halves(k, lane)
    k0_ref[WINDOW:, :] = kd0.astype(BF16)
    k1_ref[WINDOW:, :] = kd1.astype(BF16)
    vt = v.T
    vt0_ref[:, WINDOW:] = jnp.concatenate([vt[:HEAD_DIM], vt[:HEAD_DIM]], axis=0).astype(BF16)
    vt1_ref[:, WINDOW:] = jnp.concatenate([vt[HEAD_DIM:], vt[HEAD_DIM:]], axis=0).astype(BF16)

    ci = lax.broadcasted_iota(jnp.int32, (2 * WINDOW, 2 * WINDOW), 0)
    qi = lax.broadcasted_iota(jnp.int32, (2 * WINDOW, 2 * WINDOW), 1) % WINDOW
    band = (ci >= qi) & (ci <= qi + WINDOW)
    bias = jnp.where(band, 0.0, -jnp.inf)
    bias_first = jnp.where(band & (ci >= jnp.where(first, WINDOW, 0)), 0.0, -jnp.inf)
    head_lane = lax.broadcasted_iota(jnp.int32, (1, 2 * WINDOW), 1) < WINDOW
    low_lanes = lax.broadcasted_iota(jnp.int32, (WINDOW, V7X_LANES), 1) < HEAD_DIM
    top_rows = lax.broadcasted_iota(jnp.int32, (V7X_LANES, WINDOW), 0) < HEAD_DIM
    scale = HEAD_DIM ** -0.5

    n_blocks = tm // WINDOW
    n_quads = N_HEADS // 4
    chunk = 2 * V7X_LANES
    n_chunks = QD // chunk
    bpg = B_PROJ_ROWS // WINDOW
    n_groups = tm // B_PROJ_ROWS
    items = [(j, c) for j in range(n_blocks) for c in range(n_quads)]
    q_chunks, gate_chunks, gated, y_chunks = {}, {}, {}, {}
    bias4 = jnp.concatenate([bias, bias], axis=1)
    bias4_first = jnp.concatenate([bias_first, bias_first], axis=1)
    head_of_lane = lax.broadcasted_iota(jnp.int32, (1, 4 * WINDOW), 1) // WINDOW

    def block_rows(j):
        return slice(j * WINDOW, (j + 1) * WINDOW)

    def group_rows(g):
        return slice(g * B_PROJ_ROWS, (g + 1) * B_PROJ_ROWS)

    def project_q(g, c):
        q_chunks[g, c] = _dot(h[group_rows(g)], win_ref[:, c * chunk:(c + 1) * chunk])

    def project_gate(g, c):
        c0 = QD + 2 * KD + c * chunk
        gate_chunks[g, c] = _silu(_dot(h[group_rows(g)], win_ref[:, c0:c0 + chunk]))

    def project_out(g, c):
        if c == 0:
            gated[g] = jnp.concatenate(
                [att_ref[group_rows(g), cc * chunk:(cc + 1) * chunk] * gate_chunks.pop((g, cc))
                 for cc in range(n_chunks)], axis=1).astype(BF16)
        y_chunks[g, c] = _dot(gated[g], wout_ref[:, c * chunk:(c + 1) * chunk])
        if c == n_chunks - 1:
            y = jnp.concatenate([y_chunks.pop((g, cc)) for cc in range(n_chunks)], axis=1)
            rows = group_rows(g)
            o_ref[rows, :] = x[rows] + _rms(y, npost_ref[...])

    def scores(j, c):
        rows = block_rows(j)
        kref = k0_ref if (4 * c) // Q_PER_KV == 0 else k1_ref
        r0 = (j % bpg) * WINDOW
        parts = []
        for half in range(2):
            qp = q_chunks[j // bpg, c][r0:r0 + WINDOW, half * V7X_LANES:(half + 1) * V7X_LANES]
            qp = _rope(qp, cos[rows], sin[rows]) * scale
            parts += [jnp.where(low_lanes, qp, 0.0), jnp.where(low_lanes, 0.0, qp)]
        qs = jnp.concatenate(parts, axis=0).astype(BF16)
        st = lax.dot_general(kref[j * WINDOW:(j + 2) * WINDOW, :], qs, (((1,), (1,)), ((), ())),
                             preferred_element_type=F32)
        return st + (bias4_first if j == 0 else bias4)

    def attend(j, c, st):
        vtref = vt0_ref if (4 * c) // Q_PER_KV == 0 else vt1_ref
        sink = jnp.zeros((1, 4 * WINDOW), F32)
        for hh in range(4):
            sink = jnp.where(head_of_lane == hh, sinks_ref[0, 4 * c + hh], sink)
        m = jnp.maximum(jnp.max(st, axis=0, keepdims=True), sink)
        e = jnp.exp(st - m)
        inv = 1.0 / (jnp.sum(e, axis=0, keepdims=True) + jnp.exp(sink - m))
        return _dot(vtref[:, j * WINDOW:(j + 2) * WINDOW], e.astype(BF16)), inv

    def finish(j, c, ot, inv):
        ot = ot * inv
        for half in range(2):
            lo = ot[:, (2 * half) * WINDOW:(2 * half + 1) * WINDOW]
            hi = ot[:, (2 * half + 1) * WINDOW:(2 * half + 2) * WINDOW]
            p = 2 * c + half
            att_ref[block_rows(j), p * V7X_LANES:(p + 1) * V7X_LANES] = jnp.where(top_rows, lo, hi).T

    items_per_group = bpg * n_quads
    side = {}
    for g in range(n_groups):
        tasks = []
        for c in range(n_chunks):
            if g + 1 < n_groups:
                tasks.append((project_q, g + 1, c))
            tasks.append((project_gate, g, c))
            if g > 0:
                tasks.append((project_out, g - 1, c))
        for t, task in enumerate(tasks):
            at = g * items_per_group + (t * items_per_group) // len(tasks)
            side.setdefault(at, []).append(task)

    for c in range(n_chunks):
        project_q(0, c)
    st_next = scores(*items[0])
    pending = None
    for i, (j, c) in enumerate(items):
        st = st_next
        for fn, g, cc in side.get(i, []):
            if fn is not project_out:
                fn(g, cc)
        if i + 1 < len(items):
            st_next = scores(*items[i + 1])
        ot, inv = attend(j, c, st)
        if pending is not None:
            finish(*pending)
        pending = (j, c, ot, inv)
        for fn, g, cc in side.get(i, []):
            if fn is project_out:
                fn(g, cc)
    finish(*pending)
    for c in range(n_chunks):
        project_out(n_groups - 1, c)

    for ref in (k0_ref, k1_ref):
        ref[0:WINDOW, :] = ref[tm:tm + WINDOW, :]
    for ref in (vt0_ref, vt1_ref):
        ref[:, 0:WINDOW] = ref[:, tm:tm + WINDOW]


def _mixer_b_prompt(x, npre, npost, win, sinks, cos, sin, wout, *, tm, seq):
    n = x.shape[0]
    tiles_per_seq = seq // tm
    n_seq = n // seq
    row_spec = pl.BlockSpec((tm, D_MODEL), lambda i: (i, 0))
    rope_spec = pl.BlockSpec((tm, HEAD_DIM // 2), lambda i: (i % tiles_per_seq, 0))
    kv_spec = pl.BlockSpec((WINDOW, KD), lambda i: (i // tiles_per_seq, 0))
    return pl.pallas_call(
        functools.partial(_mixer_b_prompt_kernel, tm=tm, tiles_per_seq=tiles_per_seq),
        grid=(n // tm,),
        in_specs=[row_spec, _const_spec((1, D_MODEL)), _const_spec((1, D_MODEL)),
                  _const_spec((D_MODEL, 2 * QD + 2 * KD)),
                  pl.BlockSpec(memory_space=pltpu.SMEM),
                  rope_spec, rope_spec, _const_spec((QD, D_MODEL))],
        out_specs=[row_spec, kv_spec, kv_spec],
        out_shape=[jax.ShapeDtypeStruct((n, D_MODEL), F32),
                   jax.ShapeDtypeStruct((n_seq * WINDOW, KD), F32),
                   jax.ShapeDtypeStruct((n_seq * WINDOW, KD), F32)],
        scratch_shapes=[pltpu.VMEM((tm + WINDOW, V7X_LANES), BF16)] * 2
        + [pltpu.VMEM((V7X_LANES, tm + WINDOW), BF16)] * 2 + [pltpu.VMEM((tm, QD), F32)]
        + [pltpu.VMEM(win.shape, BF16), pltpu.VMEM(wout.shape, BF16)],
        compiler_params=_params(),
        name="mixer_b_prompt",
    )(x, npre, npost, win, sinks, cos, sin, wout)


def _mixer_b_sample_proj_kernel(x_ref, npre_ref, win_ref, cos_ref, sin_ref,
                                q_ref, k_ref, v_ref, kt_ref, vt_ref, gate_ref):
    n = x_ref.shape[0]
    h = _rms(x_ref[...], npre_ref[...]).astype(BF16)
    cos, sin = _expand_rope(cos_ref[...], sin_ref[...])
    scale = HEAD_DIM ** -0.5
    low_lanes = lax.broadcasted_iota(jnp.int32, (n, V7X_LANES), 1) < HEAD_DIM
    for p in range(N_HEADS // 2):
        qp = _rope(_dot(h, win_ref[:, p * V7X_LANES:(p + 1) * V7X_LANES]), cos, sin) * scale
        q_ref[pl.ds(2 * p, n, stride=N_HEADS), :] = jnp.where(low_lanes, qp, 0.0)
        q_ref[pl.ds(2 * p + 1, n, stride=N_HEADS), :] = jnp.where(
            low_lanes, pltpu.roll(qp, HEAD_DIM, 1), 0.0)
    k = _rope(_dot(h, win_ref[:, QD:QD + KD]), cos, sin)
    v = _dot(h, win_ref[:, QD + KD:QD + 2 * KD])
    k_ref[...] = k
    v_ref[...] = v
    kt_ref[...] = k.T
    vt_ref[...] = v.T
    gate_ref[...] = _silu(_dot(h, win_ref[:, QD + 2 * KD:]))


def _mixer_b_sample_proj(x, npre, win, cos_row, sin_row):
    n = x.shape[0]
    return pl.pallas_call(
        _mixer_b_sample_proj_kernel,
        out_shape=[jax.ShapeDtypeStruct((n * N_HEADS, V7X_LANES), F32),
                   jax.ShapeDtypeStruct((n, KD), F32), jax.ShapeDtypeStruct((n, KD), F32),
                   jax.ShapeDtypeStruct((KD, n), F32), jax.ShapeDtypeStruct((KD, n), F32),
                   jax.ShapeDtypeStruct((n, QD), F32)],
        compiler_params=pltpu.CompilerParams(vmem_limit_bytes=VMEM_LIMIT_BYTES),
        name="mixer_b_sample_proj",
    )(x, npre, win, cos_row, sin_row)


def _mixer_b_sample_attn_kernel(q_ref, kt_ref, vt_ref, kn_ref, vn_ref, knt_ref, vnt_ref, sinks_ref,
                                o_ref, kto_ref, vto_ref, *, bt):
    w = kt_ref.shape[3]
    n = bt * N_KV_HEADS
    kt = kt_ref[...].reshape(n, HEAD_DIM, w)
    vt = vt_ref[...].reshape(n, HEAD_DIM, w)
    q = q_ref[...].reshape(bt, N_HEADS, V7X_LANES)
    kn, vn = kn_ref[...], vn_ref[...]

    qb = q.astype(BF16)
    q_kv = qb.reshape(n, Q_PER_KV, V7X_LANES)[:, :, :HEAD_DIM]
    s_c = jnp.einsum("nhd,nds->nhs", q_kv, kt.astype(BF16),
                     preferred_element_type=F32).reshape(bt, N_HEADS, w)
    first_kv = lax.broadcasted_iota(jnp.int32, (bt, N_HEADS, V7X_LANES), 1) < Q_PER_KV
    kn_b = kn.astype(BF16).astype(F32)
    vn_b = vn.astype(BF16).astype(F32)
    k_sel = jnp.where(first_kv, kn_b[:, None, :], pltpu.roll(kn_b, HEAD_DIM, 1)[:, None, :])
    v_sel = jnp.where(first_kv, vn_b[:, None, :], pltpu.roll(vn_b, HEAD_DIM, 1)[:, None, :])
    s_n = jnp.sum(qb.astype(F32) * k_sel, axis=-1, keepdims=True)
    sink = sinks_ref[...][None]
    m = jnp.maximum(jnp.maximum(jnp.max(s_c, axis=-1, keepdims=True), s_n), sink)
    e_c = jnp.exp(s_c - m)
    e_n = jnp.exp(s_n - m)
    inv = 1.0 / (jnp.sum(e_c, axis=-1, keepdims=True) + e_n + jnp.exp(sink - m))
    p_c = (e_c * inv).astype(BF16).reshape(n, Q_PER_KV, w)
    o = jnp.einsum("nhs,nds->nhd", p_c, vt.astype(BF16),
                   preferred_element_type=F32).reshape(bt, N_HEADS, HEAD_DIM)
    p_n = (e_n * inv).astype(BF16).astype(F32)
    o = o + p_n * v_sel[:, :, :HEAD_DIM]
    o_ref[...] = jnp.zeros(o_ref.shape, F32)
    o_ref[:, 0:HEAD_DIM] = o.reshape(bt * N_HEADS, HEAD_DIM)

    is_first = pl.program_id(0) == 0
    knt = jnp.where(is_first, knt_ref[:, 0:bt], knt_ref[:, bt:])
    vnt = jnp.where(is_first, vnt_ref[:, 0:bt], vnt_ref[:, bt:])
    last = lax.broadcasted_iota(jnp.int32, (KD, w), 1) == w - 1
    for b in range(bt):
        for src, cols, dst in ((kt_ref, knt, kto_ref), (vt_ref, vnt, vto_ref)):
            old = src[b].reshape(KD, w)
            new = jnp.where(last, jnp.broadcast_to(cols[:, b:b + 1], (KD, w)),
                            pltpu.roll(old, w - 1, 1))
            dst[b] = new.reshape(N_KV_HEADS, HEAD_DIM, w)


def _mixer_b_sample_attn(q2, kt, vt, kn, vn, knt, vnt, sinks_col, *, bt):
    b, _, _, w = kt.shape
    assert b == 2 * bt
    q_spec = pl.BlockSpec((bt * N_HEADS, V7X_LANES), lambda i: (i, 0))
    c_spec = pl.BlockSpec((bt, N_KV_HEADS, HEAD_DIM, w), lambda i: (i, 0, 0, 0))
    n_spec = pl.BlockSpec((bt, KD), lambda i: (i, 0))
    t_spec = pl.BlockSpec((KD, b), lambda i: (0, 0))
    return pl.pallas_call(
        functools.partial(_mixer_b_sample_attn_kernel, bt=bt),
        grid=(b // bt,),
        in_specs=[q_spec, c_spec, c_spec, n_spec, n_spec, t_spec, t_spec,
                  pl.BlockSpec((N_HEADS, 1), lambda i: (0, 0))],
        out_specs=[q_spec, c_spec, c_spec],
        out_shape=[jax.ShapeDtypeStruct((b * N_HEADS, V7X_LANES), F32),
                   jax.ShapeDtypeStruct(kt.shape, F32), jax.ShapeDtypeStruct(vt.shape, F32)],
        compiler_params=_params(),
        name="mixer_b_sample_attn",
    )(q2, kt, vt, kn, vn, knt, vnt, sinks_col)


def _mixer_b_sample_out_kernel(x_ref, att_ref, gate_ref, wout_ref, npost_ref, o_ref):
    n = x_ref.shape[0]
    low_lanes = lax.broadcasted_iota(jnp.int32, (n, V7X_LANES), 1) < HEAD_DIM
    pairs = []
    for p in range(N_HEADS // 2):
        even = att_ref[pl.ds(2 * p, n, stride=N_HEADS), :]
        odd = att_ref[pl.ds(2 * p + 1, n, stride=N_HEADS), :]
        pairs.append(jnp.where(low_lanes, even, pltpu.roll(odd, HEAD_DIM, 1)))
    att = jnp.concatenate(pairs, axis=1)
    y = _dot((att * gate_ref[...]).astype(BF16), wout_ref[...])
    o_ref[...] = x_ref[...] + _rms(y, npost_ref[...])


def _mixer_b_sample_out(x, att, gate, wout, npost):
    return pl.pallas_call(
        _mixer_b_sample_out_kernel,
        out_shape=jax.ShapeDtypeStruct(x.shape, F32),
        compiler_params=pltpu.CompilerParams(vmem_limit_bytes=VMEM_LIMIT_BYTES),
        name="mixer_b_sample_out",
    )(x, att, gate, wout, npost)


def _rglru_gates(xc, wa_ref, ba_ref, wx_ref, bx_ref, lam_ref):
    xcb = xc.astype(BF16)
    ra, rx = [], []
    for blk in range(C_BLOCKS):
        cols = slice(blk * C_BLOCK_W, (blk + 1) * C_BLOCK_W)
        ra.append(_dot(xcb[:, cols], wa_ref[blk]))
        rx.append(_dot(xcb[:, cols], wx_ref[blk]))
    r = _sigmoid(jnp.concatenate(ra, axis=-1) + ba_ref[...])
    i_gate = _sigmoid(jnp.concatenate(rx, axis=-1) + bx_ref[...])
    log_a = r * (-LRU_C * _softplus(-lam_ref[...]))
    a = jnp.exp(log_a)
    mult = jnp.sqrt(-jnp.tanh(log_a) * (a * a + 1.0))
    return a, mult, i_gate * xc


def _mixer_c_prompt_kernel(x_ref, npre_ref, npost_ref, win_ref, cw_ref, cb_ref, wa_ref, ba_ref,
                           wx_ref, bx_ref, lam_ref, wout_ref, o_ref, conv_ref, hlast_ref,
                           xpad_ref, h_ref, win_bf_ref, wa_bf_ref, wx_bf_ref, wout_bf_ref,
                           *, tm, tiles_per_seq):
    @pl.when(pl.program_id(0) == 0)
    def _():
        for dst, src in ((win_bf_ref, win_ref), (wa_bf_ref, wa_ref), (wx_bf_ref, wx_ref),
                         (wout_bf_ref, wout_ref)):
            _cast_weights(dst, src)

    win_ref, wa_ref, wx_ref, wout_ref = win_bf_ref, wa_bf_ref, wx_bf_ref, wout_bf_ref
    first = (pl.program_id(0) % tiles_per_seq) == 0

    @pl.when(first)
    def _():
        xpad_ref[0:V7X_SUBLANES, :] = jnp.zeros((V7X_SUBLANES, D_RNN), F32)
        h_ref[...] = jnp.zeros((V7X_SUBLANES, D_RNN), F32)

    x = x_ref[...]
    h = _rms(x, npre_ref[...]).astype(BF16)
    xr = _dot(h, win_ref[:, :D_RNN])
    xpad_ref[V7X_SUBLANES:, :] = xr
    conv_ref[...] = xr[tm - V7X_SUBLANES:, :]
    cw = cw_ref[...]
    xc = cb_ref[...] + xr * cw[CONV_W - 1:CONV_W, :]
    for tap in range(CONV_W - 1):
        back = CONV_W - 1 - tap
        xc = xc + xpad_ref[V7X_SUBLANES - back:V7X_SUBLANES - back + tm, :] * cw[tap:tap + 1, :]
    xpad_ref[0:V7X_SUBLANES, :] = xr[tm - V7X_SUBLANES:, :]

    a, mult, gx = _rglru_gates(xc, wa_ref, ba_ref, wx_ref, bx_ref, lam_ref)
    row = lax.broadcasted_iota(jnp.int32, (tm, 1), 0)
    mult = jnp.where(first & (row == 0), 1.0, mult)
    b = mult * gx

    n_groups = tm // V7X_SUBLANES
    a = a.reshape(n_groups, V7X_SUBLANES, D_RNN)
    b = b.reshape(n_groups, V7X_SUBLANES, D_RNN)
    sub = lax.broadcasted_iota(jnp.int32, (n_groups, V7X_SUBLANES, D_RNN), 1)
    for dist in (1, 2, 4):
        keep = sub >= dist
        a_prev = pltpu.roll(a, dist, 1)
        b_prev = pltpu.roll(b, dist, 1)
        b = jnp.where(keep, a * b_prev + b, b)
        a = jnp.where(keep, a * a_prev, a)
    carry = h_ref[V7X_SUBLANES - 1:V7X_SUBLANES, :]
    groups = []
    for gi in range(n_groups):
        hs_g = a[gi] * carry + b[gi]
        carry = hs_g[V7X_SUBLANES - 1:V7X_SUBLANES, :]
        groups.append(hs_g)
    h_ref[...] = groups[-1]
    hlast_ref[...] = groups[-1]
    hs = jnp.concatenate(groups, axis=0)

    gate = _silu(_dot(h, win_ref[:, D_RNN:]))
    y = _dot((hs * gate).astype(BF16), wout_ref[...])
    o_ref[...] = x + _rms(y, npost_ref[...])


def _mixer_c_prompt(x, npre, npost, win, cw, cb, wa, ba, wx, bx, lam, wout, *, tm, seq):
    n = x.shape[0]
    tiles_per_seq = seq // tm
    n_seq = n // seq
    row_spec = pl.BlockSpec((tm, D_MODEL), lambda i: (i, 0))
    tail_spec = pl.BlockSpec((V7X_SUBLANES, D_RNN), lambda i: (i // tiles_per_seq, 0))
    vec = _const_spec((1, D_RNN))
    blk = _const_spec((C_BLOCKS, C_BLOCK_W, C_BLOCK_W))
    return pl.pallas_call(
        functools.partial(_mixer_c_prompt_kernel, tm=tm, tiles_per_seq=tiles_per_seq),
        grid=(n // tm,),
        in_specs=[row_spec, vec, vec, _const_spec((D_MODEL, 2 * D_RNN)),
                  _const_spec((CONV_W, D_RNN)), vec, blk, vec, blk, vec, vec,
                  _const_spec((D_RNN, D_MODEL))],
        out_specs=[row_spec, tail_spec, tail_spec],
        out_shape=[jax.ShapeDtypeStruct((n, D_MODEL), F32),
                   jax.ShapeDtypeStruct((n_seq * V7X_SUBLANES, D_RNN), F32),
                   jax.ShapeDtypeStruct((n_seq * V7X_SUBLANES, D_RNN), F32)],
        scratch_shapes=[pltpu.VMEM((tm + V7X_SUBLANES, D_RNN), F32),
                        pltpu.VMEM((V7X_SUBLANES, D_RNN), F32)]
        + [pltpu.VMEM(w.shape, BF16) for w in (win, wa, wx, wout)],
        compiler_params=_params(),
        name="mixer_c_prompt",
    )(x, npre, npost, win, cw, cb, wa, ba, wx, bx, lam, wout)


def _mixer_c_sample_kernel(x_ref, npre_ref, npost_ref, win_ref, cw_ref, cb_ref, wa_ref, ba_ref,
                           wx_ref, bx_ref, lam_ref, wout_ref, conv_ref, h0_ref,
                           o_ref, conv_out_ref, h_out_ref):
    x = x_ref[...]
    h = _rms(x, npre_ref[...]).astype(BF16)
    xr = _dot(h, win_ref[:, :D_RNN])
    cw = cw_ref[...]
    xc = cb_ref[...] + xr * cw[CONV_W - 1:CONV_W, :]
    for tap in range(CONV_W - 1):
        xc = xc + conv_ref[tap] * cw[tap:tap + 1, :]
        if tap > 0:
            conv_out_ref[tap - 1] = conv_ref[tap]
    conv_out_ref[CONV_W - 2] = xr
    a, mult, gx = _rglru_gates(xc, wa_ref, ba_ref, wx_ref, bx_ref, lam_ref)
    hs = a * h0_ref[...] + mult * gx
    h_out_ref[...] = hs
    gate = _silu(_dot(h, win_ref[:, D_RNN:]))
    y = _dot((hs * gate).astype(BF16), wout_ref[...])
    o_ref[...] = x + _rms(y, npost_ref[...])


def _mixer_c_sample(x, npre, npost, win, cw, cb, wa, ba, wx, bx, lam, wout, conv_t, h0):
    n = x.shape[0]
    return pl.pallas_call(
        _mixer_c_sample_kernel,
        out_shape=[jax.ShapeDtypeStruct((n, D_MODEL), F32),
                   jax.ShapeDtypeStruct((CONV_W - 1, n, D_RNN), F32),
                   jax.ShapeDtypeStruct((n, D_RNN), F32)],
        compiler_params=pltpu.CompilerParams(vmem_limit_bytes=VMEM_LIMIT_BYTES),
        name="mixer_c_sample",
    )(x, npre, npost, win, cw, cb, wa, ba, wx, bx, lam, wout, conv_t, h0)


def kernel(x_prompt, x_sample, cache_b_k, cache_b_v, state_c_conv, state_c_h, norm_pre, norm_post,
           a_w_in, a_ln_g, a_ln_b, a_w_s, a_b_s, a_w_out, b_w_in, b_sinks, b_w_out, c_w_in,
           c_conv_w, c_conv_b, c_w_a, c_b_a, c_w_x, c_b_x, c_lam, c_w_out):
    batch, seq, _ = x_prompt.shape
    dec_batch, dec_seq, _ = x_sample.shape
    past_len = PAST_LEN
    w_buf = cache_b_k.shape[2]
    assert dec_seq == 1 and w_buf == WINDOW and seq % ROW_TILE == 0
    assert ROW_TILE % B_PROJ_ROWS == 0 and B_PROJ_ROWS % WINDOW == 0
    assert ROW_TILE % A_SUB_ROWS == 0 and A_SUB_ROWS % CHUNK == 0

    xp = x_prompt.reshape(batch * seq, D_MODEL)
    xs = x_sample.reshape(dec_batch, D_MODEL)
    row = lambda a: a.reshape(1, -1)

    cos_p, sin_p = _rope_tables(jnp.arange(seq, dtype=jnp.int32))
    cos_s, sin_s = _rope_tables(past_len + jnp.arange(dec_seq, dtype=jnp.int32))

    a_v_s = []
    b_kp, b_vp, b_ks, b_vs = [], [], [], []
    c_cp, c_hp, c_cs, c_hs = [], [], [], []
    for i in range(DEPTH):
        kind, j = i % N_MIXERS, i // N_MIXERS
        npre, npost = row(norm_pre[i]), row(norm_post[i])
        if kind == 0:
            xp, xs, vs = _mixer_a(xp, xs, npre, npost, a_w_in, row(a_ln_g[j]), row(a_ln_b[j]),
                                  a_w_s[j], a_b_s[j].T, a_w_out, layer=j, tm=ROW_TILE)
            a_v_s.append(vs.reshape(dec_batch, dec_seq, D_A))
        elif kind == 1:
            win, wout = b_w_in[j], b_w_out[j]
            xp, kp, vp = _mixer_b_prompt(xp, npre, npost, win, row(b_sinks[j]), cos_p, sin_p, wout,
                                         tm=ROW_TILE, seq=seq)
            b_kp.append(kp.reshape(batch, WINDOW, N_KV_HEADS, HEAD_DIM))
            b_vp.append(vp.reshape(batch, WINDOW, N_KV_HEADS, HEAD_DIM))

            q2, kn, vn, knt, vnt, gate = _mixer_b_sample_proj(xs, npre, win, cos_s, sin_s)
            to_stored = lambda c: jnp.transpose(c, (0, 2, 3, 1))
            om, kt_new, vt_new = _mixer_b_sample_attn(
                q2, to_stored(cache_b_k[j]), to_stored(cache_b_v[j]), kn, vn, knt, vnt,
                b_sinks[j].reshape(N_HEADS, 1), bt=SAMPLE_BATCH_TILE)
            xs = _mixer_b_sample_out(xs, om, gate, wout, npost)
            b_ks.append(jnp.transpose(kt_new, (0, 3, 1, 2)))
            b_vs.append(jnp.transpose(vt_new, (0, 3, 1, 2)))
        else:
            args = (npre, npost, c_w_in[j], c_conv_w[j], row(c_conv_b[j]),
                    c_w_a[j], row(c_b_a[j]), c_w_x[j], row(c_b_x[j]),
                    row(c_lam[j]), c_w_out[j])
            xp, conv_tail, h_tail = _mixer_c_prompt(xp, *args, tm=ROW_TILE, seq=seq)
            c_cp.append(conv_tail.reshape(batch, V7X_SUBLANES, D_RNN)[:, V7X_SUBLANES - (CONV_W - 1):])
            c_hp.append(h_tail.reshape(batch, V7X_SUBLANES, D_RNN)[:, V7X_SUBLANES - 1])
            xs, conv_new, h_new = _mixer_c_sample(
                xs, *args, jnp.transpose(state_c_conv[j], (1, 0, 2)), state_c_h[j])
            c_cs.append(jnp.transpose(conv_new, (1, 0, 2)))
            c_hs.append(h_new)

    return (xp.reshape(batch, seq, D_MODEL), xs.reshape(dec_batch, dec_seq, D_MODEL),
            jnp.stack(a_v_s), jnp.stack(b_kp), jnp.stack(b_vp), jnp.stack(b_ks), jnp.stack(b_vs),
            jnp.stack(c_cp), jnp.stack(c_hp), jnp.stack(c_cs), jnp.stack(c_hs))
```
